```python
import math
import jax, jax.numpy as jnp
from jax import lax
import numpy as np

D_MODEL = 1024
BATCH = 8
SEQ = 2048
DEPTH = 2
DEC_BATCH = 128
DEC_SEQ = 1
PAST_LEN = 2048
PAGE_SIZE = 128

HEAD_DIM = 64
HEADS_PER_GROUP = 8
DIL_WINDOWS = (128, 512, 2048)
DIL_RATES = (1, 4, 16)
N_DIL = len(DIL_WINDOWS)
ATTN_BLOCK = 128
ATTN_SCALE = HEAD_DIM ** -0.5
ROT_DIM = HEAD_DIM // 4
ROPE_THETA = 500000.0
ATTN_WIDTH = HEADS_PER_GROUP * HEAD_DIM
QKV_COLS = 3 * N_DIL * ATTN_WIDTH
CONV_CH = D_MODEL // 4
CONV_WIDTH = 31
IN_COLS = QKV_COLS + 2 * CONV_CH
MIX_OUT = ATTN_WIDTH + CONV_CH
POOL_WINDOWS = (2, 4, 8, 16)
POOL_GROUPS = len(POOL_WINDOWS)
POOL_CH = D_MODEL // POOL_GROUPS
POOL_PREFIX = max(POOL_WINDOWS) - 1
N_EXPERTS = 16
N_EXPERT_GROUPS = 4
EXPERTS_PER_GROUP = N_EXPERTS // N_EXPERT_GROUPS
TOP_K = 2
D_EXPERT = 256
N_EVEN = (DEPTH + 1) // 2
N_ODD = DEPTH // 2
DN_ALPHA = (2.0 * DEPTH) ** 0.25
DN_BETA = (8.0 * DEPTH) ** -0.25
LN_EPS = 1e-5

kernel_name = 'hybrid_dilated_conv_pool_moe_decoder_step'


def layer_norm(x, g, b):
    xf = x.astype(jnp.float32)
    mu = xf.mean(-1, keepdims=True)
    var = jnp.square(xf - mu).mean(-1, keepdims=True)
    return ((xf - mu) * lax.rsqrt(var + LN_EPS) * g + b).astype(x.dtype)


def rope_partial(x, pos):
    half = ROT_DIM // 2
    inv_freq = ROPE_THETA ** (-jnp.arange(half, dtype=jnp.float32) * 2.0 / ROT_DIM)
    ang = pos.astype(jnp.float32)[:, None] * inv_freq[None, :]
    cos = jnp.cos(ang)[None, :, None, None, :]
    sin = jnp.sin(ang)[None, :, None, None, :]
    xr = x[..., :ROT_DIM].astype(jnp.float32)
    x1, x2 = xr[..., :half], xr[..., half:]
    rot = jnp.concatenate([x1 * cos - x2 * sin, x2 * cos + x1 * sin], axis=-1)
    return jnp.concatenate([rot.astype(x.dtype), x[..., ROT_DIM:]], axis=-1)


def masked_softmax_lse(scores, mask):
    scores = jnp.where(mask, scores, -jnp.inf)
    m = scores.max(-1, keepdims=True)
    p = jnp.exp(scores - m)
    l = p.sum(-1, keepdims=True)
    return p / l, (m + jnp.log(l))[..., 0]


def dilated_attn_prompt(q, k, v, rate, n_win):
    n, s, h, e = q.shape
    L = s // rate
    nb = -(-L // ATTN_BLOCK)
    Lp = nb * ATTN_BLOCK

    def by_residue(t):
        t = t.reshape(n, L, rate, h, e).transpose(0, 2, 1, 3, 4)
        return jnp.pad(t, ((0, 0), (0, 0), (0, Lp - L), (0, 0), (0, 0)))

    def band(t):
        tp = jnp.pad(t, ((0, 0), (0, 0), (ATTN_BLOCK, 0), (0, 0), (0, 0)))
        tp = tp.reshape(n, rate, nb + 1, ATTN_BLOCK, h, e)
        return jnp.concatenate([tp[:, :, :-1], tp[:, :, 1:]], axis=3)

    qb = by_residue(q).reshape(n, rate, nb, ATTN_BLOCK, h, e)
    kb = band(by_residue(k))
    vb = band(by_residue(v))
    qi = jnp.arange(ATTN_BLOCK)[:, None]
    ki = jnp.arange(2 * ATTN_BLOCK)[None, :]
    dist = qi + ATTN_BLOCK - ki
    key_idx = jnp.arange(nb)[:, None, None] * ATTN_BLOCK + ki[None] - ATTN_BLOCK
    mask = ((dist >= 0) & (dist <= n_win))[None] & (key_idx >= 0)
    scores = jnp.einsum('nrbqhe,nrbkhe->nrbhqk', qb, kb,
                        preferred_element_type=jnp.float32) * ATTN_SCALE
    probs, lse = masked_softmax_lse(scores, mask[None, None, :, None])
    o = jnp.einsum('nrbhqk,nrbkhe->nrbqhe', probs.astype(v.dtype), vb)
    o = o.reshape(n, rate, Lp, h, e)[:, :, :L].transpose(0, 2, 1, 3, 4).reshape(n, s, h, e)
    lse = lse.transpose(0, 1, 2, 4, 3).reshape(n, rate, Lp, h)[:, :, :L]
    lse = lse.transpose(0, 2, 1, 3).reshape(n, s, h)
    return o, lse


def dilated_attn_decode(q, k_ext, v_ext, rate, n_win, n_prefix):
    t = q.shape[1]
    idx = n_prefix + jnp.arange(t)[:, None] - rate * jnp.arange(n_win + 1)[None, :]
    valid = idx >= 0
    idx = jnp.maximum(idx, 0)
    kg = k_ext[:, idx]
    vg = v_ext[:, idx]
    scores = jnp.einsum('nthe,ntjhe->nthj', q, kg,
                        preferred_element_type=jnp.float32) * ATTN_SCALE
    probs, lse = masked_softmax_lse(scores, valid[None, :, None, :])
    o = jnp.einsum('nthj,ntjhe->nthe', probs.astype(vg.dtype), vg)
    return o, lse


def merge_by_denominator(outs, lses):
    o = jnp.stack(outs, 0)
    w = jax.nn.softmax(jnp.stack(lses, 0), axis=0)
    return jnp.einsum('gnth,gnthe->nthe', w.astype(o.dtype), o)


def conv_module(g_ext, conv_w, conv_b, ln_g, ln_b):
    y = lax.conv_general_dilated(g_ext, conv_w.astype(g_ext.dtype)[:, None, :], window_strides=(1,),
                                 padding='VALID', dimension_numbers=('NWC', 'WIO', 'NWC'),
                                 feature_group_count=CONV_CH) + conv_b
    return jax.nn.silu(layer_norm(y, ln_g, ln_b))


def mixer_ab(x, pos, attn_prefix, conv_prefix, w_in, b_in, conv_w, conv_b, cln_g, cln_b, w_out, decode):
    n, t, _ = x.shape
    proj = x @ w_in + b_in
    qkv = proj[..., :QKV_COLS].reshape(n, t, 3, N_DIL, HEADS_PER_GROUP, HEAD_DIM)
    q = rope_partial(qkv[:, :, 0], pos)
    k = rope_partial(qkv[:, :, 1], pos)
    v = qkv[:, :, 2]
    ga, gb = jnp.split(proj[..., QKV_COLS:], 2, axis=-1)
    glu = ga * jax.nn.sigmoid(gb)
    outs, lses, new_kv = [], [], []
    for g in range(N_DIL):
        window, rate = DIL_WINDOWS[g], DIL_RATES[g]
        n_win = window // rate
        kg, vg = k[:, :, g], v[:, :, g]
        if decode:
            prefix = attn_prefix[g]
            w_len = prefix.shape[1]
            k_ext = jnp.concatenate([prefix[:, :, 0], kg], axis=1)
            v_ext = jnp.concatenate([prefix[:, :, 1], vg], axis=1)
            o, l = dilated_attn_decode(q[:, :, g], k_ext, v_ext, rate, n_win, w_len)
            keep = min(window, w_len + t)
            new_kv.append(jnp.stack([k_ext, v_ext], axis=2)[:, w_len + t - keep:])
        else:
            o, l = dilated_attn_prompt(q[:, :, g], kg, vg, rate, n_win)
            keep = min(window, t)
            new_kv.append(jnp.stack([kg, vg], axis=2)[:, t - keep:])
        outs.append(o)
        lses.append(l)
    attn = merge_by_denominator(outs, lses).reshape(n, t, ATTN_WIDTH)
    if decode:
        g_ext = jnp.concatenate([conv_prefix, glu], axis=1)
    else:
        g_ext = jnp.pad(glu, ((0, 0), (CONV_WIDTH - 1, 0), (0, 0)))
    conv = conv_module(g_ext, conv_w, conv_b, cln_g, cln_b)
    y = jnp.concatenate([attn, conv], axis=-1) @ w_out
    return y, new_kv, g_ext[:, -(CONV_WIDTH - 1):]


def mixer_pool(x, pos, prefix, pool_w, pool_b, pool_scale):
    n, t, d = x.shape
    u_ext = jnp.concatenate([prefix, x], axis=1)
    cs = jnp.cumsum(jnp.pad(u_ext.astype(jnp.float32), ((0, 0), (1, 0), (0, 0))), axis=1)
    end = cs[:, POOL_PREFIX + 1:]
    xf = x.astype(jnp.float32)
    parts = []
    for gi, w in enumerate(POOL_WINDOWS):
        lo, hi = gi * POOL_CH, (gi + 1) * POOL_CH
        start = cs[:, POOL_PREFIX + 1 - w:POOL_PREFIX + 1 - w + t, lo:hi]
        cnt = jnp.minimum(w, pos + 1).astype(jnp.float32)[None, :, None]
        parts.append((end[..., lo:hi] - start) / cnt - xf[..., lo:hi])
    p = jnp.concatenate(parts, axis=-1).astype(x.dtype).reshape(n, t, POOL_GROUPS, POOL_CH)
    h = jnp.einsum('ntgc,gce->ntge', p, pool_w) + pool_b
    y = h.reshape(n, t, d) * pool_scale
    return y, u_ext[:, -POOL_PREFIX:]


def moe_ffn(h, router_w, router_bias, w1, w3, w2):
    n, t, d = h.shape
    tok = h.reshape(n * t, d)
    logits = jnp.dot(tok, router_w, preferred_element_type=jnp.float32)
    scores = jax.nn.softmax(logits, axis=-1)
    sel = scores + router_bias.astype(jnp.float32)
    grp = lax.top_k(sel.reshape(-1, N_EXPERT_GROUPS, EXPERTS_PER_GROUP), TOP_K)[0].sum(-1)
    gmask = jnp.argmax(grp, axis=-1)[:, None] == jnp.arange(N_EXPERT_GROUPS)[None, :]
    emask = jnp.repeat(gmask, EXPERTS_PER_GROUP, axis=-1)
    _, eidx = lax.top_k(jnp.where(emask, sel, -jnp.inf), TOP_K)
    gate = jnp.take_along_axis(scores, eidx, axis=-1)
    gate = gate / gate.sum(-1, keepdims=True)
    combine = (jax.nn.one_hot(eidx, N_EXPERTS, dtype=jnp.float32) * gate[..., None]).sum(1)
    a = jnp.einsum('md,edf->mef', tok, w1)
    b = jnp.einsum('md,edf->mef', tok, w3)
    g = jax.nn.silu(a) * b * combine.astype(tok.dtype)[..., None]
    return jnp.einsum('mef,efd->md', g, w2).reshape(n, t, d)


def trunk(h, pos, attn_caches, conv_cache, pool_cache, p, decode):
    new_attn = [[] for _ in range(N_DIL)]
    new_conv, new_pool = [], []
    for layer in range(DEPTH):
        if layer % 2 == 0:
            e = layer // 2
            prefixes = [c[e] for c in attn_caches] if decode else None
            cprefix = conv_cache[e] if decode else None
            mix, kv, cst = mixer_ab(h, pos, prefixes, cprefix, p['w_in'][e], p['b_in'][e], p['conv_w'][e],
                                    p['conv_b'][e], p['conv_ln_g'][e], p['conv_ln_b'][e], p['w_out'][e], decode)
            for g in range(N_DIL):
                new_attn[g].append(kv[g])
            new_conv.append(cst)
        else:
            o = layer // 2
            prefix = pool_cache[o] if decode else jnp.zeros((h.shape[0], POOL_PREFIX, D_MODEL), h.dtype)
            mix, pst = mixer_pool(h, pos, prefix, p['pool_w'][o], p['pool_b'][o], p['pool_scale'][o])
            new_pool.append(pst)
        h = layer_norm(DN_ALPHA * h + mix, p['ln_mix_g'][layer], p['ln_mix_b'][layer])
        ffn = moe_ffn(h, p['router_w'], p['router_bias'], p['moe_w1'][layer], p['moe_w3'][layer], p['moe_w2'][layer])
        h = layer_norm(DN_ALPHA * h + ffn, p['ln_ffn_g'][layer], p['ln_ffn_b'][layer])
    attn_states = [jnp.stack(a, 0) for a in new_attn]
    return h, attn_states, jnp.stack(new_conv, 0), jnp.stack(new_pool, 0)


def setup_inputs(seed: int = 0) -> dict:
    key = jax.random.key(seed)
    ks = list(jax.random.split(key, 32))

    def nrm(i, shape, scale):
        return jax.random.normal(ks[i], shape, jnp.float32) * scale

    past = [min(w, PAST_LEN) for w in DIL_WINDOWS]
    kv_tail = (2, HEADS_PER_GROUP, HEAD_DIM)
    return {
        'x_prompt': nrm(0, (BATCH, SEQ, D_MODEL), 1.0),
        'x_sample': nrm(1, (DEC_BATCH, DEC_SEQ, D_MODEL), 1.0),
        'cache_attn_w128': nrm(2, (N_EVEN, DEC_BATCH, past[0]) + kv_tail, 1.0),
        'cache_attn_w512': nrm(3, (N_EVEN, DEC_BATCH, past[1]) + kv_tail, 1.0),
        'cache_attn_w2048': nrm(4, (N_EVEN, DEC_BATCH, past[2]) + kv_tail, 1.0),
        'state_conv': nrm(5, (N_EVEN, DEC_BATCH, CONV_WIDTH - 1, CONV_CH), 0.5),
        'state_pool': nrm(6, (N_ODD, DEC_BATCH, POOL_PREFIX, D_MODEL), 1.0),
        'w_in': nrm(7, (N_EVEN, D_MODEL, IN_COLS), D_MODEL ** -0.5),
        'b_in': nrm(8, (N_EVEN, IN_COLS), 0.02),
        'conv_w': nrm(9, (N_EVEN, CONV_WIDTH, CONV_CH), CONV_WIDTH ** -0.5),
        'conv_b': nrm(10, (N_EVEN, CONV_CH), 0.02),
        'conv_ln_g': 1.0 + nrm(11, (N_EVEN, CONV_CH), 0.05),
        'conv_ln_b': nrm(12, (N_EVEN, CONV_CH), 0.02),
        'w_out': nrm(13, (N_EVEN, MIX_OUT, D_MODEL), MIX_OUT ** -0.5 * DN_BETA),
        'pool_w': nrm(14, (N_ODD, POOL_GROUPS, POOL_CH, POOL_CH), POOL_CH ** -0.5 * DN_BETA),
        'pool_b': nrm(15, (N_ODD, POOL_GROUPS, POOL_CH), 0.02),
        'pool_scale': 1.0 + nrm(16, (N_ODD, D_MODEL), 0.1),
        'ln_mix_g': 1.0 + nrm(17, (DEPTH, D_MODEL), 0.05),
        'ln_mix_b': nrm(18, (DEPTH, D_MODEL), 0.02),
        'ln_ffn_g': 1.0 + nrm(19, (DEPTH, D_MODEL), 0.05),
        'ln_ffn_b': nrm(20, (DEPTH, D_MODEL), 0.02),
        'router_w': nrm(21, (D_MODEL, N_EXPERTS), D_MODEL ** -0.5),
        'router_bias': nrm(22, (N_EXPERTS,), 0.01),
        'moe_w1': nrm(23, (DEPTH, N_EXPERTS, D_MODEL, D_EXPERT), D_MODEL ** -0.5),
        'moe_w3': nrm(24, (DEPTH, N_EXPERTS, D_MODEL, D_EXPERT), D_MODEL ** -0.5),
        'moe_w2': nrm(25, (DEPTH, N_EXPERTS, D_EXPERT, D_MODEL), D_EXPERT ** -0.5 * DN_BETA),
    }


def reference(x_prompt, x_sample, cache_attn_w128, cache_attn_w512, cache_attn_w2048, state_conv, state_pool,
              w_in, b_in, conv_w, conv_b, conv_ln_g, conv_ln_b, w_out, pool_w, pool_b, pool_scale,
              ln_mix_g, ln_mix_b, ln_ffn_g, ln_ffn_b, router_w, router_bias, moe_w1, moe_w3, moe_w2):
    params = dict(w_in=w_in, b_in=b_in, conv_w=conv_w, conv_b=conv_b, conv_ln_g=conv_ln_g, conv_ln_b=conv_ln_b,
                  w_out=w_out, pool_w=pool_w, pool_b=pool_b, pool_scale=pool_scale, ln_mix_g=ln_mix_g,
                  ln_mix_b=ln_mix_b, ln_ffn_g=ln_ffn_g, ln_ffn_b=ln_ffn_b, router_w=router_w,
                  router_bias=router_bias, moe_w1=moe_w1, moe_w3=moe_w3, moe_w2=moe_w2)
    pos_prompt = jnp.arange(x_prompt.shape[1], dtype=jnp.int32)
    pos_sample = PAST_LEN + jnp.arange(x_sample.shape[1], dtype=jnp.int32)
    y_prompt, p_attn, p_conv, p_pool = trunk(x_prompt, pos_prompt, None, None, None, params, False)
    y_sample, s_attn, s_conv, s_pool = trunk(x_sample, pos_sample,
                                             (cache_attn_w128, cache_attn_w512, cache_attn_w2048),
                                             state_conv, state_pool, params, True)
    return (y_prompt, y_sample, p_attn[0], p_attn[1], p_attn[2], p_conv, p_pool,
            s_attn[0], s_attn[1], s_attn[2], s_conv, s_pool)
```

```python
import functools
import math

import jax
import jax.numpy as jnp
from jax import lax
from jax.experimental import pallas as pl
from jax.experimental.pallas import tpu as pltpu

F32 = jnp.float32
BF16 = jnp.bfloat16

D_MODEL = 1024
HEAD_DIM = 64
HEADS = 8
ATTN_WIDTH = HEADS * HEAD_DIM
N_DIL = 3
DIL_WINDOWS = (128, 512, 2048)
DIL_RATES = (1, 4, 16)
ATTN_BLOCK = 128
ATTN_SCALE = HEAD_DIM ** -0.5
ROT_DIM = HEAD_DIM // 4
ROPE_THETA = 500000.0
QKV_COLS = 3 * N_DIL * ATTN_WIDTH
CONV_CH = D_MODEL // 4
CONV_WIDTH = 31
IN_COLS = QKV_COLS + 2 * CONV_CH
POOL_WINDOWS = (2, 4, 8, 16)
POOL_CH = D_MODEL // len(POOL_WINDOWS)
POOL_PREFIX = max(POOL_WINDOWS) - 1
N_EXPERTS = 16
N_EXPERT_GROUPS = 4
EXPERTS_PER_GROUP = 4
D_EXPERT = 256
DEPTH = 2
DN_ALPHA = (2.0 * DEPTH) ** 0.25
LN_EPS = 1e-5

LANES = 128
VMEM_LIMIT_BYTES = 56 * 1024 * 1024
ROW_TILE = 512
CONV_HALO = 32
POOL_HALO = 16
DEC_SEQ_BLOCK = 8


def _params(sem):
    return pltpu.CompilerParams(dimension_semantics=sem, vmem_limit_bytes=VMEM_LIMIT_BYTES)


def _const_spec(shape):
    nd = len(shape)
    return pl.BlockSpec(shape, lambda *_: (0,) * nd)


def _resident_spec(shape):
    nd = len(shape)
    return pl.BlockSpec(shape, lambda *_: (0,) * nd, pipeline_mode=pl.Buffered(1))


def _layer_norm(x, g, b):
    mu = jnp.mean(x, axis=-1, keepdims=True)
    xc = x - mu
    var = jnp.mean(xc * xc, axis=-1, keepdims=True)
    return xc * lax.rsqrt(var + LN_EPS) * g + b


def _bf16_round(x):
    return x.astype(BF16).astype(F32)


def _proj_kernel(x_ref, w_ref, b_ref, tab_ref, q_ref, kv0_ref, kv1_ref, kv2_ref, glu_ref):
    xb = x_ref[...].astype(BF16)
    cosm, sin_lo, sin_hi = tab_ref[0], tab_ref[1], tab_ref[2]

    def proj(c0, width):
        return (jnp.dot(xb, w_ref[:, c0:c0 + width], preferred_element_type=F32)
                + b_ref[:, c0:c0 + width])

    def rope(t):
        parts = []
        for j in range(t.shape[1] // LANES):
            v = t[:, j * LANES:(j + 1) * LANES]
            parts.append(v * cosm
                         + pltpu.roll(v, LANES - ROT_DIM // 2, 1) * sin_lo
                         + pltpu.roll(v, ROT_DIM // 2, 1) * sin_hi)
        return jnp.concatenate(parts, axis=1)

    kv_refs = (kv0_ref, kv1_ref, kv2_ref)
    for g in range(N_DIL):
        c = g * ATTN_WIDTH
        q = rope(proj(c, ATTN_WIDTH)) * ATTN_SCALE
        q_ref[:, c:c + ATTN_WIDTH] = q.astype(BF16)
        kv_refs[g][:, 0:ATTN_WIDTH] = rope(proj(N_DIL * ATTN_WIDTH + c, ATTN_WIDTH))
        kv_refs[g][:, ATTN_WIDTH:2 * ATTN_WIDTH] = proj(2 * N_DIL * ATTN_WIDTH + c, ATTN_WIDTH)
    ga = proj(QKV_COLS, CONV_CH)
    gb = proj(QKV_COLS + CONV_CH, CONV_CH)
    glu_ref[...] = ga * jax.nn.sigmoid(gb)


def _project(x2d, w_bf, b2d, tables, tm, tiles_per_seq):
    m = x2d.shape[0]
    row = lambda i: (i, 0)
    return pl.pallas_call(
        _proj_kernel,
        grid=(m // tm,),
        in_specs=[
            pl.BlockSpec((tm, D_MODEL), row),
            _resident_spec((D_MODEL, IN_COLS)),
            _const_spec((1, IN_COLS)),
            pl.BlockSpec((3, tm, LANES), lambda i: (0, i % tiles_per_seq, 0)),
        ],
        out_specs=[
            pl.BlockSpec((tm, N_DIL * ATTN_WIDTH), row),
            pl.BlockSpec((tm, 2 * ATTN_WIDTH), row),
            pl.BlockSpec((tm, 2 * ATTN_WIDTH), row),
            pl.BlockSpec((tm, 2 * ATTN_WIDTH), row),
            pl.BlockSpec((tm, CONV_CH), row),
        ],
        out_shape=[
            jax.ShapeDtypeStruct((m, N_DIL * ATTN_WIDTH), BF16),
            jax.ShapeDtypeStruct((m, 2 * ATTN_WIDTH), F32),
            jax.ShapeDtypeStruct((m, 2 * ATTN_WIDTH), F32),
            jax.ShapeDtypeStruct((m, 2 * ATTN_WIDTH), F32),
            jax.ShapeDtypeStruct((m, CONV_CH), F32),
        ],
        compiler_params=_params(("parallel",)),
        name="in_proj",
    )(x2d, w_bf, b2d, tables)


def _rope_tables(pos):
    half = ROT_DIM // 2
    t = pos.shape[0]
    inv_freq = ROPE_THETA ** (-jnp.arange(half, dtype=F32) * 2.0 / ROT_DIM)
    ang = pos.astype(F32)[:, None] * inv_freq[None, :]
    cos, sin = jnp.cos(ang), jnp.sin(ang)
    rest = HEAD_DIM - ROT_DIM
    c64 = jnp.concatenate([cos, cos, jnp.ones((t, rest), F32)], axis=1)
    lo64 = jnp.concatenate([-sin, jnp.zeros((t, HEAD_DIM - half), F32)], axis=1)
    hi64 = jnp.concatenate([jnp.zeros((t, half), F32), sin, jnp.zeros((t, rest), F32)], axis=1)
    rep = LANES // HEAD_DIM
    return jnp.stack([jnp.tile(c64, (1, rep)), jnp.tile(lo64, (1, rep)), jnp.tile(hi64, (1, rep))])


def _attn_kernel(q_ref, kc_ref, kp_ref, o_ref, lse_ref):
    lb = pl.program_id(2)
    qi = lax.broadcasted_iota(jnp.int32, (ATTN_BLOCK, ATTN_BLOCK), 0)
    ci = lax.broadcasted_iota(jnp.int32, (ATTN_BLOCK, ATTN_BLOCK), 1)
    mask_cur = ci <= qi
    mask_prev = jnp.logical_and(ci >= qi, lb > 0)
    nt = (((1,), (1,)), ((), ()))
    for h in range(HEADS):
        lo, hi = h * HEAD_DIM, (h + 1) * HEAD_DIM
        qh = q_ref[0, :, lo:hi]
        kc = kc_ref[0, :, lo:hi].astype(BF16)
        kp = kp_ref[0, :, lo:hi].astype(BF16)
        vc = kc_ref[0, :, ATTN_WIDTH + lo:ATTN_WIDTH + hi].astype(BF16)
        vp = kp_ref[0, :, ATTN_WIDTH + lo:ATTN_WIDTH + hi].astype(BF16)
        sc = lax.dot_general(qh, kc, nt, preferred_element_type=F32)
        sp = lax.dot_general(qh, kp, nt, preferred_element_type=F32)
        sc = jnp.where(mask_cur, sc, -jnp.inf)
        sp = jnp.where(mask_prev, sp, -jnp.inf)
        m = jnp.maximum(jnp.max(sc, axis=-1, keepdims=True), jnp.max(sp, axis=-1, keepdims=True))
        pc = jnp.exp(sc - m)
        pp = jnp.exp(sp - m)
        l = jnp.sum(pc, axis=-1, keepdims=True) + jnp.sum(pp, axis=-1, keepdims=True)
        inv = 1.0 / l
        o = (jnp.dot((pc * inv).astype(BF16), vc, preferred_element_type=F32)
             + jnp.dot((pp * inv).astype(BF16), vp, preferred_element_type=F32))
        o_ref[0, :, lo:hi] = o
        lse_ref[0, :, lo:hi] = jnp.broadcast_to(m + jnp.log(l), (ATTN_BLOCK, HEAD_DIM))


def _prompt_attention(q, kv, g, n, s):
    rate = DIL_RATES[g]
    sub = s // rate
    qv = q.reshape(n, sub, rate * N_DIL * ATTN_WIDTH)
    kvv = kv.reshape(n, sub, rate * 2 * ATTN_WIDTH)
    out_sds = jax.ShapeDtypeStruct((n, sub, rate * ATTN_WIDTH), F32)
    o_spec = pl.BlockSpec((1, ATTN_BLOCK, ATTN_WIDTH), lambda b, r, i: (b, i, r))
    o, lse = pl.pallas_call(
        _attn_kernel,
        grid=(n, rate, sub // ATTN_BLOCK),
        in_specs=[
            pl.BlockSpec((1, ATTN_BLOCK, ATTN_WIDTH), lambda b, r, i: (b, i, r * N_DIL + g)),
            pl.BlockSpec((1, ATTN_BLOCK, 2 * ATTN_WIDTH), lambda b, r, i: (b, i, r)),
            pl.BlockSpec((1, ATTN_BLOCK, 2 * ATTN_WIDTH),
                         lambda b, r, i: (b, jnp.maximum(i - 1, 0), r)),
        ],
        out_specs=[o_spec, o_spec],
        out_shape=[out_sds, out_sds],
        compiler_params=_params(("parallel", "parallel", "parallel")),
        name=f"band_attn_g{g}",
    )(qv, kvv, kvv)
    return o.reshape(n * s, ATTN_WIDTH), lse.reshape(n * s, ATTN_WIDTH)


def _merge_groups(os_, lses):
    lmax = jnp.maximum(jnp.maximum(lses[0], lses[1]), lses[2])
    es = [jnp.exp(l - lmax) for l in lses]
    inv = 1.0 / (es[0] + es[1] + es[2])
    return (es[0] * inv) * os_[0] + (es[1] * inv) * os_[1] + (es[2] * inv) * os_[2]


def _conv_tail(y, cb, clg, clb):
    z = _layer_norm(y + cb, clg, clb)
    return z * jax.nn.sigmoid(z)


def _out_proj_ln(x, attn, conv, wo_ref, lng, lnb):
    y = (jnp.dot(attn.astype(BF16), wo_ref[0:ATTN_WIDTH, :], preferred_element_type=F32)
         + jnp.dot(conv.astype(BF16), wo_ref[ATTN_WIDTH:ATTN_WIDTH + CONV_CH, :],
                   preferred_element_type=F32))
    return _layer_norm(DN_ALPHA * x + y, lng, lnb)


def _mix_kernel(x_ref, o0_ref, o1_ref, o2_ref, l0_ref, l1_ref, l2_ref, gc_ref, gp_ref,
                cw_ref, cb_ref, clg_ref, clb_ref, wo_ref, lng_ref, lnb_ref, h_ref, ext_ref):
    tm = x_ref.shape[0]
    first = pl.program_id(1) == 0
    ext_ref[0:CONV_HALO, :] = jnp.where(first, 0.0, gp_ref[...])
    ext_ref[CONV_HALO:CONV_HALO + tm, :] = gc_ref[...]
    base = CONV_HALO - (CONV_WIDTH - 1)
    acc = cw_ref[0:1, :] * ext_ref[base:base + tm, :]
    for k in range(1, CONV_WIDTH):
        acc = acc + cw_ref[k:k + 1, :] * ext_ref[base + k:base + k + tm, :]
    conv = _conv_tail(acc, cb_ref[...], clg_ref[...], clb_ref[...])
    attn = _merge_groups((o0_ref[...], o1_ref[...], o2_ref[...]),
                         (l0_ref[...], l1_ref[...], l2_ref[...]))
    h_ref[...] = _out_proj_ln(x_ref[...], attn, conv, wo_ref, lng_ref[...], lnb_ref[...])


def _prompt_mix(x2d, os_, lses, glu, conv_w, conv_b, clg, clb, wo_bf, lng, lnb, n, s, tm):
    tps = s // tm
    hb = tm // CONV_HALO
    row = lambda b, t: (b * tps + t, 0)
    aw = pl.BlockSpec((tm, ATTN_WIDTH), row)
    return pl.pallas_call(
        _mix_kernel,
        grid=(n, tps),
        in_specs=[
            pl.BlockSpec((tm, D_MODEL), row),
            aw, aw, aw, aw, aw, aw,
            pl.BlockSpec((tm, CONV_CH), row),
            pl.BlockSpec((CONV_HALO, CONV_CH),
                         lambda b, t: (jnp.maximum((b * tps + t) * hb - 1, 0), 0)),
            _const_spec((CONV_WIDTH, CONV_CH)),
            _const_spec((1, CONV_CH)), _const_spec((1, CONV_CH)), _const_spec((1, CONV_CH)),
            _const_spec((ATTN_WIDTH + CONV_CH, D_MODEL)),
            _const_spec((1, D_MODEL)), _const_spec((1, D_MODEL)),
        ],
        out_specs=pl.BlockSpec((tm, D_MODEL), row),
        out_shape=jax.ShapeDtypeStruct((n * s, D_MODEL), F32),
        scratch_shapes=[pltpu.VMEM((CONV_HALO + tm, CONV_CH), F32)],
        compiler_params=_params(("parallel", "parallel")),
        name="mix_out",
    )(x2d, *os_, *lses, glu, glu, conv_w, conv_b, clg, clb, wo_bf, lng, lnb)


def _routing_rows(logit_rows, bias_ref):
    m = logit_rows[0]
    for r in logit_rows[1:]:
        m = jnp.maximum(m, r)
    ex = [jnp.exp(r - m) for r in logit_rows]
    tot = ex[0]
    for e in ex[1:]:
        tot = tot + e
    scores = [e / tot for e in ex]
    sel = [scores[e] + bias_ref[e:e + 1, :] for e in range(N_EXPERTS)]
    grp = []
    for g in range(N_EXPERT_GROUPS):
        v = sel[g * EXPERTS_PER_GROUP:(g + 1) * EXPERTS_PER_GROUP]
        best = v[0] + v[1]
        for i in range(EXPERTS_PER_GROUP):
            for j in range(i + 1, EXPERTS_PER_GROUP):
                if (i, j) != (0, 1):
                    best = jnp.maximum(best, v[i] + v[j])
        grp.append(best)
    gmax = grp[0]
    for v in grp[1:]:
        gmax = jnp.maximum(gmax, v)
    taken = None
    in_group = []
    for g in range(N_EXPERT_GROUPS):
        hit = grp[g] == gmax
        if taken is None:
            in_group.append(hit)
            taken = hit
        else:
            in_group.append(jnp.logical_and(hit, jnp.logical_not(taken)))
            taken = jnp.logical_or(taken, hit)
    gates = []
    for e in range(N_EXPERTS):
        g = e // EXPERTS_PER_GROUP
        rank = jnp.zeros_like(sel[e])
        for o in range(g * EXPERTS_PER_GROUP, (g + 1) * EXPERTS_PER_GROUP):
            if o == e:
                continue
            ahead = sel[o] > sel[e]
            if o < e:
                ahead = jnp.logical_or(ahead, sel[o] == sel[e])
            rank = rank + ahead.astype(F32)
        chosen = jnp.logical_and(in_group[g], rank < float(2))
        gates.append(jnp.where(chosen, scores[e], 0.0))
    den = gates[0]
    for v in gates[1:]:
        den = den + v
    return [v / den for v in gates]


def _moe_kernel(h_ref, rw_ref, rb_ref, w1_ref, w3_ref, w2_ref, lng_ref, lnb_ref, out_ref, ct_ref):
    h = h_ref[...]
    hb = h.astype(BF16)
    logits = jnp.dot(hb, rw_ref[...], preferred_element_type=F32)
    lt = logits.T
    rows = [lt[e:e + 1, :] for e in range(N_EXPERTS)]
    comb_rows = _routing_rows(rows, rb_ref)
    ct_ref[...] = jnp.zeros_like(ct_ref)
    for e in range(N_EXPERTS):
        ct_ref[e:e + 1, :] = comb_rows[e]
    comb = ct_ref[...].T
    acc = jnp.zeros(h.shape, F32)
    for e in range(N_EXPERTS):
        a = jnp.dot(hb, w1_ref[e], preferred_element_type=F32)
        b = jnp.dot(hb, w3_ref[e], preferred_element_type=F32)
        gated = (a * jax.nn.sigmoid(a)) * b * comb[:, e:e + 1]
        acc = acc + jnp.dot(gated.astype(BF16), w2_ref[e], preferred_element_type=F32)
    out_ref[...] = _layer_norm(DN_ALPHA * h + acc, lng_ref[...], lnb_ref[...])


def _moe(h2d, rw_pad, rb, w1, w3, w2, lng, lnb, tm):
    m = h2d.shape[0]
    row = lambda i: (i, 0)
    return pl.pallas_call(
        _moe_kernel,
        grid=(m // tm,),
        in_specs=[
            pl.BlockSpec((tm, D_MODEL), row),
            _const_spec((D_MODEL, LANES)),
            _const_spec((N_EXPERTS, 1)),
            _resident_spec((N_EXPERTS, D_MODEL, D_EXPERT)),
            _resident_spec((N_EXPERTS, D_MODEL, D_EXPERT)),
            _resident_spec((N_EXPERTS, D_EXPERT, D_MODEL)),
            _const_spec((1, D_MODEL)), _const_spec((1, D_MODEL)),
        ],
        out_specs=pl.BlockSpec((tm, D_MODEL), row),
        out_shape=jax.ShapeDtypeStruct((m, D_MODEL), F32),
        scratch_shapes=[pltpu.VMEM((LANES, tm), F32)],
        compiler_params=_params(("parallel",)),
        name="moe_ffn",
    )(h2d, rw_pad, rb, w1, w3, w2, lng, lnb)


def _pool_project(parts, pw_ref, pb_ref, ps_ref):
    cols = []
    for gi in range(len(POOL_WINDOWS)):
        lo = gi * POOL_CH
        y = jnp.dot(parts[gi].astype(BF16), pw_ref[gi], preferred_element_type=F32)
        cols.append((y + pb_ref[gi:gi + 1, :]) * ps_ref[:, lo:lo + POOL_CH])
    return jnp.concatenate(cols, axis=1)


def _pool_kernel(x_ref, xp_ref, pw_ref, pb_ref, ps_ref, lng_ref, lnb_ref, out_ref, ext_ref):
    tm = x_ref.shape[0]
    t = pl.program_id(1)
    ext_ref[0:POOL_HALO, :] = jnp.where(t == 0, 0.0, xp_ref[...])
    ext_ref[POOL_HALO:POOL_HALO + tm, :] = x_ref[...]
    pos = t * tm + lax.broadcasted_iota(jnp.int32, (tm, 1), 0)
    parts = []
    for gi, w in enumerate(POOL_WINDOWS):
        lo = gi * POOL_CH
        cur = x_ref[:, lo:lo + POOL_CH]
        tot = cur
        for j in range(1, w):
            tot = tot + ext_ref[POOL_HALO - j:POOL_HALO - j + tm, lo:lo + POOL_CH]
        cnt = jnp.minimum(w, pos + 1).astype(F32)
        parts.append(tot / cnt - cur)
    y = _pool_project(parts, pw_ref, pb_ref, ps_ref)
    out_ref[...] = _layer_norm(DN_ALPHA * x_ref[...] + y, lng_ref[...], lnb_ref[...])


def _prompt_pool(h2d, pw_bf, pb, ps, lng, lnb, n, s, tm):
    tps = s // tm
    hb = tm // POOL_HALO
    row = lambda b, t: (b * tps + t, 0)
    ng = len(POOL_WINDOWS)
    return pl.pallas_call(
        _pool_kernel,
        grid=(n, tps),
        in_specs=[
            pl.BlockSpec((tm, D_MODEL), row),
            pl.BlockSpec((POOL_HALO, D_MODEL),
                         lambda b, t: (jnp.maximum((b * tps + t) * hb - 1, 0), 0)),
            _const_spec((ng, POOL_CH, POOL_CH)),
            _const_spec((ng, POOL_CH)),
            _const_spec((1, D_MODEL)), _const_spec((1, D_MODEL)), _const_spec((1, D_MODEL)),
        ],
        out_specs=pl.BlockSpec((tm, D_MODEL), row),
        out_shape=jax.ShapeDtypeStruct((n * s, D_MODEL), F32),
        scratch_shapes=[pltpu.VMEM((POOL_HALO + tm, D_MODEL), F32)],
        compiler_params=_params(("parallel", "parallel")),
        name="pool_mix",
    )(h2d, h2d, pw_bf, pb, ps, lng, lnb)


def _dec_attn_kernel(q_ref, n0_ref, n1_ref, n2_ref, c0_ref, c1_ref, c2_ref, out_ref):
    nb = q_ref.shape[0]
    head_of_lane = lax.broadcasted_iota(jnp.int32, (HEADS, ATTN_WIDTH), 1) // HEAD_DIM
    head_of_row = lax.broadcasted_iota(jnp.int32, (HEADS, ATTN_WIDTH), 0)
    own = head_of_lane == head_of_row
    new_refs = (n0_ref, n1_ref, n2_ref)
    cache_refs = (c0_ref, c1_ref, c2_ref)
    nt = (((1,), (1,)), ((), ()))
    for s in range(nb):
        outs, lses = [], []
        for g in range(N_DIL):
            c = g * ATTN_WIDTH
            qrow = q_ref[s:s + 1, c:c + ATTN_WIDTH].astype(F32)
            qf = jnp.where(own, jnp.broadcast_to(qrow, (HEADS, ATTN_WIDTH)), 0.0)
            qblk = qf.astype(BF16)
            kb = cache_refs[g][s, :, 0:ATTN_WIDTH].astype(BF16)
            vb = cache_refs[g][s, :, ATTN_WIDTH:2 * ATTN_WIDTH].astype(BF16)
            knew = _bf16_round(new_refs[g][s:s + 1, 0:ATTN_WIDTH])
            vnew = _bf16_round(new_refs[g][s:s + 1, ATTN_WIDTH:2 * ATTN_WIDTH])
            sc = lax.dot_general(qblk, kb, nt, preferred_element_type=F32)
            sn = jnp.sum(qf * knew, axis=-1, keepdims=True)
            m = jnp.maximum(jnp.max(sc, axis=-1, keepdims=True), sn)
            p = jnp.exp(sc - m)
            pn = jnp.exp(sn - m)
            l = jnp.sum(p, axis=-1, keepdims=True) + pn
            inv = 1.0 / l
            full = (jnp.dot((p * inv).astype(BF16), vb, preferred_element_type=F32)
                    + _bf16_round(pn * inv) * vnew)
            outs.append(jnp.sum(jnp.where(own, full, 0.0), axis=0, keepdims=True))
            lses.append(jnp.sum(jnp.where(own, m + jnp.log(l), 0.0), axis=0, keepdims=True))
        out_ref[s:s + 1, :] = _merge_groups(outs, lses)


def _decode_attention(q, new_kv, caches):
    nseq = q.shape[0]
    nb = DEC_SEQ_BLOCK
    views = []
    for g in range(N_DIL):
        w, rate = DIL_WINDOWS[g], DIL_RATES[g]
        views.append(caches[g].reshape(nseq, w // rate, rate * 2 * ATTN_WIDTH))
    row = lambda i: (i, 0)
    cache_spec = pl.BlockSpec((nb, ATTN_BLOCK, 2 * ATTN_WIDTH), lambda i: (i, 0, 0))
    new_spec = pl.BlockSpec((nb, 2 * ATTN_WIDTH), row)
    return pl.pallas_call(
        _dec_attn_kernel,
        grid=(nseq // nb,),
        in_specs=[pl.BlockSpec((nb, N_DIL * ATTN_WIDTH), row),
                  new_spec, new_spec, new_spec, cache_spec, cache_spec, cache_spec],
        out_specs=pl.BlockSpec((nb, ATTN_WIDTH), row),
        out_shape=jax.ShapeDtypeStruct((nseq, ATTN_WIDTH), F32),
        compiler_params=_params(("parallel",)),
        name="decode_attn",
    )(q, *new_kv, *views)


def _dec_mix_kernel(x_ref, attn_ref, st_ref, glu_ref, cw_ref, cb_ref, clg_ref, clb_ref,
                    wo_ref, lng_ref, lnb_ref, h_ref, nst_ref):
    npre = CONV_WIDTH - 1
    glu = glu_ref[...]
    acc = cw_ref[npre:npre + 1, :] * glu
    for k in range(npre):
        acc = acc + cw_ref[k:k + 1, :] * st_ref[:, k * CONV_CH:(k + 1) * CONV_CH]
    conv = _conv_tail(acc, cb_ref[...], clg_ref[...], clb_ref[...])
    h_ref[...] = _out_proj_ln(x_ref[...], attn_ref[...], conv, wo_ref, lng_ref[...], lnb_ref[...])
    nst_ref[:, 0:(npre - 1) * CONV_CH] = st_ref[:, CONV_CH:npre * CONV_CH]
    nst_ref[:, (npre - 1) * CONV_CH:npre * CONV_CH] = glu


def _decode_mix(x2d, attn, state2d, glu, conv_w, conv_b, clg, clb, wo_bf, lng, lnb):
    nseq = x2d.shape[0]
    args = (x2d, attn, state2d, glu, conv_w, conv_b, clg, clb, wo_bf, lng, lnb)
    return pl.pallas_call(
        _dec_mix_kernel,
        grid=(1,),
        in_specs=[_const_spec(a.shape) for a in args],
        out_specs=[_const_spec((nseq, D_MODEL)), _const_spec(state2d.shape)],
        out_shape=[jax.ShapeDtypeStruct((nseq, D_MODEL), F32),
                   jax.ShapeDtypeStruct(state2d.shape, F32)],
        compiler_params=_params(("arbitrary",)),
        name="decode_mix",
    )(*args)


def _dec_pool_kernel(x_ref, st_ref, pw_ref, pb_ref, ps_ref, lng_ref, lnb_ref, out_ref, nst_ref):
    x = x_ref[...]
    parts = []
    for gi, w in enumerate(POOL_WINDOWS):
        lo = gi * POOL_CH
        cur = x[:, lo:lo + POOL_CH]
        tot = cur
        for j in range(1, w):
            c0 = (POOL_PREFIX - j) * D_MODEL + lo
            tot = tot + st_ref[:, c0:c0 + POOL_CH]
        parts.append(tot / float(w) - cur)
    y = _pool_project(parts, pw_ref, pb_ref, ps_ref)
    out_ref[...] = _layer_norm(DN_ALPHA * x + y, lng_ref[...], lnb_ref[...])
    nst_ref[:, 0:(POOL_PREFIX - 1) * D_MODEL] = st_ref[:, D_MODEL:POOL_PREFIX * D_MODEL]
    nst_ref[:, (POOL_PREFIX - 1) * D_MODEL:POOL_PREFIX * D_MODEL] = x


def _decode_pool(h2d, state2d, pw_bf, pb, ps, lng, lnb):
    nseq = h2d.shape[0]
    args = (h2d, state2d, pw_bf, pb, ps, lng, lnb)
    return pl.pallas_call(
        _dec_pool_kernel,
        grid=(1,),
        in_specs=[_const_spec(a.shape) for a in args],
        out_specs=[_const_spec((nseq, D_MODEL)), _const_spec(state2d.shape)],
        out_shape=[jax.ShapeDtypeStruct((nseq, D_MODEL), F32),
                   jax.ShapeDtypeStruct(state2d.shape, F32)],
        compiler_params=_params(("arbitrary",)),
        name="decode_pool",
    )(*args)


def kernel(x_prompt, x_sample, cache_attn_w128, cache_attn_w512, cache_attn_w2048, state_conv, state_pool,
           w_in, b_in, conv_w, conv_b, conv_ln_g, conv_ln_b, w_out, pool_w, pool_b, pool_scale,
           ln_mix_g, ln_mix_b, ln_ffn_g, ln_ffn_b, router_w, router_bias, moe_w1, moe_w3, moe_w2):
    n, s, d = x_prompt.shape
    nseq = x_sample.shape[0]
    past = cache_attn_w2048.shape[2]
    assert d == D_MODEL and x_sample.shape[1] == 1 and s % ROW_TILE == 0
    assert cache_attn_w128.shape[0] == 1 and past == DIL_WINDOWS[2]
    caches = (cache_attn_w128[0], cache_attn_w512[0], cache_attn_w2048[0])

    w_in_bf = w_in[0].astype(BF16)
    w_out_bf = w_out[0].astype(BF16)
    pool_w_bf = pool_w[0].astype(BF16)
    w1_bf, w3_bf, w2_bf = moe_w1.astype(BF16), moe_w3.astype(BF16), moe_w2.astype(BF16)
    rw_pad = jnp.pad(router_w, ((0, 0), (0, LANES - N_EXPERTS))).astype(BF16)
    rb = router_bias.astype(F32).reshape(N_EXPERTS, 1)
    r2 = lambda v: v.reshape(1, -1)
    b_in2 = r2(b_in[0])
    cb, clg, clb = r2(conv_b[0]), r2(conv_ln_g[0]), r2(conv_ln_b[0])
    ps = r2(pool_scale[0])

    def moe(h2d, layer, tm):
        return _moe(h2d, rw_pad, rb, w1_bf[layer], w3_bf[layer], w2_bf[layer],
                    r2(ln_ffn_g[layer]), r2(ln_ffn_b[layer]), tm)

    xp = x_prompt.reshape(n * s, d)
    tabs_p = _rope_tables(jnp.arange(s, dtype=jnp.int32))
    q, kv0, kv1, kv2, glu = _project(xp, w_in_bf, b_in2, tabs_p, ROW_TILE, s // ROW_TILE)
    kvs = (kv0, kv1, kv2)
    os_, lses = [], []
    for g in range(N_DIL):
        o, l = _prompt_attention(q, kvs[g], g, n, s)
        os_.append(o)
        lses.append(l)
    h = _prompt_mix(xp, os_, lses, glu, conv_w[0], cb, clg, clb, w_out_bf,
                    r2(ln_mix_g[0]), r2(ln_mix_b[0]), n, s, ROW_TILE)
    h = moe(h, 0, ROW_TILE)
    p_pool = h.reshape(n, s, d)[:, s - POOL_PREFIX:][None]
    h = _prompt_pool(h, pool_w_bf, pool_b[0], ps, r2(ln_mix_g[1]), r2(ln_mix_b[1]), n, s, ROW_TILE)
    y_prompt = moe(h, 1, ROW_TILE).reshape(n, s, d)
    p_attn = []
    for g in range(N_DIL):
        keep = min(DIL_WINDOWS[g], s)
        p_attn.append(kvs[g].reshape(n, s, 2, HEADS, HEAD_DIM)[:, s - keep:][None])
    p_conv = glu.reshape(n, s, CONV_CH)[:, s - (CONV_WIDTH - 1):][None]

    xs = x_sample.reshape(nseq, d)
    tabs_s = jnp.broadcast_to(_rope_tables(past + jnp.arange(1, dtype=jnp.int32)), (3, nseq, LANES))
    qs, n0, n1, n2, glus = _project(xs, w_in_bf, b_in2, tabs_s, nseq, 1)
    new_kv = (n0, n1, n2)
    attn_s = _decode_attention(qs, new_kv, caches)
    conv_state = state_conv[0].reshape(nseq, (CONV_WIDTH - 1) * CONV_CH)
    hs, new_conv = _decode_mix(xs, attn_s, conv_state, glus, conv_w[0], cb, clg, clb, w_out_bf,
                               r2(ln_mix_g[0]), r2(ln_mix_b[0]))
    hs = moe(hs, 0, nseq)
    pool_state = state_pool[0].reshape(nseq, POOL_PREFIX * D_MODEL)
    hs, new_pool = _decode_pool(hs, pool_state, pool_w_bf, pool_b[0], ps,
                                r2(ln_mix_g[1]), r2(ln_mix_b[1]))
    y_sample = moe(hs, 1, nseq).reshape(nseq, 1, d)
    s_attn = []
    for g in range(N_DIL):
        newrow = new_kv[g].reshape(nseq, 1, 2, HEADS, HEAD_DIM)
        s_attn.append(jnp.concatenate([caches[g][:, 1:], newrow], axis=1)[None])
    s_conv = new_conv.reshape(1, nseq, CONV_WIDTH - 1, CONV_CH)
    s_pool = new_pool.reshape(1, nseq, POOL_PREFIX, D_MODEL)

    return (y_prompt, y_sample, p_attn[0], p_attn[1], p_attn[2], p_conv, p_pool,
            s_attn[0], s_attn[1], s_attn[2], s_conv, s_pool)
```

```python
import functools

import jax
import jax.numpy as jnp
from jax import lax
from jax.experimental import pallas as pl
from jax.experimental.pallas import tpu as pltpu

F32 = jnp.float32
BF16 = jnp.bfloat16

D_MODEL = 1024
HEAD_DIM = 64
HEADS = 8
ATTN_WIDTH = HEADS * HEAD_DIM
N_DIL = 3
DIL_WINDOWS = (128, 512, 2048)
DIL_RATES = (1, 4, 16)
ATTN_BLOCK = 128
ATTN_SCALE = HEAD_DIM ** -0.5
ROT_DIM = HEAD_DIM // 4
ROT_HALF = ROT_DIM // 2
ROPE_THETA = 500000.0
QKV_COLS = 3 * N_DIL * ATTN_WIDTH
CONV_CH = D_MODEL // 4
CONV_WIDTH = 31
IN_COLS = QKV_COLS + 2 * CONV_CH
POOL_WINDOWS = (2, 4, 8, 16)
POOL_CH = D_MODEL // len(POOL_WINDOWS)
POOL_PREFIX = max(POOL_WINDOWS) - 1
N_EXPERTS = 16
N_EXPERT_GROUPS = 4
EXPERTS_PER_GROUP = 4
D_EXPERT = 256
DEPTH = 2
DN_ALPHA = (2.0 * DEPTH) ** 0.25
LN_EPS = 1e-5

LANES = 128
SUBLANES = 8
VMEM_LIMIT_BYTES = 56 * 1024 * 1024
ROW_TILE = 512
CONV_HALO = 32
POOL_HALO = 16
ATTN_RES_BLOCK = 4
CACHE_SEQ_BLOCK = (16, 4, 1)
HEAD_PAIRS = ATTN_WIDTH // LANES

NT_DIMS = (((1,), (1,)), ((), ()))


def _params(sem):
    return pltpu.CompilerParams(dimension_semantics=sem, vmem_limit_bytes=VMEM_LIMIT_BYTES)


def _const_spec(shape):
    nd = len(shape)
    return pl.BlockSpec(shape, lambda *_: (0,) * nd)


def _resident_spec(shape):
    nd = len(shape)
    return pl.BlockSpec(shape, lambda *_: (0,) * nd, pipeline_mode=pl.Buffered(1))


def _layer_norm(x, g, b):
    mu = jnp.mean(x, axis=-1, keepdims=True)
    xc = x - mu
    var = jnp.mean(xc * xc, axis=-1, keepdims=True)
    return xc * lax.rsqrt(var + LN_EPS) * g + b


def _proj_kernel(x_ref, w_ref, b_ref, tab_ref, q0_ref, q1_ref, q2_ref, kb0_ref, kb1_ref, kb2_ref,
                 glu_ref, kt0_ref, kt1_ref, kt2_ref):
    tm = x_ref.shape[0]
    last = pl.program_id(1) == pl.num_programs(1) - 1
    xb = x_ref[...].astype(BF16)
    cosm, sin_lo, sin_hi = tab_ref[0], tab_ref[1], tab_ref[2]

    def proj(c0, width):
        return (jnp.dot(xb, w_ref[:, c0:c0 + width], preferred_element_type=F32)
                + b_ref[:, c0:c0 + width])

    def rope(t):
        parts = []
        for j in range(t.shape[1] // LANES):
            v = t[:, j * LANES:(j + 1) * LANES]
            parts.append(v * cosm
                         + pltpu.roll(v, LANES - ROT_HALF, 1) * sin_lo
                         + pltpu.roll(v, ROT_HALF, 1) * sin_hi)
        return jnp.concatenate(parts, axis=1)

    q_refs = (q0_ref, q1_ref, q2_ref)
    kb_refs = (kb0_ref, kb1_ref, kb2_ref)
    for g in range(N_DIL):
        c = g * ATTN_WIDTH
        q_refs[g][...] = (rope(proj(c, ATTN_WIDTH)) * ATTN_SCALE).astype(BF16)
        k = rope(proj(N_DIL * ATTN_WIDTH + c, ATTN_WIDTH))
        v = proj(2 * N_DIL * ATTN_WIDTH + c, ATTN_WIDTH)
        kb_refs[g][:, 0:ATTN_WIDTH] = k.astype(BF16)
        kb_refs[g][:, ATTN_WIDTH:2 * ATTN_WIDTH] = v.astype(BF16)
        if g == 2:
            kt2_ref[0, 0:ATTN_WIDTH, :] = k.T
            kt2_ref[0, ATTN_WIDTH:2 * ATTN_WIDTH, :] = v.T
        elif g == 1:
            @pl.when(last)
            def _():
                kt1_ref[0, 0:ATTN_WIDTH, :] = k.T
                kt1_ref[0, ATTN_WIDTH:2 * ATTN_WIDTH, :] = v.T
        else:
            @pl.when(last)
            def _():
                keep = DIL_WINDOWS[0]
                kt0_ref[0, 0:ATTN_WIDTH, :] = k[tm - keep:, :].T
                kt0_ref[0, ATTN_WIDTH:2 * ATTN_WIDTH, :] = v[tm - keep:, :].T
    ga = proj(QKV_COLS, CONV_CH)
    gb = proj(QKV_COLS + CONV_CH, CONV_CH)
    glu_ref[...] = ga * jax.nn.sigmoid(gb)


def _project(x2d, w_bf, b2d, tables, n, s, tm):
    assert tm == DIL_WINDOWS[1] and tm >= DIL_WINDOWS[0]
    tps = s // tm
    m = n * s
    row = lambda b, t: (b * tps + t, 0)
    seq = lambda b, t: (b, 0, 0)
    kv_rows = 2 * ATTN_WIDTH
    out_specs = ([pl.BlockSpec((tm, ATTN_WIDTH), row)] * N_DIL
                 + [pl.BlockSpec((tm, kv_rows), row)] * N_DIL
                 + [pl.BlockSpec((tm, CONV_CH), row),
                    pl.BlockSpec((1, kv_rows, DIL_WINDOWS[0]), seq),
                    pl.BlockSpec((1, kv_rows, DIL_WINDOWS[1]), seq),
                    pl.BlockSpec((1, kv_rows, tm), lambda b, t: (b, 0, t))])
    out_shape = ([jax.ShapeDtypeStruct((m, ATTN_WIDTH), BF16)] * N_DIL
                 + [jax.ShapeDtypeStruct((m, kv_rows), BF16)] * N_DIL
                 + [jax.ShapeDtypeStruct((m, CONV_CH), F32),
                    jax.ShapeDtypeStruct((n, kv_rows, DIL_WINDOWS[0]), F32),
                    jax.ShapeDtypeStruct((n, kv_rows, DIL_WINDOWS[1]), F32),
                    jax.ShapeDtypeStruct((n, kv_rows, s), F32)])
    return pl.pallas_call(
        _proj_kernel,
        grid=(n, tps),
        in_specs=[
            pl.BlockSpec((tm, D_MODEL), row),
            _resident_spec((D_MODEL, IN_COLS)),
            _const_spec((1, IN_COLS)),
            pl.BlockSpec((3, tm, LANES), lambda b, t: (0, t, 0)),
        ],
        out_specs=out_specs,
        out_shape=out_shape,
        compiler_params=_params(("parallel", "arbitrary")),
        name="in_proj",
    )(x2d, w_bf, b2d, tables)


def _rope_angles(pos):
    inv_freq = ROPE_THETA ** (-jnp.arange(ROT_HALF, dtype=F32) * 2.0 / ROT_DIM)
    return pos.astype(F32)[:, None] * inv_freq[None, :]


def _rope_tables(pos):
    t = pos.shape[0]
    ang = _rope_angles(pos)
    cos, sin = jnp.cos(ang), jnp.sin(ang)
    rest = HEAD_DIM - ROT_DIM
    c64 = jnp.concatenate([cos, cos, jnp.ones((t, rest), F32)], axis=1)
    lo64 = jnp.concatenate([-sin, jnp.zeros((t, HEAD_DIM - ROT_HALF), F32)], axis=1)
    hi64 = jnp.concatenate([jnp.zeros((t, ROT_HALF), F32), sin, jnp.zeros((t, rest), F32)], axis=1)
    rep = LANES // HEAD_DIM
    return jnp.stack([jnp.tile(c64, (1, rep)), jnp.tile(lo64, (1, rep)), jnp.tile(hi64, (1, rep))])


def _attn_kernel(q_ref, kv_ref, bias_ref, o_ref, lse_ref, s_ref, p_ref, *, rb, nblk):
    nk = s_ref.shape[2]
    lane = lax.broadcasted_iota(jnp.int32, (ATTN_BLOCK, LANES), 1)
    low_half = lane < HEAD_DIM
    keep_lo = low_half.astype(BF16)
    keep_hi = 1 - keep_lo
    ones_rhs = jnp.ones((nk, LANES), BF16)

    def unit(j, b):
        if nblk > 1:
            q0 = pl.multiple_of(b * ATTN_BLOCK, ATTN_BLOCK)
            k0 = pl.multiple_of(jnp.maximum(b - 1, 0) * ATTN_BLOCK, ATTN_BLOCK)
            bias = bias_ref[jnp.minimum(b, 1)]
        else:
            q0 = k0 = 0
            bias = bias_ref[0]
        rows_q = pl.ds(q0, ATTN_BLOCK)
        rows_k = pl.ds(k0, nk)
        for hp in range(HEAD_PAIRS):
            qp = q_ref[0, rows_q, j * ATTN_WIDTH + hp * LANES:j * ATTN_WIDTH + (hp + 1) * LANES]
            kp = kv_ref[0, rows_k, j * 2 * ATTN_WIDTH + hp * LANES:j * 2 * ATTN_WIDTH + (hp + 1) * LANES]
            for half, keep in enumerate((keep_lo, keep_hi)):
                s = lax.dot_general(qp * keep, kp, NT_DIMS, preferred_element_type=F32)
                s_ref[2 * hp + half] = s + bias
        sc = s_ref[...]
        m = jnp.max(sc, axis=-1, keepdims=True)
        p_ref[...] = jnp.exp(sc - m).astype(BF16)
        for hp in range(HEAD_PAIRS):
            c0 = j * 2 * ATTN_WIDTH + ATTN_WIDTH + hp * LANES
            rhs = jnp.concatenate([kv_ref[0, rows_k, c0:c0 + LANES], ones_rhs], axis=1)
            ol_lo = jnp.dot(p_ref[2 * hp], rhs, preferred_element_type=F32)
            ol_hi = jnp.dot(p_ref[2 * hp + 1], rhs, preferred_element_type=F32)
            l_lo, l_hi = ol_lo[:, LANES:], ol_hi[:, LANES:]
            o_pair = jnp.where(low_half, ol_lo[:, :LANES] * (1.0 / l_lo), ol_hi[:, :LANES] * (1.0 / l_hi))
            lse_pair = jnp.where(low_half, m[2 * hp] + jnp.log(l_lo), m[2 * hp + 1] + jnp.log(l_hi))
            cols = slice(j * ATTN_WIDTH + hp * LANES, j * ATTN_WIDTH + (hp + 1) * LANES)
            o_ref[0, rows_q, cols] = o_pair
            lse_ref[0, rows_q, cols] = lse_pair

    def block(b, carry):
        for j in range(rb):
            unit(j, b)
        return carry

    if nblk > 1:
        lax.fori_loop(0, nblk, block, 0)
    else:
        block(0, 0)


def _band_bias(nblk):
    qi = jnp.arange(ATTN_BLOCK)[:, None]
    ci = jnp.arange(ATTN_BLOCK)[None, :]
    causal = jnp.where(ci <= qi, 0.0, -jnp.inf).astype(F32)
    if nblk == 1:
        return jnp.stack([causal, causal])
    band = jnp.where(ci >= qi, 0.0, -jnp.inf).astype(F32)
    closed = jnp.full((ATTN_BLOCK, ATTN_BLOCK), -jnp.inf, F32)
    return jnp.stack([jnp.concatenate([causal, closed], axis=1), jnp.concatenate([band, causal], axis=1)])


def _prompt_attention(q, kvb, g, n, s):
    rate = DIL_RATES[g]
    sub = s // rate
    nblk = sub // ATTN_BLOCK
    rb = min(rate, ATTN_RES_BLOCK)
    nk = 2 * ATTN_BLOCK if nblk > 1 else ATTN_BLOCK
    qv = q.reshape(n, sub, rate * ATTN_WIDTH)
    kvv = kvb.reshape(n, sub, rate * 2 * ATTN_WIDTH)
    out_sds = jax.ShapeDtypeStruct((n, sub, rate * ATTN_WIDTH), F32)
    blk = lambda b, r: (b, 0, r)
    o_spec = pl.BlockSpec((1, sub, rb * ATTN_WIDTH), blk)
    o, lse = pl.pallas_call(
        functools.partial(_attn_kernel, rb=rb, nblk=nblk),
        grid=(n, rate // rb),
        in_specs=[
            pl.BlockSpec((1, sub, rb * ATTN_WIDTH), blk),
            pl.BlockSpec((1, sub, rb * 2 * ATTN_WIDTH), blk),
            _const_spec((2, ATTN_BLOCK, nk)),
        ],
        out_specs=[o_spec, o_spec],
        out_shape=[out_sds, out_sds],
        scratch_shapes=[pltpu.VMEM((HEADS, ATTN_BLOCK, nk), F32),
                        pltpu.VMEM((HEADS, ATTN_BLOCK, nk), BF16)],
        compiler_params=_params(("parallel", "parallel")),
        name=f"band_attn_g{g}",
    )(qv, kvv, _band_bias(nblk))
    return o.reshape(n * s, ATTN_WIDTH), lse.reshape(n * s, ATTN_WIDTH)


def _group_weights(lses):
    lmax = jnp.maximum(jnp.maximum(lses[0], lses[1]), lses[2])
    es = [jnp.exp(l - lmax) for l in lses]
    inv = 1.0 / (es[0] + es[1] + es[2])
    return [e * inv for e in es]


def _conv_tail(y, cb, clg, clb):
    z = _layer_norm(y + cb, clg, clb)
    return z * jax.nn.sigmoid(z)


def _out_proj_ln(x, attn, conv, wo_ref, lng, lnb):
    y = (jnp.dot(attn.astype(BF16), wo_ref[0:ATTN_WIDTH, :], preferred_element_type=F32)
         + jnp.dot(conv.astype(BF16), wo_ref[ATTN_WIDTH:ATTN_WIDTH + CONV_CH, :],
                   preferred_element_type=F32))
    return _layer_norm(DN_ALPHA * x + y, lng, lnb)


def _mix_kernel(x_ref, o0_ref, o1_ref, o2_ref, l0_ref, l1_ref, l2_ref, gc_ref, gp_ref,
                cw_ref, cb_ref, clg_ref, clb_ref, wo_ref, lng_ref, lnb_ref, h_ref, ext_ref):
    tm = x_ref.shape[0]
    first = pl.program_id(1) == 0
    ext_ref[0:CONV_HALO, :] = jnp.where(first, 0.0, gp_ref[...])
    ext_ref[CONV_HALO:CONV_HALO + tm, :] = gc_ref[...]
    base = CONV_HALO - (CONV_WIDTH - 1)
    acc = cw_ref[0:1, :] * ext_ref[base:base + tm, :]
    for k in range(1, CONV_WIDTH):
        acc = acc + cw_ref[k:k + 1, :] * ext_ref[base + k:base + k + tm, :]
    conv = _conv_tail(acc, cb_ref[...], clg_ref[...], clb_ref[...])
    w = _group_weights((l0_ref[...], l1_ref[...], l2_ref[...]))
    attn = w[0] * o0_ref[...] + w[1] * o1_ref[...] + w[2] * o2_ref[...]
    h_ref[...] = _out_proj_ln(x_ref[...], attn, conv, wo_ref, lng_ref[...], lnb_ref[...])


def _prompt_mix(x2d, os_, lses, glu, conv_w, conv_b, clg, clb, wo_bf, lng, lnb, n, s, tm):
    tps = s // tm
    hb = tm // CONV_HALO
    row = lambda b, t: (b * tps + t, 0)
    aw = pl.BlockSpec((tm, ATTN_WIDTH), row)
    return pl.pallas_call(
        _mix_kernel,
        grid=(n, tps),
        in_specs=[
            pl.BlockSpec((tm, D_MODEL), row),
            aw, aw, aw, aw, aw, aw,
            pl.BlockSpec((tm, CONV_CH), row),
            pl.BlockSpec((CONV_HALO, CONV_CH),
                         lambda b, t: (jnp.maximum((b * tps + t) * hb - 1, 0), 0)),
            _const_spec((CONV_WIDTH, CONV_CH)),
            _const_spec((1, CONV_CH)), _const_spec((1, CONV_CH)), _const_spec((1, CONV_CH)),
            _const_spec((ATTN_WIDTH + CONV_CH, D_MODEL)),
            _const_spec((1, D_MODEL)), _const_spec((1, D_MODEL)),
        ],
        out_specs=pl.BlockSpec((tm, D_MODEL), row),
        out_shape=jax.ShapeDtypeStruct((n * s, D_MODEL), F32),
        scratch_shapes=[pltpu.VMEM((CONV_HALO + tm, CONV_CH), F32)],
        compiler_params=_params(("parallel", "parallel")),
        name="mix_out",
    )(x2d, *os_, *lses, glu, glu, conv_w, conv_b, clg, clb, wo_bf, lng, lnb)


def _routing_rows(logit_rows, bias_ref):
    m = logit_rows[0]
    for r in logit_rows[1:]:
        m = jnp.maximum(m, r)
    ex = [jnp.exp(r - m) for r in logit_rows]
    tot = ex[0]
    for e in ex[1:]:
        tot = tot + e
    scores = [e / tot for e in ex]
    sel = [scores[e] + bias_ref[e:e + 1, :] for e in range(N_EXPERTS)]
    grp = []
    for g in range(N_EXPERT_GROUPS):
        v = sel[g * EXPERTS_PER_GROUP:(g + 1) * EXPERTS_PER_GROUP]
        best = v[0] + v[1]
        for i in range(EXPERTS_PER_GROUP):
            for j in range(i + 1, EXPERTS_PER_GROUP):
                if (i, j) != (0, 1):
                    best = jnp.maximum(best, v[i] + v[j])
        grp.append(best)
    gmax = grp[0]
    for v in grp[1:]:
        gmax = jnp.maximum(gmax, v)
    taken = None
    in_group = []
    for g in range(N_EXPERT_GROUPS):
        hit = grp[g] == gmax
        if taken is None:
            in_group.append(hit)
            taken = hit
        else:
            in_group.append(jnp.logical_and(hit, jnp.logical_not(taken)))
            taken = jnp.logical_or(taken, hit)
    gates = []
    for e in range(N_EXPERTS):
        g = e // EXPERTS_PER_GROUP
        rank = jnp.zeros_like(sel[e])
        for o in range(g * EXPERTS_PER_GROUP, (g + 1) * EXPERTS_PER_GROUP):
            if o == e:
                continue
            ahead = sel[o] > sel[e]
            if o < e:
                ahead = jnp.logical_or(ahead, sel[o] == sel[e])
            rank = rank + ahead.astype(F32)
        chosen = jnp.logical_and(in_group[g], rank < float(2))
        gates.append(jnp.where(chosen, scores[e], 0.0))
    den = gates[0]
    for v in gates[1:]:
        den = den + v
    return [v / den for v in gates]


def _moe_kernel(h_ref, rw_ref, rb_ref, w1_ref, w3_ref, w2_ref, lng_ref, lnb_ref, out_ref, ct_ref):
    h = h_ref[...]
    hb = h.astype(BF16)
    logits = jnp.dot(hb, rw_ref[...], preferred_element_type=F32)
    lt = logits.T
    rows = [lt[e:e + 1, :] for e in range(N_EXPERTS)]
    comb_rows = _routing_rows(rows, rb_ref)
    ct_ref[...] = jnp.zeros_like(ct_ref)
    for e in range(N_EXPERTS):
        ct_ref[e:e + 1, :] = comb_rows[e]
    comb = ct_ref[...].T
    acc = jnp.zeros(h.shape, F32)
    for e in range(N_EXPERTS):
        a = jnp.dot(hb, w1_ref[e], preferred_element_type=F32)
        b = jnp.dot(hb, w3_ref[e], preferred_element_type=F32)
        gated = (a * jax.nn.sigmoid(a)) * b * comb[:, e:e + 1]
        acc = acc + jnp.dot(gated.astype(BF16), w2_ref[e], preferred_element_type=F32)
    out_ref[...] = _layer_norm(DN_ALPHA * h + acc, lng_ref[...], lnb_ref[...])


def _moe(h2d, rw_pad, rb, w1, w3, w2, lng, lnb, tm):
    m = h2d.shape[0]
    row = lambda i: (i, 0)
    return pl.pallas_call(
        _moe_kernel,
        grid=(m // tm,),
        in_specs=[
            pl.BlockSpec((tm, D_MODEL), row),
            _const_spec((D_MODEL, LANES)),
            _const_spec((N_EXPERTS, 1)),
            _resident_spec((N_EXPERTS, D_MODEL, D_EXPERT)),
            _resident_spec((N_EXPERTS, D_MODEL, D_EXPERT)),
            _resident_spec((N_EXPERTS, D_EXPERT, D_MODEL)),
            _const_spec((1, D_MODEL)), _const_spec((1, D_MODEL)),
        ],
        out_specs=pl.BlockSpec((tm, D_MODEL), row),
        out_shape=jax.ShapeDtypeStruct((m, D_MODEL), F32),
        scratch_shapes=[pltpu.VMEM((LANES, tm), F32)],
        compiler_params=_params(("parallel",)),
        name="moe_ffn",
    )(h2d, rw_pad, rb, w1, w3, w2, lng, lnb)


def _pool_project(parts, pw_ref, pb_ref, ps_ref):
    cols = []
    for gi in range(len(POOL_WINDOWS)):
        lo = gi * POOL_CH
        y = jnp.dot(parts[gi].astype(BF16), pw_ref[gi], preferred_element_type=F32)
        cols.append((y + pb_ref[gi:gi + 1, :]) * ps_ref[:, lo:lo + POOL_CH])
    return jnp.concatenate(cols, axis=1)


def _pool_kernel(x_ref, xp_ref, pw_ref, pb_ref, ps_ref, lng_ref, lnb_ref, out_ref, ext_ref):
    tm = x_ref.shape[0]
    t = pl.program_id(1)
    ext_ref[0:POOL_HALO, :] = jnp.where(t == 0, 0.0, xp_ref[...])
    ext_ref[POOL_HALO:POOL_HALO + tm, :] = x_ref[...]
    pos = t * tm + lax.broadcasted_iota(jnp.int32, (tm, 1), 0)
    parts = []
    for gi, w in enumerate(POOL_WINDOWS):
        lo = gi * POOL_CH
        cur = x_ref[:, lo:lo + POOL_CH]
        tot = cur
        for j in range(1, w):
            tot = tot + ext_ref[POOL_HALO - j:POOL_HALO - j + tm, lo:lo + POOL_CH]
        cnt = jnp.minimum(w, pos + 1).astype(F32)
        parts.append(tot / cnt - cur)
    y = _pool_project(parts, pw_ref, pb_ref, ps_ref)
    out_ref[...] = _layer_norm(DN_ALPHA * x_ref[...] + y, lng_ref[...], lnb_ref[...])


def _prompt_pool(h2d, pw_bf, pb, ps, lng, lnb, n, s, tm):
    tps = s // tm
    hb = tm // POOL_HALO
    row = lambda b, t: (b * tps + t, 0)
    ng = len(POOL_WINDOWS)
    return pl.pallas_call(
        _pool_kernel,
        grid=(n, tps),
        in_specs=[
            pl.BlockSpec((tm, D_MODEL), row),
            pl.BlockSpec((POOL_HALO, D_MODEL),
                         lambda b, t: (jnp.maximum((b * tps + t) * hb - 1, 0), 0)),
            _const_spec((ng, POOL_CH, POOL_CH)),
            _const_spec((ng, POOL_CH)),
            _const_spec((1, D_MODEL)), _const_spec((1, D_MODEL)), _const_spec((1, D_MODEL)),
        ],
        out_specs=pl.BlockSpec((tm, D_MODEL), row),
        out_shape=jax.ShapeDtypeStruct((n * s, D_MODEL), F32),
        scratch_shapes=[pltpu.VMEM((POOL_HALO + tm, D_MODEL), F32)],
        compiler_params=_params(("parallel", "parallel")),
        name="pool_mix",
    )(h2d, h2d, pw_bf, pb, ps, lng, lnb)


def _dec_proj_kernel(x_ref, w_ref, b_ref, cs_ref, qkvt_ref, glu_ref):
    xb = x_ref[...].astype(BF16)
    cos, sin = cs_ref[0], cs_ref[1]
    chunk = ATTN_WIDTH

    def proj(c0, width):
        return (jnp.dot(xb, w_ref[:, c0:c0 + width], preferred_element_type=F32)
                + b_ref[:, c0:c0 + width])

    for ci in range(QKV_COLS // chunk):
        pt = proj(ci * chunk, chunk).T
        if ci < 2 * N_DIL:
            pieces = []
            for hh in range(HEADS):
                base = hh * HEAD_DIM
                x1 = pt[base:base + ROT_HALF, :]
                x2 = pt[base + ROT_HALF:base + ROT_DIM, :]
                pieces += [x1 * cos - x2 * sin, x2 * cos + x1 * sin, pt[base + ROT_DIM:base + HEAD_DIM, :]]
            pt = jnp.concatenate(pieces, axis=0)
            if ci < N_DIL:
                pt = pt * ATTN_SCALE
        qkvt_ref[ci * chunk:(ci + 1) * chunk, :] = pt
    ga = proj(QKV_COLS, CONV_CH)
    gb = proj(QKV_COLS + CONV_CH, CONV_CH)
    glu_ref[...] = ga * jax.nn.sigmoid(gb)


def _decode_project(x2d, w_bf, b2d, cs):
    nseq = x2d.shape[0]
    args = (x2d, w_bf, b2d, cs)
    return pl.pallas_call(
        _dec_proj_kernel,
        grid=(1,),
        in_specs=[_const_spec(a.shape) for a in args],
        out_specs=[_const_spec((QKV_COLS, nseq)), _const_spec((nseq, CONV_CH))],
        out_shape=[jax.ShapeDtypeStruct((QKV_COLS, nseq), F32),
                   jax.ShapeDtypeStruct((nseq, CONV_CH), F32)],
        compiler_params=_params(("arbitrary",)),
        name="decode_proj",
    )(*args)


def _cache_kernel(c_ref, qkvt_ref, nc_ref, ot_ref, lt_ref, *, g, rate):
    nb, w = c_ref.shape[0], c_ref.shape[4]
    nseq = qkvt_ref.shape[1]
    u = nb * HEADS
    step = pl.program_id(0)

    @pl.when(step == 0)
    def _():
        ot_ref[...] = jnp.zeros_like(ot_ref)
        lt_ref[...] = jnp.zeros_like(lt_ref)

    seq_lane = lax.broadcasted_iota(jnp.int32, (ATTN_WIDTH, nseq), 1)

    def columns(row0):
        x = qkvt_ref[row0:row0 + ATTN_WIDTH, :]
        cols = [jnp.sum(jnp.where(seq_lane == step * nb + j, x, 0.0), axis=1, keepdims=True)
                for j in range(nb)]
        return jnp.concatenate(cols, axis=0).reshape(u, HEAD_DIM, 1)

    qc = columns(g * ATTN_WIDTH)
    kc = columns((N_DIL + g) * ATTN_WIDTH)
    vc = columns((2 * N_DIL + g) * ATTN_WIDTH)
    kt = c_ref[:, 0].reshape(u, HEAD_DIM, w)
    vt = c_ref[:, 1].reshape(u, HEAD_DIM, w)
    tok = lax.broadcasted_iota(jnp.int32, (1, 1, w), 2)
    in_window = (tok & (rate - 1)) == 0
    newest = tok == w - 1

    sc = jnp.where(in_window, jnp.sum(kt * qc, axis=1, keepdims=True), -jnp.inf)
    sn = jnp.sum(qc * kc, axis=1, keepdims=True)
    m = jnp.maximum(jnp.max(sc, axis=2, keepdims=True), sn)
    p = jnp.exp(sc - m)
    pn = jnp.exp(sn - m)
    l = jnp.sum(p, axis=2, keepdims=True) + pn
    inv = 1.0 / l
    o = jnp.sum(vt * (p * inv), axis=2, keepdims=True) + vc * (pn * inv)
    lse = jnp.broadcast_to(m + jnp.log(l), (u, HEAD_DIM, 1))
    o = o.reshape(nb, ATTN_WIDTH, 1)
    lse = lse.reshape(nb, ATTN_WIDTH, 1)
    o_acc, l_acc = ot_ref[...], lt_ref[...]
    for j in range(nb):
        mine = seq_lane == step * nb + j
        o_acc = jnp.where(mine, o[j], o_acc)
        l_acc = jnp.where(mine, lse[j], l_acc)
    ot_ref[...] = o_acc
    lt_ref[...] = l_acc

    def shifted(old, new_col):
        rolled = pltpu.roll(old.reshape(u * HEAD_DIM, w), w - 1, 1).reshape(u, HEAD_DIM, w)
        return jnp.where(newest, new_col, rolled).reshape(nb, HEADS, HEAD_DIM, w)

    nc_ref[:, 0] = shifted(kt, kc)
    nc_ref[:, 1] = shifted(vt, vc)


def _cache_pass(cache_t, qkvt, g):
    nseq, w = cache_t.shape[0], cache_t.shape[4]
    nb = CACHE_SEQ_BLOCK[g]
    blk = (nb, 2, HEADS, HEAD_DIM, w)
    cspec = pl.BlockSpec(blk, lambda i: (i, 0, 0, 0, 0))
    return pl.pallas_call(
        functools.partial(_cache_kernel, g=g, rate=DIL_RATES[g]),
        grid=(nseq // nb,),
        in_specs=[cspec, _const_spec(qkvt.shape)],
        out_specs=[cspec, _const_spec((ATTN_WIDTH, nseq)), _const_spec((ATTN_WIDTH, nseq))],
        out_shape=[jax.ShapeDtypeStruct(cache_t.shape, F32),
                   jax.ShapeDtypeStruct((ATTN_WIDTH, nseq), F32),
                   jax.ShapeDtypeStruct((ATTN_WIDTH, nseq), F32)],
        compiler_params=_params(("arbitrary",)),
        name=f"cache_pass_g{g}",
    )(cache_t, qkvt)


def _dec_mix_kernel(x_ref, o0_ref, o1_ref, o2_ref, l0_ref, l1_ref, l2_ref, st_ref, glu_ref,
                    cw_ref, cb_ref, clg_ref, clb_ref, wo_ref, lng_ref, lnb_ref, h_ref, nst_ref):
    npre = CONV_WIDTH - 1
    glu = glu_ref[...]
    acc = cw_ref[npre:npre + 1, :] * glu
    for k in range(npre):
        acc = acc + cw_ref[k:k + 1, :] * st_ref[k]
    conv = _conv_tail(acc, cb_ref[...], clg_ref[...], clb_ref[...])
    w = _group_weights((l0_ref[...], l1_ref[...], l2_ref[...]))
    attn = (w[0] * o0_ref[...] + w[1] * o1_ref[...] + w[2] * o2_ref[...]).T
    h_ref[...] = _out_proj_ln(x_ref[...], attn, conv, wo_ref, lng_ref[...], lnb_ref[...])
    nst_ref[0:npre - 1] = st_ref[1:npre]
    nst_ref[npre - 1] = glu


def _decode_mix(x2d, ots, lts, state_t, glu, conv_w, conv_b, clg, clb, wo_bf, lng, lnb):
    nseq = x2d.shape[0]
    args = (x2d, *ots, *lts, state_t, glu, conv_w, conv_b, clg, clb, wo_bf, lng, lnb)
    return pl.pallas_call(
        _dec_mix_kernel,
        grid=(1,),
        in_specs=[_const_spec(a.shape) for a in args],
        out_specs=[_const_spec((nseq, D_MODEL)), _const_spec(state_t.shape)],
        out_shape=[jax.ShapeDtypeStruct((nseq, D_MODEL), F32),
                   jax.ShapeDtypeStruct(state_t.shape, F32)],
        compiler_params=_params(("arbitrary",)),
        name="decode_mix",
    )(*args)


def _dec_pool_kernel(x_ref, st_ref, pw_ref, pb_ref, ps_ref, lng_ref, lnb_ref, out_ref, nst_ref):
    x = x_ref[...]
    parts = []
    for gi, w in enumerate(POOL_WINDOWS):
        lo = gi * POOL_CH
        cur = x[:, lo:lo + POOL_CH]
        tot = cur
        for j in range(1, w):
            tot = tot + st_ref[POOL_PREFIX - j, :, lo:lo + POOL_CH]
        parts.append(tot / float(w) - cur)
    y = _pool_project(parts, pw_ref, pb_ref, ps_ref)
    out_ref[...] = _layer_norm(DN_ALPHA * x + y, lng_ref[...], lnb_ref[...])
    nst_ref[0:POOL_PREFIX - 1] = st_ref[1:POOL_PREFIX]
    nst_ref[POOL_PREFIX - 1] = x


def _decode_pool(h2d, state_t, pw_bf, pb, ps, lng, lnb):
    nseq = h2d.shape[0]
    args = (h2d, state_t, pw_bf, pb, ps, lng, lnb)
    return pl.pallas_call(
        _dec_pool_kernel,
        grid=(1,),
        in_specs=[_const_spec(a.shape) for a in args],
        out_specs=[_const_spec((nseq, D_MODEL)), _const_spec(state_t.shape)],
        out_shape=[jax.ShapeDtypeStruct((nseq, D_MODEL), F32),
                   jax.ShapeDtypeStruct(state_t.shape, F32)],
        compiler_params=_params(("arbitrary",)),
        name="decode_pool",
    )(*args)


def _token_minor(a):
    return jnp.transpose(a, (0, 2, 3, 4, 1))


def _token_major(a):
    return jnp.transpose(a, (0, 4, 1, 2, 3))


def kernel(x_prompt, x_sample, cache_attn_w128, cache_attn_w512, cache_attn_w2048, state_conv, state_pool,
           w_in, b_in, conv_w, conv_b, conv_ln_g, conv_ln_b, w_out, pool_w, pool_b, pool_scale,
           ln_mix_g, ln_mix_b, ln_ffn_g, ln_ffn_b, router_w, router_bias, moe_w1, moe_w3, moe_w2):
    n, s, d = x_prompt.shape
    nseq = x_sample.shape[0]
    past = cache_attn_w2048.shape[2]
    assert d == D_MODEL and x_sample.shape[1] == 1 and s % ROW_TILE == 0 and s == DIL_WINDOWS[2]
    assert cache_attn_w128.shape[0] == 1 and past == DIL_WINDOWS[2] and nseq % max(CACHE_SEQ_BLOCK) == 0
    caches = (cache_attn_w128[0], cache_attn_w512[0], cache_attn_w2048[0])

    w_in_bf = w_in[0].astype(BF16)
    w_out_bf = w_out[0].astype(BF16)
    pool_w_bf = pool_w[0].astype(BF16)
    w1_bf, w3_bf, w2_bf = moe_w1.astype(BF16), moe_w3.astype(BF16), moe_w2.astype(BF16)
    rw_pad = jnp.pad(router_w, ((0, 0), (0, LANES - N_EXPERTS))).astype(BF16)
    rb = router_bias.astype(F32).reshape(N_EXPERTS, 1)
    r2 = lambda v: v.reshape(1, -1)
    b_in2 = r2(b_in[0])
    cb, clg, clb = r2(conv_b[0]), r2(conv_ln_g[0]), r2(conv_ln_b[0])
    ps = r2(pool_scale[0])

    def moe(h2d, layer, tm):
        return _moe(h2d, rw_pad, rb, w1_bf[layer], w3_bf[layer], w2_bf[layer],
                    r2(ln_ffn_g[layer]), r2(ln_ffn_b[layer]), tm)

    xp = x_prompt.reshape(n * s, d)
    tabs_p = _rope_tables(jnp.arange(s, dtype=jnp.int32))
    (q0, q1, q2, kb0, kb1, kb2, glu, kt0, kt1, kt2) = _project(xp, w_in_bf, b_in2, tabs_p, n, s, ROW_TILE)
    os_, lses = [], []
    for g, (q, kvb) in enumerate(((q0, kb0), (q1, kb1), (q2, kb2))):
        o, l = _prompt_attention(q, kvb, g, n, s)
        os_.append(o)
        lses.append(l)
    h = _prompt_mix(xp, os_, lses, glu, conv_w[0], cb, clg, clb, w_out_bf,
                    r2(ln_mix_g[0]), r2(ln_mix_b[0]), n, s, ROW_TILE)
    h = moe(h, 0, ROW_TILE)
    p_pool = h.reshape(n, s, d)[:, s - POOL_PREFIX:][None]
    h = _prompt_pool(h, pool_w_bf, pool_b[0], ps, r2(ln_mix_g[1]), r2(ln_mix_b[1]), n, s, ROW_TILE)
    y_prompt = moe(h, 1, ROW_TILE).reshape(n, s, d)
    p_attn = [_token_major(kt.reshape(n, 2, HEADS, HEAD_DIM, kt.shape[2]))[None] for kt in (kt0, kt1, kt2)]
    p_conv = glu.reshape(n, s, CONV_CH)[:, s - (CONV_WIDTH - 1):][None]

    xs = x_sample.reshape(nseq, d)
    ang = _rope_angles(past + jnp.arange(1, dtype=jnp.int32))
    cs = jnp.stack([jnp.broadcast_to(jnp.cos(ang).T, (ROT_HALF, nseq)),
                    jnp.broadcast_to(jnp.sin(ang).T, (ROT_HALF, nseq))])
    qkvt, glus = _decode_project(xs, w_in_bf, b_in2, cs)
    ots, lts, s_attn = [], [], []
    for g in range(N_DIL):
        new_cache, ot, lt = _cache_pass(_token_minor(caches[g]), qkvt, g)
        ots.append(ot)
        lts.append(lt)
        s_attn.append(_token_major(new_cache)[None])
    conv_t = jnp.transpose(state_conv[0], (1, 0, 2))
    hs, new_conv = _decode_mix(xs, ots, lts, conv_t, glus, conv_w[0], cb, clg, clb, w_out_bf,
                               r2(ln_mix_g[0]), r2(ln_mix_b[0]))
    hs = moe(hs, 0, nseq)
    pool_t = jnp.transpose(state_pool[0], (1, 0, 2))
    hs, new_pool = _decode_pool(hs, pool_t, pool_w_bf, pool_b[0], ps,
                                r2(ln_mix_g[1]), r2(ln_mix_b[1]))
    y_sample = moe(hs, 1, nseq).reshape(nseq, 1, d)
    s_conv = jnp.transpose(new_conv, (1, 0, 2))[None]
    s_pool = jnp.transpose(new_pool, (1, 0, 2))[None]

    return (y_prompt, y_sample, p_attn[0], p_attn[1], p_attn[2], p_conv, p_pool,
            s_attn[0], s_attn[1], s_attn[2], s_conv, s_pool)
```

```python
import functools

import jax
import jax.numpy as jnp
from jax import lax
from jax.experimental import pallas as pl
from jax.experimental.pallas import tpu as pltpu

F32 = jnp.float32
BF16 = jnp.bfloat16

D_MODEL = 1024
HEAD_DIM = 64
HEADS = 8
ATTN_WIDTH = HEADS * HEAD_DIM
N_DIL = 3
DIL_WINDOWS = (128, 512, 2048)
DIL_RATES = (1, 4, 16)
ATTN_BLOCK = 128
ATTN_SCALE = HEAD_DIM ** -0.5
ROT_DIM = HEAD_DIM // 4
ROT_HALF = ROT_DIM // 2
ROPE_THETA = 500000.0
QKV_COLS = 3 * N_DIL * ATTN_WIDTH
CONV_CH = D_MODEL // 4
CONV_WIDTH = 31
IN_COLS = QKV_COLS + 2 * CONV_CH
POOL_WINDOWS = (2, 4, 8, 16)
POOL_CH = D_MODEL // len(POOL_WINDOWS)
POOL_PREFIX = max(POOL_WINDOWS) - 1
N_EXPERTS = 16
N_EXPERT_GROUPS = 4
EXPERTS_PER_GROUP = 4
D_EXPERT = 256
DEPTH = 2
DN_ALPHA = (2.0 * DEPTH) ** 0.25
LN_EPS = 1e-5

LANES = 128
SUBLANES = 8
VMEM_LIMIT_BYTES = 56 * 1024 * 1024
ROW_TILE = 512
CONV_HALO = 32
POOL_HALO = 16
ATTN_RES_BLOCK = 4
STREAM_VMEM_LIMIT_BYTES = 60 * 1024 * 1024
MOE_SUBSTEPS = 8
EXPERTS_PER_SUBSTEP = N_EXPERTS // MOE_SUBSTEPS
CACHE_HEAD_BLOCK = 4
HEAD_PAIRS = ATTN_WIDTH // LANES

NT_DIMS = (((1,), (1,)), ((), ()))


def _params(sem):
    return pltpu.CompilerParams(dimension_semantics=sem, vmem_limit_bytes=VMEM_LIMIT_BYTES)


def _const_spec(shape):
    nd = len(shape)
    return pl.BlockSpec(shape, lambda *_: (0,) * nd)


def _resident_spec(shape):
    nd = len(shape)
    return pl.BlockSpec(shape, lambda *_: (0,) * nd, pipeline_mode=pl.Buffered(1))


def _layer_norm(x, g, b):
    mu = jnp.mean(x, axis=-1, keepdims=True)
    xc = x - mu
    var = jnp.mean(xc * xc, axis=-1, keepdims=True)
    return xc * lax.rsqrt(var + LN_EPS) * g + b


def _proj_kernel(x_ref, w_ref, b_ref, tab_ref, q0_ref, q1_ref, q2_ref, kb0_ref, kb1_ref, kb2_ref,
                 glu_ref, kt0_ref, kt1_ref, kt2_ref):
    tm = x_ref.shape[0]
    last = pl.program_id(1) == pl.num_programs(1) - 1
    xb = x_ref[...].astype(BF16)
    cosm, sin_lo, sin_hi = tab_ref[0], tab_ref[1], tab_ref[2]

    def proj(c0, width):
        return (jnp.dot(xb, w_ref[:, c0:c0 + width], preferred_element_type=F32)
                + b_ref[:, c0:c0 + width])

    def rope(t):
        parts = []
        for j in range(t.shape[1] // LANES):
            v = t[:, j * LANES:(j + 1) * LANES]
            parts.append(v * cosm
                         + pltpu.roll(v, LANES - ROT_HALF, 1) * sin_lo
                         + pltpu.roll(v, ROT_HALF, 1) * sin_hi)
        return jnp.concatenate(parts, axis=1)

    q_refs = (q0_ref, q1_ref, q2_ref)
    kb_refs = (kb0_ref, kb1_ref, kb2_ref)
    for g in range(N_DIL):
        c = g * ATTN_WIDTH
        q_refs[g][...] = (rope(proj(c, ATTN_WIDTH)) * ATTN_SCALE).astype(BF16)
        k = rope(proj(N_DIL * ATTN_WIDTH + c, ATTN_WIDTH))
        v = proj(2 * N_DIL * ATTN_WIDTH + c, ATTN_WIDTH)
        kb_refs[g][:, 0:ATTN_WIDTH] = k.astype(BF16)
        kb_refs[g][:, ATTN_WIDTH:2 * ATTN_WIDTH] = v.astype(BF16)
        if g == 2:
            kt2_ref[0, 0:ATTN_WIDTH, :] = k.T
            kt2_ref[0, ATTN_WIDTH:2 * ATTN_WIDTH, :] = v.T
        elif g == 1:
            @pl.when(last)
            def _():
                kt1_ref[0, 0:ATTN_WIDTH, :] = k.T
                kt1_ref[0, ATTN_WIDTH:2 * ATTN_WIDTH, :] = v.T
        else:
            @pl.when(last)
            def _():
                keep = DIL_WINDOWS[0]
                kt0_ref[0, 0:ATTN_WIDTH, :] = k[tm - keep:, :].T
                kt0_ref[0, ATTN_WIDTH:2 * ATTN_WIDTH, :] = v[tm - keep:, :].T
    ga = proj(QKV_COLS, CONV_CH)
    gb = proj(QKV_COLS + CONV_CH, CONV_CH)
    glu_ref[...] = ga * jax.nn.sigmoid(gb)


def _project(x2d, w_bf, b2d, tables, n, s, tm):
    assert tm == DIL_WINDOWS[1] and tm >= DIL_WINDOWS[0]
    tps = s // tm
    m = n * s
    row = lambda b, t: (b * tps + t, 0)
    seq = lambda b, t: (b, 0, 0)
    kv_rows = 2 * ATTN_WIDTH
    out_specs = ([pl.BlockSpec((tm, ATTN_WIDTH), row)] * N_DIL
                 + [pl.BlockSpec((tm, kv_rows), row)] * N_DIL
                 + [pl.BlockSpec((tm, CONV_CH), row),
                    pl.BlockSpec((1, kv_rows, DIL_WINDOWS[0]), seq),
                    pl.BlockSpec((1, kv_rows, DIL_WINDOWS[1]), seq),
                    pl.BlockSpec((1, kv_rows, tm), lambda b, t: (b, 0, t))])
    out_shape = ([jax.ShapeDtypeStruct((m, ATTN_WIDTH), BF16)] * N_DIL
                 + [jax.ShapeDtypeStruct((m, kv_rows), BF16)] * N_DIL
                 + [jax.ShapeDtypeStruct((m, CONV_CH), F32),
                    jax.ShapeDtypeStruct((n, kv_rows, DIL_WINDOWS[0]), F32),
                    jax.ShapeDtypeStruct((n, kv_rows, DIL_WINDOWS[1]), F32),
                    jax.ShapeDtypeStruct((n, kv_rows, s), F32)])
    return pl.pallas_call(
        _proj_kernel,
        grid=(n, tps),
        in_specs=[
            pl.BlockSpec((tm, D_MODEL), row),
            _resident_spec((D_MODEL, IN_COLS)),
            _const_spec((1, IN_COLS)),
            pl.BlockSpec((3, tm, LANES), lambda b, t: (0, t, 0)),
        ],
        out_specs=out_specs,
        out_shape=out_shape,
        compiler_params=_params(("parallel", "arbitrary")),
        name="in_proj",
    )(x2d, w_bf, b2d, tables)


def _rope_angles(pos):
    inv_freq = ROPE_THETA ** (-jnp.arange(ROT_HALF, dtype=F32) * 2.0 / ROT_DIM)
    return pos.astype(F32)[:, None] * inv_freq[None, :]


def _rope_tables(pos):
    t = pos.shape[0]
    ang = _rope_angles(pos)
    cos, sin = jnp.cos(ang), jnp.sin(ang)
    rest = HEAD_DIM - ROT_DIM
    c64 = jnp.concatenate([cos, cos, jnp.ones((t, rest), F32)], axis=1)
    lo64 = jnp.concatenate([-sin, jnp.zeros((t, HEAD_DIM - ROT_HALF), F32)], axis=1)
    hi64 = jnp.concatenate([jnp.zeros((t, ROT_HALF), F32), sin, jnp.zeros((t, rest), F32)], axis=1)
    rep = LANES // HEAD_DIM
    return jnp.stack([jnp.tile(c64, (1, rep)), jnp.tile(lo64, (1, rep)), jnp.tile(hi64, (1, rep))])


def _attn_kernel(q_ref, kv_ref, bias_ref, o_ref, lse_ref, s_ref, p_ref, *, rb, nblk):
    nk = s_ref.shape[2]
    lane = lax.broadcasted_iota(jnp.int32, (ATTN_BLOCK, LANES), 1)
    low_half = lane < HEAD_DIM
    keep_lo = low_half.astype(BF16)
    keep_hi = 1 - keep_lo
    ones_rhs = jnp.ones((nk, LANES), BF16)

    def unit(j, b):
        if nblk > 1:
            q0 = pl.multiple_of(b * ATTN_BLOCK, ATTN_BLOCK)
            k0 = pl.multiple_of(jnp.maximum(b - 1, 0) * ATTN_BLOCK, ATTN_BLOCK)
            bias = bias_ref[jnp.minimum(b, 1)]
        else:
            q0 = k0 = 0
            bias = bias_ref[0]
        rows_q = pl.ds(q0, ATTN_BLOCK)
        rows_k = pl.ds(k0, nk)
        for hp in range(HEAD_PAIRS):
            qp = q_ref[0, rows_q, j * ATTN_WIDTH + hp * LANES:j * ATTN_WIDTH + (hp + 1) * LANES]
            kp = kv_ref[0, rows_k, j * 2 * ATTN_WIDTH + hp * LANES:j * 2 * ATTN_WIDTH + (hp + 1) * LANES]
            for half, keep in enumerate((keep_lo, keep_hi)):
                s = lax.dot_general(qp * keep, kp, NT_DIMS, preferred_element_type=F32)
                s_ref[2 * hp + half] = s + bias
        sc = s_ref[...]
        m = jnp.max(sc, axis=-1, keepdims=True)
        p_ref[...] = jnp.exp(sc - m).astype(BF16)
        for hp in range(HEAD_PAIRS):
            c0 = j * 2 * ATTN_WIDTH + ATTN_WIDTH + hp * LANES
            rhs = jnp.concatenate([kv_ref[0, rows_k, c0:c0 + LANES], ones_rhs], axis=1)
            ol_lo = jnp.dot(p_ref[2 * hp], rhs, preferred_element_type=F32)
            ol_hi = jnp.dot(p_ref[2 * hp + 1], rhs, preferred_element_type=F32)
            l_lo, l_hi = ol_lo[:, LANES:], ol_hi[:, LANES:]
            o_pair = jnp.where(low_half, ol_lo[:, :LANES] * (1.0 / l_lo), ol_hi[:, :LANES] * (1.0 / l_hi))
            lse_pair = jnp.where(low_half, m[2 * hp] + jnp.log(l_lo), m[2 * hp + 1] + jnp.log(l_hi))
            cols = slice(j * ATTN_WIDTH + hp * LANES, j * ATTN_WIDTH + (hp + 1) * LANES)
            o_ref[0, rows_q, cols] = o_pair
            lse_ref[0, rows_q, cols] = lse_pair

    def block(b, carry):
        for j in range(rb):
            unit(j, b)
        return carry

    if nblk > 1:
        lax.fori_loop(0, nblk, block, 0)
    else:
        block(0, 0)


def _band_bias(nblk):
    qi = jnp.arange(ATTN_BLOCK)[:, None]
    ci = jnp.arange(ATTN_BLOCK)[None, :]
    causal = jnp.where(ci <= qi, 0.0, -jnp.inf).astype(F32)
    if nblk == 1:
        return jnp.stack([causal, causal])
    band = jnp.where(ci >= qi, 0.0, -jnp.inf).astype(F32)
    closed = jnp.full((ATTN_BLOCK, ATTN_BLOCK), -jnp.inf, F32)
    return jnp.stack([jnp.concatenate([causal, closed], axis=1), jnp.concatenate([band, causal], axis=1)])


def _prompt_attention(q, kvb, g, n, s):
    rate = DIL_RATES[g]
    sub = s // rate
    nblk = sub // ATTN_BLOCK
    rb = min(rate, ATTN_RES_BLOCK)
    nk = 2 * ATTN_BLOCK if nblk > 1 else ATTN_BLOCK
    qv = q.reshape(n, sub, rate * ATTN_WIDTH)
    kvv = kvb.reshape(n, sub, rate * 2 * ATTN_WIDTH)
    out_sds = jax.ShapeDtypeStruct((n, sub, rate * ATTN_WIDTH), F32)
    blk = lambda b, r: (b, 0, r)
    o_spec = pl.BlockSpec((1, sub, rb * ATTN_WIDTH), blk)
    o, lse = pl.pallas_call(
        functools.partial(_attn_kernel, rb=rb, nblk=nblk),
        grid=(n, rate // rb),
        in_specs=[
            pl.BlockSpec((1, sub, rb * ATTN_WIDTH), blk),
            pl.BlockSpec((1, sub, rb * 2 * ATTN_WIDTH), blk),
            _const_spec((2, ATTN_BLOCK, nk)),
        ],
        out_specs=[o_spec, o_spec],
        out_shape=[out_sds, out_sds],
        scratch_shapes=[pltpu.VMEM((HEADS, ATTN_BLOCK, nk), F32),
                        pltpu.VMEM((HEADS, ATTN_BLOCK, nk), BF16)],
        compiler_params=_params(("parallel", "parallel")),
        name=f"band_attn_g{g}",
    )(qv, kvv, _band_bias(nblk))
    return o.reshape(n * s, ATTN_WIDTH), lse.reshape(n * s, ATTN_WIDTH)


def _group_weights(lses):
    lmax = jnp.maximum(jnp.maximum(lses[0], lses[1]), lses[2])
    es = [jnp.exp(l - lmax) for l in lses]
    inv = 1.0 / (es[0] + es[1] + es[2])
    return [e * inv for e in es]


def _conv_tail(y, cb, clg, clb):
    z = _layer_norm(y + cb, clg, clb)
    return z * jax.nn.sigmoid(z)


def _out_proj_ln(x, attn, conv, wo_ref, lng, lnb):
    y = (jnp.dot(attn.astype(BF16), wo_ref[0:ATTN_WIDTH, :], preferred_element_type=F32)
         + jnp.dot(conv.astype(BF16), wo_ref[ATTN_WIDTH:ATTN_WIDTH + CONV_CH, :],
                   preferred_element_type=F32))
    return _layer_norm(DN_ALPHA * x + y, lng, lnb)


def _mix_kernel(x_ref, o0_ref, o1_ref, o2_ref, l0_ref, l1_ref, l2_ref, gc_ref, gp_ref,
                cw_ref, cb_ref, clg_ref, clb_ref, wo_ref, lng_ref, lnb_ref, h_ref, ext_ref):
    tm = x_ref.shape[0]
    first = pl.program_id(1) == 0
    ext_ref[0:CONV_HALO, :] = jnp.where(first, 0.0, gp_ref[...])
    ext_ref[CONV_HALO:CONV_HALO + tm, :] = gc_ref[...]
    base = CONV_HALO - (CONV_WIDTH - 1)
    acc = cw_ref[0:1, :] * ext_ref[base:base + tm, :]
    for k in range(1, CONV_WIDTH):
        acc = acc + cw_ref[k:k + 1, :] * ext_ref[base + k:base + k + tm, :]
    conv = _conv_tail(acc, cb_ref[...], clg_ref[...], clb_ref[...])
    w = _group_weights((l0_ref[...], l1_ref[...], l2_ref[...]))
    attn = w[0] * o0_ref[...] + w[1] * o1_ref[...] + w[2] * o2_ref[...]
    h_ref[...] = _out_proj_ln(x_ref[...], attn, conv, wo_ref, lng_ref[...], lnb_ref[...])


def _prompt_mix(x2d, os_, lses, glu, conv_w, conv_b, clg, clb, wo_bf, lng, lnb, n, s, tm):
    tps = s // tm
    hb = tm // CONV_HALO
    row = lambda b, t: (b * tps + t, 0)
    aw = pl.BlockSpec((tm, ATTN_WIDTH), row)
    return pl.pallas_call(
        _mix_kernel,
        grid=(n, tps),
        in_specs=[
            pl.BlockSpec((tm, D_MODEL), row),
            aw, aw, aw, aw, aw, aw,
            pl.BlockSpec((tm, CONV_CH), row),
            pl.BlockSpec((CONV_HALO, CONV_CH),
                         lambda b, t: (jnp.maximum((b * tps + t) * hb - 1, 0), 0)),
            _const_spec((CONV_WIDTH, CONV_CH)),
            _const_spec((1, CONV_CH)), _const_spec((1, CONV_CH)), _const_spec((1, CONV_CH)),
            _const_spec((ATTN_WIDTH + CONV_CH, D_MODEL)),
            _const_spec((1, D_MODEL)), _const_spec((1, D_MODEL)),
        ],
        out_specs=pl.BlockSpec((tm, D_MODEL), row),
        out_shape=jax.ShapeDtypeStruct((n * s, D_MODEL), F32),
        scratch_shapes=[pltpu.VMEM((CONV_HALO + tm, CONV_CH), F32)],
        compiler_params=_params(("parallel", "parallel")),
        name="mix_out",
    )(x2d, *os_, *lses, glu, glu, conv_w, conv_b, clg, clb, wo_bf, lng, lnb)


def _routing_rows(logit_rows, bias_ref):
    m = logit_rows[0]
    for r in logit_rows[1:]:
        m = jnp.maximum(m, r)
    ex = [jnp.exp(r - m) for r in logit_rows]
    tot = ex[0]
    for e in ex[1:]:
        tot = tot + e
    scores = [e / tot for e in ex]
    sel = [scores[e] + bias_ref[e:e + 1, :] for e in range(N_EXPERTS)]
    grp = []
    for g in range(N_EXPERT_GROUPS):
        v = sel[g * EXPERTS_PER_GROUP:(g + 1) * EXPERTS_PER_GROUP]
        best = v[0] + v[1]
        for i in range(EXPERTS_PER_GROUP):
            for j in range(i + 1, EXPERTS_PER_GROUP):
                if (i, j) != (0, 1):
                    best = jnp.maximum(best, v[i] + v[j])
        grp.append(best)
    gmax = grp[0]
    for v in grp[1:]:
        gmax = jnp.maximum(gmax, v)
    taken = None
    in_group = []
    for g in range(N_EXPERT_GROUPS):
        hit = grp[g] == gmax
        if taken is None:
            in_group.append(hit)
            taken = hit
        else:
            in_group.append(jnp.logical_and(hit, jnp.logical_not(taken)))
            taken = jnp.logical_or(taken, hit)
    gates = []
    for e in range(N_EXPERTS):
        g = e // EXPERTS_PER_GROUP
        rank = jnp.zeros_like(sel[e])
        for o in range(g * EXPERTS_PER_GROUP, (g + 1) * EXPERTS_PER_GROUP):
            if o == e:
                continue
            ahead = sel[o] > sel[e]
            if o < e:
                ahead = jnp.logical_or(ahead, sel[o] == sel[e])
            rank = rank + ahead.astype(F32)
        chosen = jnp.logical_and(in_group[g], rank < float(2))
        gates.append(jnp.where(chosen, scores[e], 0.0))
    den = gates[0]
    for v in gates[1:]:
        den = den + v
    return [v / den for v in gates]


def _moe_kernel(h_ref, rw_ref, rb_ref, w1_ref, w3_ref, w2_ref, lng_ref, lnb_ref, out_ref, ct_ref):
    h = h_ref[...]
    hb = h.astype(BF16)
    logits = jnp.dot(hb, rw_ref[...], preferred_element_type=F32)
    lt = logits.T
    rows = [lt[e:e + 1, :] for e in range(N_EXPERTS)]
    comb_rows = _routing_rows(rows, rb_ref)
    ct_ref[...] = jnp.zeros_like(ct_ref)
    for e in range(N_EXPERTS):
        ct_ref[e:e + 1, :] = comb_rows[e]
    comb = ct_ref[...].T
    acc = jnp.zeros(h.shape, F32)
    for e in range(N_EXPERTS):
        a = jnp.dot(hb, w1_ref[e], preferred_element_type=F32)
        b = jnp.dot(hb, w3_ref[e], preferred_element_type=F32)
        gated = (a * jax.nn.sigmoid(a)) * b * comb[:, e:e + 1]
        acc = acc + jnp.dot(gated.astype(BF16), w2_ref[e], preferred_element_type=F32)
    out_ref[...] = _layer_norm(DN_ALPHA * h + acc, lng_ref[...], lnb_ref[...])


def _moe(h2d, rw_pad, rb, w1, w3, w2, lng, lnb, tm):
    m = h2d.shape[0]
    row = lambda i: (i, 0)
    return pl.pallas_call(
        _moe_kernel,
        grid=(m // tm,),
        in_specs=[
            pl.BlockSpec((tm, D_MODEL), row),
            _const_spec((D_MODEL, LANES)),
            _const_spec((N_EXPERTS, 1)),
            _resident_spec((N_EXPERTS, D_MODEL, D_EXPERT)),
            _resident_spec((N_EXPERTS, D_MODEL, D_EXPERT)),
            _resident_spec((N_EXPERTS, D_EXPERT, D_MODEL)),
            _const_spec((1, D_MODEL)), _const_spec((1, D_MODEL)),
        ],
        out_specs=pl.BlockSpec((tm, D_MODEL), row),
        out_shape=jax.ShapeDtypeStruct((m, D_MODEL), F32),
        scratch_shapes=[pltpu.VMEM((LANES, tm), F32)],
        compiler_params=_params(("parallel",)),
        name="moe_ffn",
    )(h2d, rw_pad, rb, w1, w3, w2, lng, lnb)


def _pool_project(parts, pw_ref, pb_ref, ps_ref):
    cols = []
    for gi in range(len(POOL_WINDOWS)):
        lo = gi * POOL_CH
        y = jnp.dot(parts[gi].astype(BF16), pw_ref[gi], preferred_element_type=F32)
        cols.append((y + pb_ref[gi:gi + 1, :]) * ps_ref[:, lo:lo + POOL_CH])
    return jnp.concatenate(cols, axis=1)


def _pool_kernel(x_ref, xp_ref, pw_ref, pb_ref, ps_ref, lng_ref, lnb_ref, out_ref, ext_ref):
    tm = x_ref.shape[0]
    t = pl.program_id(1)
    ext_ref[0:POOL_HALO, :] = jnp.where(t == 0, 0.0, xp_ref[...])
    ext_ref[POOL_HALO:POOL_HALO + tm, :] = x_ref[...]
    pos = t * tm + lax.broadcasted_iota(jnp.int32, (tm, 1), 0)
    parts = []
    for gi, w in enumerate(POOL_WINDOWS):
        lo = gi * POOL_CH
        cur = x_ref[:, lo:lo + POOL_CH]
        tot = cur
        for j in range(1, w):
            tot = tot + ext_ref[POOL_HALO - j:POOL_HALO - j + tm, lo:lo + POOL_CH]
        cnt = jnp.minimum(w, pos + 1).astype(F32)
        parts.append(tot / cnt - cur)
    y = _pool_project(parts, pw_ref, pb_ref, ps_ref)
    out_ref[...] = _layer_norm(DN_ALPHA * x_ref[...] + y, lng_ref[...], lnb_ref[...])


def _prompt_pool(h2d, pw_bf, pb, ps, lng, lnb, n, s, tm):
    tps = s // tm
    hb = tm // POOL_HALO
    row = lambda b, t: (b * tps + t, 0)
    ng = len(POOL_WINDOWS)
    return pl.pallas_call(
        _pool_kernel,
        grid=(n, tps),
        in_specs=[
            pl.BlockSpec((tm, D_MODEL), row),
            pl.BlockSpec((POOL_HALO, D_MODEL),
                         lambda b, t: (jnp.maximum((b * tps + t) * hb - 1, 0), 0)),
            _const_spec((ng, POOL_CH, POOL_CH)),
            _const_spec((ng, POOL_CH)),
            _const_spec((1, D_MODEL)), _const_spec((1, D_MODEL)), _const_spec((1, D_MODEL)),
        ],
        out_specs=pl.BlockSpec((tm, D_MODEL), row),
        out_shape=jax.ShapeDtypeStruct((n * s, D_MODEL), F32),
        scratch_shapes=[pltpu.VMEM((POOL_HALO + tm, D_MODEL), F32)],
        compiler_params=_params(("parallel", "parallel")),
        name="pool_mix",
    )(h2d, h2d, pw_bf, pb, ps, lng, lnb)


def _dec_proj_kernel(x_ref, w_ref, b_ref, cs_ref, qkvt_ref, glu_ref):
    xb = x_ref[...].astype(BF16)
    cos, sin = cs_ref[0], cs_ref[1]
    chunk = ATTN_WIDTH

    def proj(c0, width):
        return (jnp.dot(xb, w_ref[:, c0:c0 + width], preferred_element_type=F32)
                + b_ref[:, c0:c0 + width])

    for ci in range(QKV_COLS // chunk):
        pt = proj(ci * chunk, chunk).T
        if ci < 2 * N_DIL:
            pieces = []
            for hh in range(HEADS):
                base = hh * HEAD_DIM
                x1 = pt[base:base + ROT_HALF, :]
                x2 = pt[base + ROT_HALF:base + ROT_DIM, :]
                pieces += [x1 * cos - x2 * sin, x2 * cos + x1 * sin, pt[base + ROT_DIM:base + HEAD_DIM, :]]
            pt = jnp.concatenate(pieces, axis=0)
            if ci < N_DIL:
                pt = pt * ATTN_SCALE
        qkvt_ref[ci * chunk:(ci + 1) * chunk, :] = pt
    ga = proj(QKV_COLS, CONV_CH)
    gb = proj(QKV_COLS + CONV_CH, CONV_CH)
    glu_ref[...] = ga * jax.nn.sigmoid(gb)


def _decode_project(x2d, w_bf, b2d, cs):
    nseq = x2d.shape[0]
    args = (x2d, w_bf, b2d, cs)
    return pl.pallas_call(
        _dec_proj_kernel,
        grid=(1,),
        in_specs=[_const_spec(a.shape) for a in args],
        out_specs=[_const_spec((QKV_COLS, nseq)), _const_spec((nseq, CONV_CH))],
        out_shape=[jax.ShapeDtypeStruct((QKV_COLS, nseq), F32),
                   jax.ShapeDtypeStruct((nseq, CONV_CH), F32)],
        compiler_params=_params(("arbitrary",)),
        name="decode_proj",
    )(*args)


def _cache_block(c_ref, nc_ref, ot_ref, lt_ref, qkvt_ref, g, seq, head0):
    hb, w = c_ref.shape[2], c_ref.shape[4]
    nseq = qkvt_ref.shape[1]
    rate = DIL_RATES[g]
    rows = hb * HEAD_DIM
    r0 = pl.multiple_of(head0 * HEAD_DIM, rows)
    mine = lax.broadcasted_iota(jnp.int32, (rows, nseq), 1) == seq

    def column(base):
        x = qkvt_ref[pl.ds(base + r0, rows), :]
        return jnp.sum(jnp.where(mine, x, 0.0), axis=1, keepdims=True).reshape(hb, HEAD_DIM, 1)

    qc = column(g * ATTN_WIDTH)
    kc = column((N_DIL + g) * ATTN_WIDTH)
    vc = column((2 * N_DIL + g) * ATTN_WIDTH)
    kt = c_ref[0, 0]
    vt = c_ref[0, 1]
    tok = lax.broadcasted_iota(jnp.int32, (1, 1, w), 2)
    in_window = (tok & (rate - 1)) == 0
    newest = tok == w - 1

    sc = jnp.where(in_window, jnp.sum(kt * qc, axis=1, keepdims=True), -jnp.inf)
    sn = jnp.sum(qc * kc, axis=1, keepdims=True)
    m = jnp.maximum(jnp.max(sc, axis=2, keepdims=True), sn)
    p = jnp.exp(sc - m)
    pn = jnp.exp(sn - m)
    l = jnp.sum(p, axis=2, keepdims=True) + pn
    inv = 1.0 / l
    o = jnp.sum(vt * (p * inv), axis=2, keepdims=True) + vc * (pn * inv)
    lse = jnp.broadcast_to(m + jnp.log(l), (hb, HEAD_DIM, 1))
    acc_rows = pl.ds(r0, rows)
    ot_ref[acc_rows, :] = jnp.where(mine, o.reshape(rows, 1), ot_ref[acc_rows, :])
    lt_ref[acc_rows, :] = jnp.where(mine, lse.reshape(rows, 1), lt_ref[acc_rows, :])

    def shifted(old, new_col):
        rolled = pltpu.roll(old.reshape(rows, w), w - 1, 1).reshape(hb, HEAD_DIM, w)
        return jnp.where(newest, new_col, rolled)

    nc_ref[0, 0] = shifted(kt, kc)
    nc_ref[0, 1] = shifted(vt, vc)


def _moe_stream_kernel(*refs, groups):
    ng = len(groups)
    (h_ref, rw_ref, rb_ref, w1_ref, w3_ref, w2_ref, lng_ref, lnb_ref, qkvt_ref) = refs[:9]
    c_refs = refs[9:9 + ng]
    out_ref = refs[9 + ng]
    stream_out = refs[10 + ng:10 + 4 * ng]
    hb_ref, comb_ref, ct_ref = refs[10 + 4 * ng:]
    i, k = pl.program_id(0), pl.program_id(1)
    flat = i * MOE_SUBSTEPS + k

    @pl.when(k == 0)
    def _():
        hb = h_ref[...].astype(BF16)
        hb_ref[...] = hb
        logits = jnp.dot(hb, rw_ref[...], preferred_element_type=F32)
        lt = logits.T
        comb_rows = _routing_rows([lt[e:e + 1, :] for e in range(N_EXPERTS)], rb_ref)
        ct_ref[...] = jnp.zeros_like(ct_ref)
        for e in range(N_EXPERTS):
            ct_ref[e:e + 1, :] = comb_rows[e]
        comb_ref[...] = ct_ref[...].T
        out_ref[...] = jnp.zeros_like(out_ref)

    @pl.when(flat == 0)
    def _():
        for s in range(ng):
            stream_out[3 * s + 1][...] = jnp.zeros_like(stream_out[3 * s + 1])
            stream_out[3 * s + 2][...] = jnp.zeros_like(stream_out[3 * s + 2])

    hb = hb_ref[...]
    comb = comb_ref[...]
    lane = lax.broadcasted_iota(jnp.int32, comb.shape, 1)
    for ee in range(EXPERTS_PER_SUBSTEP):
        e = k * EXPERTS_PER_SUBSTEP + ee
        a = jnp.dot(hb, w1_ref[e], preferred_element_type=F32)
        b = jnp.dot(hb, w3_ref[e], preferred_element_type=F32)
        gate = jnp.sum(jnp.where(lane == e, comb, 0.0), axis=1, keepdims=True)
        gated = (a * jax.nn.sigmoid(a)) * b * gate
        out_ref[...] += jnp.dot(gated.astype(BF16), w2_ref[e], preferred_element_type=F32)

    blocks_per_seq = HEADS // CACHE_HEAD_BLOCK
    for s, g in enumerate(groups):
        _cache_block(c_refs[s], stream_out[3 * s], stream_out[3 * s + 1], stream_out[3 * s + 2], qkvt_ref,
                     g, flat // blocks_per_seq, (flat % blocks_per_seq) * CACHE_HEAD_BLOCK)

    @pl.when(k == MOE_SUBSTEPS - 1)
    def _():
        out_ref[...] = _layer_norm(DN_ALPHA * h_ref[...] + out_ref[...], lng_ref[...], lnb_ref[...])


def _moe_stream(h2d, rw_pad, rb, w1, w3, w2, lng, lnb, qkvt, caches_t, groups, tm):
    m = h2d.shape[0]
    nseq = qkvt.shape[1]
    blocks_per_seq = HEADS // CACHE_HEAD_BLOCK
    assert (m // tm) * MOE_SUBSTEPS == nseq * blocks_per_seq
    row = lambda i, k: (i, 0)
    acc_spec = _const_spec((ATTN_WIDTH, nseq))
    acc_sds = jax.ShapeDtypeStruct((ATTN_WIDTH, nseq), F32)
    cache_specs, stream_specs, stream_shapes = [], [], []
    for c in caches_t:
        spec = pl.BlockSpec((1, 2, CACHE_HEAD_BLOCK, HEAD_DIM, c.shape[4]),
                            lambda i, k: ((i * MOE_SUBSTEPS + k) // blocks_per_seq, 0,
                                          (i * MOE_SUBSTEPS + k) % blocks_per_seq, 0, 0))
        cache_specs.append(spec)
        stream_specs += [spec, acc_spec, acc_spec]
        stream_shapes += [jax.ShapeDtypeStruct(c.shape, F32), acc_sds, acc_sds]
    return pl.pallas_call(
        functools.partial(_moe_stream_kernel, groups=tuple(groups)),
        grid=(m // tm, MOE_SUBSTEPS),
        in_specs=[
            pl.BlockSpec((tm, D_MODEL), row),
            _const_spec((D_MODEL, LANES)),
            _const_spec((N_EXPERTS, 1)),
            _resident_spec((N_EXPERTS, D_MODEL, D_EXPERT)),
            _resident_spec((N_EXPERTS, D_MODEL, D_EXPERT)),
            _resident_spec((N_EXPERTS, D_EXPERT, D_MODEL)),
            _const_spec((1, D_MODEL)), _const_spec((1, D_MODEL)),
            _resident_spec(qkvt.shape),
        ] + cache_specs,
        out_specs=[pl.BlockSpec((tm, D_MODEL), row)] + stream_specs,
        out_shape=[jax.ShapeDtypeStruct((m, D_MODEL), F32)] + stream_shapes,
        scratch_shapes=[pltpu.VMEM((tm, D_MODEL), BF16), pltpu.VMEM((tm, LANES), F32),
                        pltpu.VMEM((LANES, tm), F32)],
        compiler_params=pltpu.CompilerParams(dimension_semantics=("arbitrary", "arbitrary"),
                                             vmem_limit_bytes=STREAM_VMEM_LIMIT_BYTES),
        name="moe_stream_" + "".join(str(g) for g in groups),
    )(h2d, rw_pad, rb, w1, w3, w2, lng, lnb, qkvt, *caches_t)


def _dec_mix_kernel(x_ref, o0_ref, o1_ref, o2_ref, l0_ref, l1_ref, l2_ref, st_ref, glu_ref,
                    cw_ref, cb_ref, clg_ref, clb_ref, wo_ref, lng_ref, lnb_ref, h_ref, nst_ref):
    npre = CONV_WIDTH - 1
    glu = glu_ref[...]
    acc = cw_ref[npre:npre + 1, :] * glu
    for k in range(npre):
        acc = acc + cw_ref[k:k + 1, :] * st_ref[k]
    conv = _conv_tail(acc, cb_ref[...], clg_ref[...], clb_ref[...])
    w = _group_weights((l0_ref[...], l1_ref[...], l2_ref[...]))
    attn = (w[0] * o0_ref[...] + w[1] * o1_ref[...] + w[2] * o2_ref[...]).T
    h_ref[...] = _out_proj_ln(x_ref[...], attn, conv, wo_ref, lng_ref[...], lnb_ref[...])
    nst_ref[0:npre - 1] = st_ref[1:npre]
    nst_ref[npre - 1] = glu


def _decode_mix(x2d, ots, lts, state_t, glu, conv_w, conv_b, clg, clb, wo_bf, lng, lnb):
    nseq = x2d.shape[0]
    args = (x2d, *ots, *lts, state_t, glu, conv_w, conv_b, clg, clb, wo_bf, lng, lnb)
    return pl.pallas_call(
        _dec_mix_kernel,
        grid=(1,),
        in_specs=[_const_spec(a.shape) for a in args],
        out_specs=[_const_spec((nseq, D_MODEL)), _const_spec(state_t.shape)],
        out_shape=[jax.ShapeDtypeStruct((nseq, D_MODEL), F32),
                   jax.ShapeDtypeStruct(state_t.shape, F32)],
        compiler_params=_params(("arbitrary",)),
        name="decode_mix",
    )(*args)


def _dec_pool_kernel(x_ref, st_ref, pw_ref, pb_ref, ps_ref, lng_ref, lnb_ref, out_ref, nst_ref):
    x = x_ref[...]
    parts = []
    for gi, w in enumerate(POOL_WINDOWS):
        lo = gi * POOL_CH
        cur = x[:, lo:lo + POOL_CH]
        tot = cur
        for j in range(1, w):
            tot = tot + st_ref[POOL_PREFIX - j, :, lo:lo + POOL_CH]
        parts.append(tot / float(w) - cur)
    y = _pool_project(parts, pw_ref, pb_ref, ps_ref)
    out_ref[...] = _layer_norm(DN_ALPHA * x + y, lng_ref[...], lnb_ref[...])
    nst_ref[0:POOL_PREFIX - 1] = st_ref[1:POOL_PREFIX]
    nst_ref[POOL_PREFIX - 1] = x


def _decode_pool(h2d, state_t, pw_bf, pb, ps, lng, lnb):
    nseq = h2d.shape[0]
    args = (h2d, state_t, pw_bf, pb, ps, lng, lnb)
    return pl.pallas_call(
        _dec_pool_kernel,
        grid=(1,),
        in_specs=[_const_spec(a.shape) for a in args],
        out_specs=[_const_spec((nseq, D_MODEL)), _const_spec(state_t.shape)],
        out_shape=[jax.ShapeDtypeStruct((nseq, D_MODEL), F32),
                   jax.ShapeDtypeStruct(state_t.shape, F32)],
        compiler_params=_params(("arbitrary",)),
        name="decode_pool",
    )(*args)


def _token_minor(a):
    return jnp.transpose(a, (0, 2, 3, 4, 1))


def _token_major(a):
    return jnp.transpose(a, (0, 4, 1, 2, 3))


def kernel(x_prompt, x_sample, cache_attn_w128, cache_attn_w512, cache_attn_w2048, state_conv, state_pool,
           w_in, b_in, conv_w, conv_b, conv_ln_g, conv_ln_b, w_out, pool_w, pool_b, pool_scale,
           ln_mix_g, ln_mix_b, ln_ffn_g, ln_ffn_b, router_w, router_bias, moe_w1, moe_w3, moe_w2):
    n, s, d = x_prompt.shape
    nseq = x_sample.shape[0]
    past = cache_attn_w2048.shape[2]
    assert d == D_MODEL and x_sample.shape[1] == 1 and s % ROW_TILE == 0 and s == DIL_WINDOWS[2]
    assert cache_attn_w128.shape[0] == 1 and past == DIL_WINDOWS[2]
    caches = (cache_attn_w128[0], cache_attn_w512[0], cache_attn_w2048[0])

    w_in_bf = w_in[0].astype(BF16)
    w_out_bf = w_out[0].astype(BF16)
    pool_w_bf = pool_w[0].astype(BF16)
    w1_bf, w3_bf, w2_bf = moe_w1.astype(BF16), moe_w3.astype(BF16), moe_w2.astype(BF16)
    rw_pad = jnp.pad(router_w, ((0, 0), (0, LANES - N_EXPERTS))).astype(BF16)
    rb = router_bias.astype(F32).reshape(N_EXPERTS, 1)
    r2 = lambda v: v.reshape(1, -1)
    b_in2 = r2(b_in[0])
    cb, clg, clb = r2(conv_b[0]), r2(conv_ln_g[0]), r2(conv_ln_b[0])
    ps = r2(pool_scale[0])

    def moe(h2d, layer, tm):
        return _moe(h2d, rw_pad, rb, w1_bf[layer], w3_bf[layer], w2_bf[layer],
                    r2(ln_ffn_g[layer]), r2(ln_ffn_b[layer]), tm)

    xs = x_sample.reshape(nseq, d)
    ang = _rope_angles(past + jnp.arange(1, dtype=jnp.int32))
    cs = jnp.stack([jnp.broadcast_to(jnp.cos(ang).T, (ROT_HALF, nseq)),
                    jnp.broadcast_to(jnp.sin(ang).T, (ROT_HALF, nseq))])
    qkvt, glus = _decode_project(xs, w_in_bf, b_in2, cs)
    caches_t = [_token_minor(c) for c in caches]

    def moe_stream(h2d, layer, groups):
        return _moe_stream(h2d, rw_pad, rb, w1_bf[layer], w3_bf[layer], w2_bf[layer],
                           r2(ln_ffn_g[layer]), r2(ln_ffn_b[layer]), qkvt,
                           [caches_t[g] for g in groups], groups, ROW_TILE)

    xp = x_prompt.reshape(n * s, d)
    tabs_p = _rope_tables(jnp.arange(s, dtype=jnp.int32))
    (q0, q1, q2, kb0, kb1, kb2, glu, kt0, kt1, kt2) = _project(xp, w_in_bf, b_in2, tabs_p, n, s, ROW_TILE)
    os_, lses = [], []
    for g, (q, kvb) in enumerate(((q0, kb0), (q1, kb1), (q2, kb2))):
        o, l = _prompt_attention(q, kvb, g, n, s)
        os_.append(o)
        lses.append(l)
    h = _prompt_mix(xp, os_, lses, glu, conv_w[0], cb, clg, clb, w_out_bf,
                    r2(ln_mix_g[0]), r2(ln_mix_b[0]), n, s, ROW_TILE)
    h, nc2, ot2, lt2 = moe_stream(h, 0, (2,))
    p_pool = h.reshape(n, s, d)[:, s - POOL_PREFIX:][None]
    h = _prompt_pool(h, pool_w_bf, pool_b[0], ps, r2(ln_mix_g[1]), r2(ln_mix_b[1]), n, s, ROW_TILE)
    h, nc1, ot1, lt1, nc0, ot0, lt0 = moe_stream(h, 1, (1, 0))
    y_prompt = h.reshape(n, s, d)
    p_attn = [_token_major(kt.reshape(n, 2, HEADS, HEAD_DIM, kt.shape[2]))[None] for kt in (kt0, kt1, kt2)]
    p_conv = glu.reshape(n, s, CONV_CH)[:, s - (CONV_WIDTH - 1):][None]

    ots, lts = (ot0, ot1, ot2), (lt0, lt1, lt2)
    s_attn = [_token_major(c)[None] for c in (nc0, nc1, nc2)]
    conv_t = jnp.transpose(state_conv[0], (1, 0, 2))
    hs, new_conv = _decode_mix(xs, ots, lts, conv_t, glus, conv_w[0], cb, clg, clb, w_out_bf,
                               r2(ln_mix_g[0]), r2(ln_mix_b[0]))
    hs = moe(hs, 0, nseq)
    pool_t = jnp.transpose(state_pool[0], (1, 0, 2))
    hs, new_pool = _decode_pool(hs, pool_t, pool_w_bf, pool_b[0], ps,
                                r2(ln_mix_g[1]), r2(ln_mix_b[1]))
    y_sample = moe(hs, 1, nseq).reshape(nseq, 1, d)
    s_conv = jnp.transpose(new_conv, (1, 0, 2))[None]
    s_pool = jnp.transpose(new_pool, (1, 0, 2))[None]

    return (y_prompt, y_sample, p_attn[0], p_attn[1], p_attn[2], p_conv, p_pool,
            s_attn[0], s_attn[1], s_attn[2], s_conv, s_pool)
```

```python
import functools

import jax
import jax.numpy as jnp
from jax import lax
from jax.experimental import pallas as pl
from jax.experimental.pallas import tpu as pltpu

F32 = jnp.float32
BF16 = jnp.bfloat16

D_MODEL = 1024
HEAD_DIM = 64
HEADS = 8
ATTN_WIDTH = HEADS * HEAD_DIM
N_DIL = 3
DIL_WINDOWS = (128, 512, 2048)
DIL_RATES = (1, 4, 16)
ATTN_BLOCK = 128
ATTN_SCALE = HEAD_DIM ** -0.5
ROT_DIM = HEAD_DIM // 4
ROT_HALF = ROT_DIM // 2
ROPE_THETA = 500000.0
QKV_COLS = 3 * N_DIL * ATTN_WIDTH
CONV_CH = D_MODEL // 4
CONV_WIDTH = 31
IN_COLS = QKV_COLS + 2 * CONV_CH
POOL_WINDOWS = (2, 4, 8, 16)
POOL_CH = D_MODEL // len(POOL_WINDOWS)
POOL_PREFIX = max(POOL_WINDOWS) - 1
N_EXPERTS = 16
N_EXPERT_GROUPS = 4
EXPERTS_PER_GROUP = 4
D_EXPERT = 256
DEPTH = 2
DN_ALPHA = (2.0 * DEPTH) ** 0.25
LN_EPS = 1e-5

LANES = 128
SUBLANES = 8
VMEM_LIMIT_BYTES = 56 * 1024 * 1024
ROW_TILE = 512
CONV_HALO = 32
POOL_HALO = 16
STREAM_VMEM_LIMIT_BYTES = 60 * 1024 * 1024
MOE_SUBSTEPS = 8
EXPERTS_PER_SUBSTEP = N_EXPERTS // MOE_SUBSTEPS
CACHE_HEAD_BLOCK = 2
HEAD_PAIRS = ATTN_WIDTH // LANES

NT_DIMS = (((1,), (1,)), ((), ()))


def _params(sem):
    return pltpu.CompilerParams(dimension_semantics=sem, vmem_limit_bytes=VMEM_LIMIT_BYTES)


def _const_spec(shape):
    nd = len(shape)
    return pl.BlockSpec(shape, lambda *_: (0,) * nd)


def _resident_spec(shape):
    nd = len(shape)
    return pl.BlockSpec(shape, lambda *_: (0,) * nd, pipeline_mode=pl.Buffered(1))


def _layer_norm(x, g, b):
    mu = jnp.mean(x, axis=-1, keepdims=True)
    xc = x - mu
    var = jnp.mean(xc * xc, axis=-1, keepdims=True)
    return xc * lax.rsqrt(var + LN_EPS) * g + b


def _proj_kernel(x_ref, w_ref, b_ref, tab_ref, q0_ref, q1_ref, q2_ref, kv0_ref, kv1_ref, kv2_ref,
                 glu_ref, kt0_ref, kt1_ref, kt2_ref):
    tm = x_ref.shape[0]
    last = pl.program_id(1) == pl.num_programs(1) - 1
    xb = x_ref[...].astype(BF16)
    cosm, sin_lo, sin_hi = tab_ref[0], tab_ref[1], tab_ref[2]

    def proj(c0, width):
        return (jnp.dot(xb, w_ref[:, c0:c0 + width], preferred_element_type=F32)
                + b_ref[:, c0:c0 + width])

    def rope(t):
        parts = []
        for j in range(t.shape[1] // LANES):
            v = t[:, j * LANES:(j + 1) * LANES]
            parts.append(v * cosm
                         + pltpu.roll(v, LANES - ROT_HALF, 1) * sin_lo
                         + pltpu.roll(v, ROT_HALF, 1) * sin_hi)
        return jnp.concatenate(parts, axis=1)

    q_refs = (q0_ref, q1_ref, q2_ref)
    kv_refs = (kv0_ref, kv1_ref, kv2_ref)
    for g in range(N_DIL):
        c = g * ATTN_WIDTH
        q = rope(proj(c, ATTN_WIDTH)) * ATTN_SCALE
        k = rope(proj(N_DIL * ATTN_WIDTH + c, ATTN_WIDTH))
        v = proj(2 * N_DIL * ATTN_WIDTH + c, ATTN_WIDTH)
        for hp in range(HEAD_PAIRS):
            cols = slice(hp * LANES, (hp + 1) * LANES)
            q_refs[g][hp] = q[:, cols]
            kv_refs[g][hp] = k[:, cols]
            kv_refs[g][HEAD_PAIRS + hp] = v[:, cols]
        if g == 2:
            kt2_ref[0, 0:ATTN_WIDTH, :] = k.T
            kt2_ref[0, ATTN_WIDTH:2 * ATTN_WIDTH, :] = v.T
        elif g == 1:
            @pl.when(last)
            def _():
                kt1_ref[0, 0:ATTN_WIDTH, :] = k.T
                kt1_ref[0, ATTN_WIDTH:2 * ATTN_WIDTH, :] = v.T
        else:
            @pl.when(last)
            def _():
                keep = DIL_WINDOWS[0]
                kt0_ref[0, 0:ATTN_WIDTH, :] = k[tm - keep:, :].T
                kt0_ref[0, ATTN_WIDTH:2 * ATTN_WIDTH, :] = v[tm - keep:, :].T
    ga = proj(QKV_COLS, CONV_CH)
    gb = proj(QKV_COLS + CONV_CH, CONV_CH)
    glu_ref[...] = ga * jax.nn.sigmoid(gb)


def _project(x2d, w_bf, b2d, tables, n, s, tm):
    assert tm == DIL_WINDOWS[1] and tm >= DIL_WINDOWS[0]
    tps = s // tm
    m = n * s
    row = lambda b, t: (b * tps + t, 0)
    seq = lambda b, t: (b, 0, 0)
    kv_rows = 2 * ATTN_WIDTH
    slab = lambda b, t: (0, b * tps + t, 0)
    out_specs = ([pl.BlockSpec((HEAD_PAIRS, tm, LANES), slab)] * N_DIL
                 + [pl.BlockSpec((2 * HEAD_PAIRS, tm, LANES), slab)] * N_DIL
                 + [pl.BlockSpec((tm, CONV_CH), row),
                    pl.BlockSpec((1, kv_rows, DIL_WINDOWS[0]), seq),
                    pl.BlockSpec((1, kv_rows, DIL_WINDOWS[1]), seq),
                    pl.BlockSpec((1, kv_rows, tm), lambda b, t: (b, 0, t))])
    out_shape = ([jax.ShapeDtypeStruct((HEAD_PAIRS, m, LANES), F32)] * N_DIL
                 + [jax.ShapeDtypeStruct((2 * HEAD_PAIRS, m, LANES), F32)] * N_DIL
                 + [jax.ShapeDtypeStruct((m, CONV_CH), F32),
                    jax.ShapeDtypeStruct((n, kv_rows, DIL_WINDOWS[0]), F32),
                    jax.ShapeDtypeStruct((n, kv_rows, DIL_WINDOWS[1]), F32),
                    jax.ShapeDtypeStruct((n, kv_rows, s), F32)])
    return pl.pallas_call(
        _proj_kernel,
        grid=(n, tps),
        in_specs=[
            pl.BlockSpec((tm, D_MODEL), row),
            _resident_spec((D_MODEL, IN_COLS)),
            _const_spec((1, IN_COLS)),
            pl.BlockSpec((3, tm, LANES), lambda b, t: (0, t, 0)),
        ],
        out_specs=out_specs,
        out_shape=out_shape,
        compiler_params=_params(("parallel", "arbitrary")),
        name="in_proj",
    )(x2d, w_bf, b2d, tables)


def _rope_angles(pos):
    inv_freq = ROPE_THETA ** (-jnp.arange(ROT_HALF, dtype=F32) * 2.0 / ROT_DIM)
    return pos.astype(F32)[:, None] * inv_freq[None, :]


def _rope_tables(pos):
    t = pos.shape[0]
    ang = _rope_angles(pos)
    cos, sin = jnp.cos(ang), jnp.sin(ang)
    rest = HEAD_DIM - ROT_DIM
    c64 = jnp.concatenate([cos, cos, jnp.ones((t, rest), F32)], axis=1)
    lo64 = jnp.concatenate([-sin, jnp.zeros((t, HEAD_DIM - ROT_HALF), F32)], axis=1)
    hi64 = jnp.concatenate([jnp.zeros((t, ROT_HALF), F32), sin, jnp.zeros((t, rest), F32)], axis=1)
    rep = LANES // HEAD_DIM
    return jnp.stack([jnp.tile(c64, (1, rep)), jnp.tile(lo64, (1, rep)), jnp.tile(hi64, (1, rep))])


def _attn_kernel(q_ref, kv_ref, bias_ref, o_ref, lse_ref, s_ref, p_ref, *, rate, nblk):
    nk = s_ref.shape[2]
    lane = lax.broadcasted_iota(jnp.int32, (ATTN_BLOCK, LANES), 1)
    low_half = lane < HEAD_DIM
    keep_lo = low_half.astype(BF16)
    keep_hi = 1 - keep_lo
    ones_rhs = jnp.ones((nk, LANES), BF16)

    def strided(start, size):
        return pl.ds(start, size, stride=rate) if rate > 1 else pl.ds(start, size)

    def unit(u, carry):
        if nblk > 1:
            res, b = u // nblk, u % nblk
            bias = bias_ref[jnp.minimum(b, 1)]
        else:
            res, b = u, 0
            bias = bias_ref[0]
        rows_q = strided(b * (ATTN_BLOCK * rate) + res, ATTN_BLOCK)
        rows_k = strided(jnp.maximum(b - 1, 0) * (ATTN_BLOCK * rate) + res, nk)
        for hp in range(HEAD_PAIRS):
            qp = q_ref[hp, rows_q, :].astype(BF16)
            kp = kv_ref[hp, rows_k, :].astype(BF16)
            for half, keep in enumerate((keep_lo, keep_hi)):
                s = lax.dot_general(qp * keep, kp, NT_DIMS, preferred_element_type=F32)
                s_ref[2 * hp + half] = s + bias
        sc = s_ref[...]
        m = jnp.max(sc, axis=-1, keepdims=True)
        p_ref[...] = jnp.exp(sc - m).astype(BF16)
        for hp in range(HEAD_PAIRS):
            vp = kv_ref[HEAD_PAIRS + hp, rows_k, :].astype(BF16)
            rhs = jnp.concatenate([vp, ones_rhs], axis=1)
            ol_lo = jnp.dot(p_ref[2 * hp], rhs, preferred_element_type=F32)
            ol_hi = jnp.dot(p_ref[2 * hp + 1], rhs, preferred_element_type=F32)
            l_lo, l_hi = ol_lo[:, LANES:], ol_hi[:, LANES:]
            o_pair = jnp.where(low_half, ol_lo[:, :LANES] * (1.0 / l_lo), ol_hi[:, :LANES] * (1.0 / l_hi))
            lse_pair = jnp.where(low_half, m[2 * hp] + jnp.log(l_lo), m[2 * hp + 1] + jnp.log(l_hi))
            o_ref[hp, rows_q, :] = o_pair
            lse_ref[hp, rows_q, :] = lse_pair
        return carry

    lax.fori_loop(0, rate * nblk, unit, 0)


def _band_bias(nblk):
    qi = jnp.arange(ATTN_BLOCK)[:, None]
    ci = jnp.arange(ATTN_BLOCK)[None, :]
    causal = jnp.where(ci <= qi, 0.0, -jnp.inf).astype(F32)
    if nblk == 1:
        return jnp.stack([causal, causal])
    band = jnp.where(ci >= qi, 0.0, -jnp.inf).astype(F32)
    closed = jnp.full((ATTN_BLOCK, ATTN_BLOCK), -jnp.inf, F32)
    return jnp.stack([jnp.concatenate([causal, closed], axis=1), jnp.concatenate([band, causal], axis=1)])


def _prompt_attention(q, kv, g, n, s):
    rate = DIL_RATES[g]
    nblk = s // rate // ATTN_BLOCK
    nk = 2 * ATTN_BLOCK if nblk > 1 else ATTN_BLOCK
    out_sds = jax.ShapeDtypeStruct((HEAD_PAIRS, n * s, LANES), F32)
    seq = lambda b: (0, b, 0)
    o_spec = pl.BlockSpec((HEAD_PAIRS, s, LANES), seq)
    return pl.pallas_call(
        functools.partial(_attn_kernel, rate=rate, nblk=nblk),
        grid=(n,),
        in_specs=[
            o_spec,
            pl.BlockSpec((2 * HEAD_PAIRS, s, LANES), seq),
            _const_spec((2, ATTN_BLOCK, nk)),
        ],
        out_specs=[o_spec, o_spec],
        out_shape=[out_sds, out_sds],
        scratch_shapes=[pltpu.VMEM((HEADS, ATTN_BLOCK, nk), F32),
                        pltpu.VMEM((HEADS, ATTN_BLOCK, nk), BF16)],
        compiler_params=_params(("parallel",)),
        name=f"band_attn_g{g}",
    )(q, kv, _band_bias(nblk))


def _group_weights(lses):
    lmax = jnp.maximum(jnp.maximum(lses[0], lses[1]), lses[2])
    es = [jnp.exp(l - lmax) for l in lses]
    inv = 1.0 / (es[0] + es[1] + es[2])
    return [e * inv for e in es]


def _conv_tail(y, cb, clg, clb):
    z = _layer_norm(y + cb, clg, clb)
    return z * jax.nn.sigmoid(z)


def _out_proj_ln(x, attn, conv, wo_ref, lng, lnb):
    y = (jnp.dot(attn.astype(BF16), wo_ref[0:ATTN_WIDTH, :], preferred_element_type=F32)
         + jnp.dot(conv.astype(BF16), wo_ref[ATTN_WIDTH:ATTN_WIDTH + CONV_CH, :],
                   preferred_element_type=F32))
    return _layer_norm(DN_ALPHA * x + y, lng, lnb)


def _mix_kernel(x_ref, o0_ref, o1_ref, o2_ref, l0_ref, l1_ref, l2_ref, gc_ref, gp_ref,
                cw_ref, cb_ref, clg_ref, clb_ref, wo_ref, lng_ref, lnb_ref, h_ref, ext_ref):
    tm = x_ref.shape[0]
    first = pl.program_id(1) == 0
    ext_ref[0:CONV_HALO, :] = jnp.where(first, 0.0, gp_ref[...])
    ext_ref[CONV_HALO:CONV_HALO + tm, :] = gc_ref[...]
    base = CONV_HALO - (CONV_WIDTH - 1)
    acc = cw_ref[0:1, :] * ext_ref[base:base + tm, :]
    for k in range(1, CONV_WIDTH):
        acc = acc + cw_ref[k:k + 1, :] * ext_ref[base + k:base + k + tm, :]
    conv = _conv_tail(acc, cb_ref[...], clg_ref[...], clb_ref[...])
    pieces = []
    for hp in range(HEAD_PAIRS):
        w = _group_weights((l0_ref[hp], l1_ref[hp], l2_ref[hp]))
        pieces.append(w[0] * o0_ref[hp] + w[1] * o1_ref[hp] + w[2] * o2_ref[hp])
    attn = jnp.concatenate(pieces, axis=1)
    h_ref[...] = _out_proj_ln(x_ref[...], attn, conv, wo_ref, lng_ref[...], lnb_ref[...])


def _prompt_mix(x2d, os_, lses, glu, conv_w, conv_b, clg, clb, wo_bf, lng, lnb, n, s, tm):
    tps = s // tm
    hb = tm // CONV_HALO
    row = lambda b, t: (b * tps + t, 0)
    aw = pl.BlockSpec((HEAD_PAIRS, tm, LANES), lambda b, t: (0, b * tps + t, 0))
    return pl.pallas_call(
        _mix_kernel,
        grid=(n, tps),
        in_specs=[
            pl.BlockSpec((tm, D_MODEL), row),
            aw, aw, aw, aw, aw, aw,
            pl.BlockSpec((tm, CONV_CH), row),
            pl.BlockSpec((CONV_HALO, CONV_CH),
                         lambda b, t: (jnp.maximum((b * tps + t) * hb - 1, 0), 0)),
            _const_spec((CONV_WIDTH, CONV_CH)),
            _const_spec((1, CONV_CH)), _const_spec((1, CONV_CH)), _const_spec((1, CONV_CH)),
            _const_spec((ATTN_WIDTH + CONV_CH, D_MODEL)),
            _const_spec((1, D_MODEL)), _const_spec((1, D_MODEL)),
        ],
        out_specs=pl.BlockSpec((tm, D_MODEL), row),
        out_shape=jax.ShapeDtypeStruct((n * s, D_MODEL), F32),
        scratch_shapes=[pltpu.VMEM((CONV_HALO + tm, CONV_CH), F32)],
        compiler_params=_params(("parallel", "parallel")),
        name="mix_out",
    )(x2d, *os_, *lses, glu, glu, conv_w, conv_b, clg, clb, wo_bf, lng, lnb)


def _routing_rows(logit_rows, bias_ref):
    m = logit_rows[0]
    for r in logit_rows[1:]:
        m = jnp.maximum(m, r)
    ex = [jnp.exp(r - m) for r in logit_rows]
    tot = ex[0]
    for e in ex[1:]:
        tot = tot + e
    scores = [e / tot for e in ex]
    sel = [scores[e] + bias_ref[e:e + 1, :] for e in range(N_EXPERTS)]
    grp = []
    for g in range(N_EXPERT_GROUPS):
        v = sel[g * EXPERTS_PER_GROUP:(g + 1) * EXPERTS_PER_GROUP]
        best = v[0] + v[1]
        for i in range(EXPERTS_PER_GROUP):
            for j in range(i + 1, EXPERTS_PER_GROUP):
                if (i, j) != (0, 1):
                    best = jnp.maximum(best, v[i] + v[j])
        grp.append(best)
    gmax = grp[0]
    for v in grp[1:]:
        gmax = jnp.maximum(gmax, v)
    taken = None
    in_group = []
    for g in range(N_EXPERT_GROUPS):
        hit = grp[g] == gmax
        if taken is None:
            in_group.append(hit)
            taken = hit
        else:
            in_group.append(jnp.logical_and(hit, jnp.logical_not(taken)))
            taken = jnp.logical_or(taken, hit)
    gates = []
    for e in range(N_EXPERTS):
        g = e // EXPERTS_PER_GROUP
        rank = jnp.zeros_like(sel[e])
        for o in range(g * EXPERTS_PER_GROUP, (g + 1) * EXPERTS_PER_GROUP):
            if o == e:
                continue
            ahead = sel[o] > sel[e]
            if o < e:
                ahead = jnp.logical_or(ahead, sel[o] == sel[e])
            rank = rank + ahead.astype(F32)
        chosen = jnp.logical_and(in_group[g], rank < float(2))
        gates.append(jnp.where(chosen, scores[e], 0.0))
    den = gates[0]
    for v in gates[1:]:
        den = den + v
    return [v / den for v in gates]


def _moe_kernel(h_ref, rw_ref, rb_ref, w1_ref, w3_ref, w2_ref, lng_ref, lnb_ref, out_ref, ct_ref):
    h = h_ref[...]
    hb = h.astype(BF16)
    logits = jnp.dot(hb, rw_ref[...], preferred_element_type=F32)
    lt = logits.T
    rows = [lt[e:e + 1, :] for e in range(N_EXPERTS)]
    comb_rows = _routing_rows(rows, rb_ref)
    ct_ref[...] = jnp.zeros_like(ct_ref)
    for e in range(N_EXPERTS):
        ct_ref[e:e + 1, :] = comb_rows[e]
    comb = ct_ref[...].T
    acc = jnp.zeros(h.shape, F32)
    for e in range(N_EXPERTS):
        a = jnp.dot(hb, w1_ref[e], preferred_element_type=F32)
        b = jnp.dot(hb, w3_ref[e], preferred_element_type=F32)
        gated = (a * jax.nn.sigmoid(a)) * b * comb[:, e:e + 1]
        acc = acc + jnp.dot(gated.astype(BF16), w2_ref[e], preferred_element_type=F32)
    out_ref[...] = _layer_norm(DN_ALPHA * h + acc, lng_ref[...], lnb_ref[...])


def _moe(h2d, rw_pad, rb, w1, w3, w2, lng, lnb, tm):
    m = h2d.shape[0]
    row = lambda i: (i, 0)
    return pl.pallas_call(
        _moe_kernel,
        grid=(m // tm,),
        in_specs=[
            pl.BlockSpec((tm, D_MODEL), row),
            _const_spec((D_MODEL, LANES)),
            _const_spec((N_EXPERTS, 1)),
            _resident_spec((N_EXPERTS, D_MODEL, D_EXPERT)),
            _resident_spec((N_EXPERTS, D_MODEL, D_EXPERT)),
            _resident_spec((N_EXPERTS, D_EXPERT, D_MODEL)),
            _const_spec((1, D_MODEL)), _const_spec((1, D_MODEL)),
        ],
        out_specs=pl.BlockSpec((tm, D_MODEL), row),
        out_shape=jax.ShapeDtypeStruct((m, D_MODEL), F32),
        scratch_shapes=[pltpu.VMEM((LANES, tm), F32)],
        compiler_params=_params(("parallel",)),
        name="moe_ffn",
    )(h2d, rw_pad, rb, w1, w3, w2, lng, lnb)


def _pool_project(parts, pw_ref, pb_ref, ps_ref):
    cols = []
    for gi in range(len(POOL_WINDOWS)):
        lo = gi * POOL_CH
        y = jnp.dot(parts[gi].astype(BF16), pw_ref[gi], preferred_element_type=F32)
        cols.append((y + pb_ref[gi:gi + 1, :]) * ps_ref[:, lo:lo + POOL_CH])
    return jnp.concatenate(cols, axis=1)


def _pool_kernel(x_ref, xp_ref, pw_ref, pb_ref, ps_ref, lng_ref, lnb_ref, out_ref, ext_ref):
    tm = x_ref.shape[0]
    t = pl.program_id(1)
    ext_ref[0:POOL_HALO, :] = jnp.where(t == 0, 0.0, xp_ref[...])
    ext_ref[POOL_HALO:POOL_HALO + tm, :] = x_ref[...]
    pos = t * tm + lax.broadcasted_iota(jnp.int32, (tm, 1), 0)
    parts = []
    for gi, w in enumerate(POOL_WINDOWS):
        lo = gi * POOL_CH
        cur = x_ref[:, lo:lo + POOL_CH]
        tot = cur
        for j in range(1, w):
            tot = tot + ext_ref[POOL_HALO - j:POOL_HALO - j + tm, lo:lo + POOL_CH]
        cnt = jnp.minimum(w, pos + 1).astype(F32)
        parts.append(tot / cnt - cur)
    y = _pool_project(parts, pw_ref, pb_ref, ps_ref)
    out_ref[...] = _layer_norm(DN_ALPHA * x_ref[...] + y, lng_ref[...], lnb_ref[...])


def _prompt_pool(h2d, pw_bf, pb, ps, lng, lnb, n, s, tm):
    tps = s // tm
    hb = tm // POOL_HALO
    row = lambda b, t: (b * tps + t, 0)
    ng = len(POOL_WINDOWS)
    return pl.pallas_call(
        _pool_kernel,
        grid=(n, tps),
        in_specs=[
            pl.BlockSpec((tm, D_MODEL), row),
            pl.BlockSpec((POOL_HALO, D_MODEL),
                         lambda b, t: (jnp.maximum((b * tps + t) * hb - 1, 0), 0)),
            _const_spec((ng, POOL_CH, POOL_CH)),
            _const_spec((ng, POOL_CH)),
            _const_spec((1, D_MODEL)), _const_spec((1, D_MODEL)), _const_spec((1, D_MODEL)),
        ],
        out_specs=pl.BlockSpec((tm, D_MODEL), row),
        out_shape=jax.ShapeDtypeStruct((n * s, D_MODEL), F32),
        scratch_shapes=[pltpu.VMEM((POOL_HALO + tm, D_MODEL), F32)],
        compiler_params=_params(("parallel", "parallel")),
        name="pool_mix",
    )(h2d, h2d, pw_bf, pb, ps, lng, lnb)


def _dec_proj_kernel(x_ref, w_ref, b_ref, cs_ref, qkvt_ref, glu_ref):
    xb = x_ref[...].astype(BF16)
    cos, sin = cs_ref[0], cs_ref[1]
    chunk = ATTN_WIDTH

    def proj(c0, width):
        return (jnp.dot(xb, w_ref[:, c0:c0 + width], preferred_element_type=F32)
                + b_ref[:, c0:c0 + width])

    for ci in range(QKV_COLS // chunk):
        pt = proj(ci * chunk, chunk).T
        if ci < 2 * N_DIL:
            pieces = []
            for hh in range(HEADS):
                base = hh * HEAD_DIM
                x1 = pt[base:base + ROT_HALF, :]
                x2 = pt[base + ROT_HALF:base + ROT_DIM, :]
                pieces += [x1 * cos - x2 * sin, x2 * cos + x1 * sin, pt[base + ROT_DIM:base + HEAD_DIM, :]]
            pt = jnp.concatenate(pieces, axis=0)
            if ci < N_DIL:
                pt = pt * ATTN_SCALE
        qkvt_ref[ci * chunk:(ci + 1) * chunk, :] = pt
    ga = proj(QKV_COLS, CONV_CH)
    gb = proj(QKV_COLS + CONV_CH, CONV_CH)
    glu_ref[...] = ga * jax.nn.sigmoid(gb)


def _decode_project(x2d, w_bf, b2d, cs):
    nseq = x2d.shape[0]
    args = (x2d, w_bf, b2d, cs)
    return pl.pallas_call(
        _dec_proj_kernel,
        grid=(1,),
        in_specs=[_const_spec(a.shape) for a in args],
        out_specs=[_const_spec((QKV_COLS, nseq)), _const_spec((nseq, CONV_CH))],
        out_shape=[jax.ShapeDtypeStruct((QKV_COLS, nseq), F32),
                   jax.ShapeDtypeStruct((nseq, CONV_CH), F32)],
        compiler_params=_params(("arbitrary",)),
        name="decode_proj",
    )(*args)


def _cache_block(c_ref, nc_ref, ot_ref, lt_ref, qkvt_ref, g, seq, head0):
    hb, w = c_ref.shape[2], c_ref.shape[4]
    nseq = qkvt_ref.shape[1]
    rate = DIL_RATES[g]
    rows = hb * HEAD_DIM
    r0 = pl.multiple_of(head0 * HEAD_DIM, rows)
    mine = lax.broadcasted_iota(jnp.int32, (rows, nseq), 1) == seq

    def column(base):
        x = qkvt_ref[pl.ds(base + r0, rows), :]
        return jnp.sum(jnp.where(mine, x, 0.0), axis=1, keepdims=True).reshape(hb, HEAD_DIM, 1)

    qc = column(g * ATTN_WIDTH)
    kc = column((N_DIL + g) * ATTN_WIDTH)
    vc = column((2 * N_DIL + g) * ATTN_WIDTH)
    kt = c_ref[0, 0]
    vt = c_ref[0, 1]
    tok = lax.broadcasted_iota(jnp.int32, (1, 1, w), 2)
    in_window = (tok & (rate - 1)) == 0
    newest = tok == w - 1

    sc = jnp.where(in_window, jnp.sum(kt * qc, axis=1, keepdims=True), -jnp.inf)
    sn = jnp.sum(qc * kc, axis=1, keepdims=True)
    m = jnp.maximum(jnp.max(sc, axis=2, keepdims=True), sn)
    p = jnp.exp(sc - m)
    pn = jnp.exp(sn - m)
    l = jnp.sum(p, axis=2, keepdims=True) + pn
    inv = 1.0 / l
    o = jnp.sum(vt * (p * inv), axis=2, keepdims=True) + vc * (pn * inv)
    lse = jnp.broadcast_to(m + jnp.log(l), (hb, HEAD_DIM, 1))
    acc_rows = pl.ds(r0, rows)
    ot_ref[acc_rows, :] = jnp.where(mine, o.reshape(rows, 1), ot_ref[acc_rows, :])
    lt_ref[acc_rows, :] = jnp.where(mine, lse.reshape(rows, 1), lt_ref[acc_rows, :])

    def shifted(old, new_col):
        rolled = pltpu.roll(old.reshape(rows, w), w - 1, 1).reshape(hb, HEAD_DIM, w)
        return jnp.where(newest, new_col, rolled)

    nc_ref[0, 0] = shifted(kt, kc)
    nc_ref[0, 1] = shifted(vt, vc)


def _moe_stream_kernel(*refs, groups, seq_base):
    ng = len(groups)
    (h_ref, rw_ref, rb_ref, w1_ref, w3_ref, w2_ref, lng_ref, lnb_ref, qkvt_ref) = refs[:9]
    c_refs = refs[9:9 + ng]
    n_in = 9 + ng + (3 * ng if seq_base else 0)
    prev_acc = refs[9 + 2 * ng:n_in]
    out_ref = refs[n_in]
    stream_out = refs[n_in + 1:n_in + 1 + 3 * ng]
    hb_ref, comb_ref, ct_ref = refs[n_in + 1 + 3 * ng:]
    i, k = pl.program_id(0), pl.program_id(1)
    flat = i * MOE_SUBSTEPS + k

    @pl.when(k == 0)
    def _():
        hb = h_ref[...].astype(BF16)
        hb_ref[...] = hb
        logits = jnp.dot(hb, rw_ref[...], preferred_element_type=F32)
        lt = logits.T
        comb_rows = _routing_rows([lt[e:e + 1, :] for e in range(N_EXPERTS)], rb_ref)
        ct_ref[...] = jnp.zeros_like(ct_ref)
        for e in range(N_EXPERTS):
            ct_ref[e:e + 1, :] = comb_rows[e]
        comb_ref[...] = ct_ref[...].T
        out_ref[...] = jnp.zeros_like(out_ref)

    @pl.when(flat == 0)
    def _():
        for s in range(ng):
            for t in range(2):
                acc = stream_out[3 * s + 1 + t]
                acc[...] = prev_acc[2 * s + t][...] if seq_base else jnp.zeros_like(acc)

    hb = hb_ref[...]
    comb = comb_ref[...]
    lane = lax.broadcasted_iota(jnp.int32, comb.shape, 1)
    for ee in range(EXPERTS_PER_SUBSTEP):
        e = k * EXPERTS_PER_SUBSTEP + ee
        a = jnp.dot(hb, w1_ref[e], preferred_element_type=F32)
        b = jnp.dot(hb, w3_ref[e], preferred_element_type=F32)
        gate = jnp.sum(jnp.where(lane == e, comb, 0.0), axis=1, keepdims=True)
        gated = (a * jax.nn.sigmoid(a)) * b * gate
        out_ref[...] += jnp.dot(gated.astype(BF16), w2_ref[e], preferred_element_type=F32)

    blocks_per_seq = HEADS // CACHE_HEAD_BLOCK
    for s, g in enumerate(groups):
        _cache_block(c_refs[s], stream_out[3 * s], stream_out[3 * s + 1], stream_out[3 * s + 2], qkvt_ref,
                     g, seq_base + flat // blocks_per_seq, (flat % blocks_per_seq) * CACHE_HEAD_BLOCK)

    @pl.when(k == MOE_SUBSTEPS - 1)
    def _():
        out_ref[...] = _layer_norm(DN_ALPHA * h_ref[...] + out_ref[...], lng_ref[...], lnb_ref[...])


def _moe_stream(h2d, rw_pad, rb, w1, w3, w2, lng, lnb, qkvt, caches_t, tm, seq_base, prev):
    m = h2d.shape[0]
    nseq = qkvt.shape[1]
    groups = tuple(range(N_DIL))
    blocks_per_seq = HEADS // CACHE_HEAD_BLOCK
    first_block = seq_base * blocks_per_seq
    assert ((m // tm) * MOE_SUBSTEPS) % blocks_per_seq == 0
    assert first_block + (m // tm) * MOE_SUBSTEPS <= nseq * blocks_per_seq
    row = lambda i, k: (i, 0)
    acc_spec = _const_spec((ATTN_WIDTH, nseq))
    acc_sds = jax.ShapeDtypeStruct((ATTN_WIDTH, nseq), F32)

    def block_index(i, k):
        b = first_block + i * MOE_SUBSTEPS + k
        return (b // blocks_per_seq, 0, b % blocks_per_seq, 0, 0)

    cache_specs, stream_specs, stream_shapes = [], [], []
    for c in caches_t:
        spec = pl.BlockSpec((1, 2, CACHE_HEAD_BLOCK, HEAD_DIM, c.shape[4]), block_index)
        cache_specs.append(spec)
        stream_specs += [spec, acc_spec, acc_spec]
        stream_shapes += [jax.ShapeDtypeStruct(c.shape, F32), acc_sds, acc_sds]
    extra_in, extra_specs, aliases = [], [], {}
    if seq_base:
        n_fixed = 9 + len(caches_t)
        for s, (nc, _, _) in enumerate(prev):
            aliases[n_fixed + s] = 1 + 3 * s
            extra_in.append(nc)
            extra_specs.append(pl.BlockSpec(memory_space=pl.ANY))
        for _, ot, lt in prev:
            extra_in += [ot, lt]
            extra_specs += [acc_spec, acc_spec]
    return pl.pallas_call(
        functools.partial(_moe_stream_kernel, groups=groups, seq_base=seq_base),
        grid=(m // tm, MOE_SUBSTEPS),
        in_specs=[
            pl.BlockSpec((tm, D_MODEL), row),
            _const_spec((D_MODEL, LANES)),
            _const_spec((N_EXPERTS, 1)),
            _resident_spec((N_EXPERTS, D_MODEL, D_EXPERT)),
            _resident_spec((N_EXPERTS, D_MODEL, D_EXPERT)),
            _resident_spec((N_EXPERTS, D_EXPERT, D_MODEL)),
            _const_spec((1, D_MODEL)), _const_spec((1, D_MODEL)),
            _resident_spec(qkvt.shape),
        ] + cache_specs + extra_specs,
        out_specs=[pl.BlockSpec((tm, D_MODEL), row)] + stream_specs,
        out_shape=[jax.ShapeDtypeStruct((m, D_MODEL), F32)] + stream_shapes,
        input_output_aliases=aliases,
        scratch_shapes=[pltpu.VMEM((tm, D_MODEL), BF16), pltpu.VMEM((tm, LANES), F32),
                        pltpu.VMEM((LANES, tm), F32)],
        compiler_params=pltpu.CompilerParams(dimension_semantics=("arbitrary", "arbitrary"),
                                             vmem_limit_bytes=STREAM_VMEM_LIMIT_BYTES),
        name=f"moe_stream_from{seq_base}",
    )(h2d, rw_pad, rb, w1, w3, w2, lng, lnb, qkvt, *caches_t, *extra_in)


def _dec_mix_kernel(x_ref, o0_ref, o1_ref, o2_ref, l0_ref, l1_ref, l2_ref, st_ref, glu_ref,
                    cw_ref, cb_ref, clg_ref, clb_ref, wo_ref, lng_ref, lnb_ref, h_ref, nst_ref):
    npre = CONV_WIDTH - 1
    glu = glu_ref[...]
    acc = cw_ref[npre:npre + 1, :] * glu
    for k in range(npre):
        acc = acc + cw_ref[k:k + 1, :] * st_ref[k]
    conv = _conv_tail(acc, cb_ref[...], clg_ref[...], clb_ref[...])
    w = _group_weights((l0_ref[...], l1_ref[...], l2_ref[...]))
    attn = (w[0] * o0_ref[...] + w[1] * o1_ref[...] + w[2] * o2_ref[...]).T
    h_ref[...] = _out_proj_ln(x_ref[...], attn, conv, wo_ref, lng_ref[...], lnb_ref[...])
    nst_ref[0:npre - 1] = st_ref[1:npre]
    nst_ref[npre - 1] = glu


def _decode_mix(x2d, ots, lts, state_t, glu, conv_w, conv_b, clg, clb, wo_bf, lng, lnb):
    nseq = x2d.shape[0]
    args = (x2d, *ots, *lts, state_t, glu, conv_w, conv_b, clg, clb, wo_bf, lng, lnb)
    return pl.pallas_call(
        _dec_mix_kernel,
        grid=(1,),
        in_specs=[_const_spec(a.shape) for a in args],
        out_specs=[_const_spec((nseq, D_MODEL)), _const_spec(state_t.shape)],
        out_shape=[jax.ShapeDtypeStruct((nseq, D_MODEL), F32),
                   jax.ShapeDtypeStruct(state_t.shape, F32)],
        compiler_params=_params(("arbitrary",)),
        name="decode_mix",
    )(*args)


def _dec_pool_kernel(x_ref, st_ref, pw_ref, pb_ref, ps_ref, lng_ref, lnb_ref, out_ref, nst_ref):
    x = x_ref[...]
    parts = []
    for gi, w in enumerate(POOL_WINDOWS):
        lo = gi * POOL_CH
        cur = x[:, lo:lo + POOL_CH]
        tot = cur
        for j in range(1, w):
            tot = tot + st_ref[POOL_PREFIX - j, :, lo:lo + POOL_CH]
        parts.append(tot / float(w) - cur)
    y = _pool_project(parts, pw_ref, pb_ref, ps_ref)
    out_ref[...] = _layer_norm(DN_ALPHA * x + y, lng_ref[...], lnb_ref[...])
    nst_ref[0:POOL_PREFIX - 1] = st_ref[1:POOL_PREFIX]
    nst_ref[POOL_PREFIX - 1] = x


def _decode_pool(h2d, state_t, pw_bf, pb, ps, lng, lnb):
    nseq = h2d.shape[0]
    args = (h2d, state_t, pw_bf, pb, ps, lng, lnb)
    return pl.pallas_call(
        _dec_pool_kernel,
        grid=(1,),
        in_specs=[_const_spec(a.shape) for a in args],
        out_specs=[_const_spec((nseq, D_MODEL)), _const_spec(state_t.shape)],
        out_shape=[jax.ShapeDtypeStruct((nseq, D_MODEL), F32),
                   jax.ShapeDtypeStruct(state_t.shape, F32)],
        compiler_params=_params(("arbitrary",)),
        name="decode_pool",
    )(*args)


def _token_minor(a):
    return jnp.transpose(a, (0, 2, 3, 4, 1))


def _token_major(a):
    return jnp.transpose(a, (0, 4, 1, 2, 3))


def kernel(x_prompt, x_sample, cache_attn_w128, cache_attn_w512, cache_attn_w2048, state_conv, state_pool,
           w_in, b_in, conv_w, conv_b, conv_ln_g, conv_ln_b, w_out, pool_w, pool_b, pool_scale,
           ln_mix_g, ln_mix_b, ln_ffn_g, ln_ffn_b, router_w, router_bias, moe_w1, moe_w3, moe_w2):
    n, s, d = x_prompt.shape
    nseq = x_sample.shape[0]
    past = cache_attn_w2048.shape[2]
    assert d == D_MODEL and x_sample.shape[1] == 1 and s % ROW_TILE == 0 and s == DIL_WINDOWS[2]
    assert cache_attn_w128.shape[0] == 1 and past == DIL_WINDOWS[2]
    caches = (cache_attn_w128[0], cache_attn_w512[0], cache_attn_w2048[0])

    w_in_bf = w_in[0].astype(BF16)
    w_out_bf = w_out[0].astype(BF16)
    pool_w_bf = pool_w[0].astype(BF16)
    w1_bf, w3_bf, w2_bf = moe_w1.astype(BF16), moe_w3.astype(BF16), moe_w2.astype(BF16)
    rw_pad = jnp.pad(router_w, ((0, 0), (0, LANES - N_EXPERTS))).astype(BF16)
    rb = router_bias.astype(F32).reshape(N_EXPERTS, 1)
    r2 = lambda v: v.reshape(1, -1)
    b_in2 = r2(b_in[0])
    cb, clg, clb = r2(conv_b[0]), r2(conv_ln_g[0]), r2(conv_ln_b[0])
    ps = r2(pool_scale[0])

    def moe(h2d, layer, tm):
        return _moe(h2d, rw_pad, rb, w1_bf[layer], w3_bf[layer], w2_bf[layer],
                    r2(ln_ffn_g[layer]), r2(ln_ffn_b[layer]), tm)

    xs = x_sample.reshape(nseq, d)
    ang = _rope_angles(past + jnp.arange(1, dtype=jnp.int32))
    cs = jnp.stack([jnp.broadcast_to(jnp.cos(ang).T, (ROT_HALF, nseq)),
                    jnp.broadcast_to(jnp.sin(ang).T, (ROT_HALF, nseq))])
    qkvt, glus = _decode_project(xs, w_in_bf, b_in2, cs)
    caches_t = [_token_minor(c) for c in caches]

    seqs_per_layer = (n * s // ROW_TILE) * MOE_SUBSTEPS * CACHE_HEAD_BLOCK // HEADS
    assert DEPTH * seqs_per_layer == nseq

    def moe_stream(h2d, layer, prev):
        res = _moe_stream(h2d, rw_pad, rb, w1_bf[layer], w3_bf[layer], w2_bf[layer],
                          r2(ln_ffn_g[layer]), r2(ln_ffn_b[layer]), qkvt, caches_t, ROW_TILE,
                          layer * seqs_per_layer, prev)
        return res[0], [tuple(res[1 + 3 * g:4 + 3 * g]) for g in range(N_DIL)]

    xp = x_prompt.reshape(n * s, d)
    tabs_p = _rope_tables(jnp.arange(s, dtype=jnp.int32))
    (q0, q1, q2, kv0, kv1, kv2, glu, kt0, kt1, kt2) = _project(xp, w_in_bf, b_in2, tabs_p, n, s, ROW_TILE)
    os_, lses = [], []
    for g, (q, kv) in enumerate(((q0, kv0), (q1, kv1), (q2, kv2))):
        o, l = _prompt_attention(q, kv, g, n, s)
        os_.append(o)
        lses.append(l)
    h = _prompt_mix(xp, os_, lses, glu, conv_w[0], cb, clg, clb, w_out_bf,
                    r2(ln_mix_g[0]), r2(ln_mix_b[0]), n, s, ROW_TILE)
    h, streamed = moe_stream(h, 0, None)
    p_pool = h.reshape(n, s, d)[:, s - POOL_PREFIX:][None]
    h = _prompt_pool(h, pool_w_bf, pool_b[0], ps, r2(ln_mix_g[1]), r2(ln_mix_b[1]), n, s, ROW_TILE)
    h, streamed = moe_stream(h, 1, streamed)
    y_prompt = h.reshape(n, s, d)
    p_attn = [_token_major(kt.reshape(n, 2, HEADS, HEAD_DIM, kt.shape[2]))[None] for kt in (kt0, kt1, kt2)]
    p_conv = glu.reshape(n, s, CONV_CH)[:, s - (CONV_WIDTH - 1):][None]

    ots = [st[1] for st in streamed]
    lts = [st[2] for st in streamed]
    s_attn = [_token_major(st[0])[None] for st in streamed]
    conv_t = jnp.transpose(state_conv[0], (1, 0, 2))
    hs, new_conv = _decode_mix(xs, ots, lts, conv_t, glus, conv_w[0], cb, clg, clb, w_out_bf,
                               r2(ln_mix_g[0]), r2(ln_mix_b[0]))
    hs = moe(hs, 0, nseq)
    pool_t = jnp.transpose(state_pool[0], (1, 0, 2))
    hs, new_pool = _decode_pool(hs, pool_t, pool_w_bf, pool_b[0], ps,
                                r2(ln_mix_g[1]), r2(ln_mix_b[1]))
    y_sample = moe(hs, 1, nseq).reshape(nseq, 1, d)
    s_conv = jnp.transpose(new_conv, (1, 0, 2))[None]
    s_pool = jnp.transpose(new_pool, (1, 0, 2))[None]

    return (y_prompt, y_sample, p_attn[0], p_attn[1], p_attn[2], p_conv, p_pool,
            s_attn[0], s_attn[1], s_attn[2], s_conv, s_pool)
```

```python
import functools

import jax
import jax.numpy as jnp
from jax import lax
from jax.experimental import pallas as pl
from jax.experimental.pallas import tpu as pltpu

F32 = jnp.float32
BF16 = jnp.bfloat16

D_MODEL = 1024
HEAD_DIM = 64
HEADS = 8
ATTN_WIDTH = HEADS * HEAD_DIM
N_DIL = 3
DIL_WINDOWS = (128, 512, 2048)
DIL_RATES = (1, 4, 16)
ATTN_BLOCK = 128
ATTN_SCALE = HEAD_DIM ** -0.5
ROT_DIM = HEAD_DIM // 4
ROT_HALF = ROT_DIM // 2
ROPE_THETA = 500000.0
QKV_COLS = 3 * N_DIL * ATTN_WIDTH
CONV_CH = D_MODEL // 4
CONV_WIDTH = 31
IN_COLS = QKV_COLS + 2 * CONV_CH
POOL_WINDOWS = (2, 4, 8, 16)
POOL_CH = D_MODEL // len(POOL_WINDOWS)
POOL_PREFIX = max(POOL_WINDOWS) - 1
N_EXPERTS = 16
N_EXPERT_GROUPS = 4
EXPERTS_PER_GROUP = 4
D_EXPERT = 256
DEPTH = 2
DN_ALPHA = (2.0 * DEPTH) ** 0.25
LN_EPS = 1e-5

LANES = 128
SUBLANES = 8
VMEM_LIMIT_BYTES = 56 * 1024 * 1024
ROW_TILE = 512
CONV_HALO = 32
POOL_HALO = 16
STREAM_VMEM_LIMIT_BYTES = 60000 * 1024
MOE_SUBSTEPS = 4
EXPERTS_PER_SUBSTEP = N_EXPERTS // MOE_SUBSTEPS
CACHE_HEAD_BLOCK = 4
HEAD_PAIRS = ATTN_WIDTH // LANES

NT_DIMS = (((1,), (1,)), ((), ()))


def _params(sem):
    return pltpu.CompilerParams(dimension_semantics=sem, vmem_limit_bytes=VMEM_LIMIT_BYTES)


def _const_spec(shape):
    nd = len(shape)
    return pl.BlockSpec(shape, lambda *_: (0,) * nd)


def _resident_spec(shape):
    nd = len(shape)
    return pl.BlockSpec(shape, lambda *_: (0,) * nd, pipeline_mode=pl.Buffered(1))


def _layer_norm(x, g, b):
    mu = jnp.mean(x, axis=-1, keepdims=True)
    xc = x - mu
    var = jnp.mean(xc * xc, axis=-1, keepdims=True)
    return xc * lax.rsqrt(var + LN_EPS) * g + b


def _proj_kernel(x_ref, w_ref, b_ref, tab_ref, q0_ref, q1_ref, q2_ref, kv0_ref, kv1_ref, kv2_ref,
                 glu_ref, kt0_ref, kt1_ref, kt2_ref):
    tm = x_ref.shape[0]
    last = pl.program_id(1) == pl.num_programs(1) - 1
    xb = x_ref[...].astype(BF16)
    cosm, sin_lo, sin_hi = tab_ref[0], tab_ref[1], tab_ref[2]

    def proj(c0, width):
        return (jnp.dot(xb, w_ref[:, c0:c0 + width], preferred_element_type=F32)
                + b_ref[:, c0:c0 + width])

    def rope(t):
        parts = []
        for j in range(t.shape[1] // LANES):
            v = t[:, j * LANES:(j + 1) * LANES]
            parts.append(v * cosm
                         + pltpu.roll(v, LANES - ROT_HALF, 1) * sin_lo
                         + pltpu.roll(v, ROT_HALF, 1) * sin_hi)
        return jnp.concatenate(parts, axis=1)

    q_refs = (q0_ref, q1_ref, q2_ref)
    kv_refs = (kv0_ref, kv1_ref, kv2_ref)
    for g in range(N_DIL):
        c = g * ATTN_WIDTH
        q = rope(proj(c, ATTN_WIDTH)) * ATTN_SCALE
        k = rope(proj(N_DIL * ATTN_WIDTH + c, ATTN_WIDTH))
        v = proj(2 * N_DIL * ATTN_WIDTH + c, ATTN_WIDTH)
        for hp in range(HEAD_PAIRS):
            cols = slice(hp * LANES, (hp + 1) * LANES)
            q_refs[g][hp] = q[:, cols]
            kv_refs[g][hp] = k[:, cols]
            kv_refs[g][HEAD_PAIRS + hp] = v[:, cols]
        if g == 2:
            kt2_ref[0, 0:ATTN_WIDTH, :] = k.T
            kt2_ref[0, ATTN_WIDTH:2 * ATTN_WIDTH, :] = v.T
        elif g == 1:
            @pl.when(last)
            def _():
                kt1_ref[0, 0:ATTN_WIDTH, :] = k.T
                kt1_ref[0, ATTN_WIDTH:2 * ATTN_WIDTH, :] = v.T
        else:
            @pl.when(last)
            def _():
                keep = DIL_WINDOWS[0]
                kt0_ref[0, 0:ATTN_WIDTH, :] = k[tm - keep:, :].T
                kt0_ref[0, ATTN_WIDTH:2 * ATTN_WIDTH, :] = v[tm - keep:, :].T
    ga = proj(QKV_COLS, CONV_CH)
    gb = proj(QKV_COLS + CONV_CH, CONV_CH)
    glu_ref[...] = ga * jax.nn.sigmoid(gb)


def _project(x2d, w_bf, b2d, tables, n, s, tm):
    assert tm == DIL_WINDOWS[1] and tm >= DIL_WINDOWS[0]
    tps = s // tm
    m = n * s
    row = lambda b, t: (b * tps + t, 0)
    seq = lambda b, t: (b, 0, 0)
    kv_rows = 2 * ATTN_WIDTH
    slab = lambda b, t: (0, b * tps + t, 0)
    out_specs = ([pl.BlockSpec((HEAD_PAIRS, tm, LANES), slab)] * N_DIL
                 + [pl.BlockSpec((2 * HEAD_PAIRS, tm, LANES), slab)] * N_DIL
                 + [pl.BlockSpec((tm, CONV_CH), row),
                    pl.BlockSpec((1, kv_rows, DIL_WINDOWS[0]), seq),
                    pl.BlockSpec((1, kv_rows, DIL_WINDOWS[1]), seq),
                    pl.BlockSpec((1, kv_rows, tm), lambda b, t: (b, 0, t))])
    out_shape = ([jax.ShapeDtypeStruct((HEAD_PAIRS, m, LANES), F32)] * N_DIL
                 + [jax.ShapeDtypeStruct((2 * HEAD_PAIRS, m, LANES), F32)] * N_DIL
                 + [jax.ShapeDtypeStruct((m, CONV_CH), F32),
                    jax.ShapeDtypeStruct((n, kv_rows, DIL_WINDOWS[0]), F32),
                    jax.ShapeDtypeStruct((n, kv_rows, DIL_WINDOWS[1]), F32),
                    jax.ShapeDtypeStruct((n, kv_rows, s), F32)])
    return pl.pallas_call(
        _proj_kernel,
        grid=(n, tps),
        in_specs=[
            pl.BlockSpec((tm, D_MODEL), row),
            _resident_spec((D_MODEL, IN_COLS)),
            _const_spec((1, IN_COLS)),
            pl.BlockSpec((3, tm, LANES), lambda b, t: (0, t, 0)),
        ],
        out_specs=out_specs,
        out_shape=out_shape,
        compiler_params=_params(("parallel", "arbitrary")),
        name="in_proj",
    )(x2d, w_bf, b2d, tables)


def _rope_angles(pos):
    inv_freq = ROPE_THETA ** (-jnp.arange(ROT_HALF, dtype=F32) * 2.0 / ROT_DIM)
    return pos.astype(F32)[:, None] * inv_freq[None, :]


def _rope_tables(pos):
    t = pos.shape[0]
    ang = _rope_angles(pos)
    cos, sin = jnp.cos(ang), jnp.sin(ang)
    rest = HEAD_DIM - ROT_DIM
    c64 = jnp.concatenate([cos, cos, jnp.ones((t, rest), F32)], axis=1)
    lo64 = jnp.concatenate([-sin, jnp.zeros((t, HEAD_DIM - ROT_HALF), F32)], axis=1)
    hi64 = jnp.concatenate([jnp.zeros((t, ROT_HALF), F32), sin, jnp.zeros((t, rest), F32)], axis=1)
    rep = LANES // HEAD_DIM
    return jnp.stack([jnp.tile(c64, (1, rep)), jnp.tile(lo64, (1, rep)), jnp.tile(hi64, (1, rep))])


def _attn_kernel(q_ref, kv_ref, bias_ref, o_ref, lse_ref, s_ref, p_ref, *, rate, nblk):
    nk = s_ref.shape[2]
    lane = lax.broadcasted_iota(jnp.int32, (ATTN_BLOCK, LANES), 1)
    low_half = lane < HEAD_DIM
    keep_lo = low_half.astype(BF16)
    keep_hi = 1 - keep_lo
    ones_rhs = jnp.ones((nk, LANES), BF16)

    def strided(start, size):
        return pl.ds(start, size, stride=rate) if rate > 1 else pl.ds(start, size)

    def unit(u, carry):
        if nblk > 1:
            res, b = u // nblk, u % nblk
            bias = bias_ref[jnp.minimum(b, 1)]
        else:
            res, b = u, 0
            bias = bias_ref[0]
        rows_q = strided(b * (ATTN_BLOCK * rate) + res, ATTN_BLOCK)
        rows_k = strided(jnp.maximum(b - 1, 0) * (ATTN_BLOCK * rate) + res, nk)
        for hp in range(HEAD_PAIRS):
            qp = q_ref[hp, rows_q, :].astype(BF16)
            kp = kv_ref[hp, rows_k, :].astype(BF16)
            for half, keep in enumerate((keep_lo, keep_hi)):
                s = lax.dot_general(qp * keep, kp, NT_DIMS, preferred_element_type=F32)
                s_ref[2 * hp + half] = s + bias
        sc = s_ref[...]
        m = jnp.max(sc, axis=-1, keepdims=True)
        p_ref[...] = jnp.exp(sc - m).astype(BF16)
        for hp in range(HEAD_PAIRS):
            vp = kv_ref[HEAD_PAIRS + hp, rows_k, :].astype(BF16)
            rhs = jnp.concatenate([vp, ones_rhs], axis=1)
            ol_lo = jnp.dot(p_ref[2 * hp], rhs, preferred_element_type=F32)
            ol_hi = jnp.dot(p_ref[2 * hp + 1], rhs, preferred_element_type=F32)
            l_lo, l_hi = ol_lo[:, LANES:], ol_hi[:, LANES:]
            o_pair = jnp.where(low_half, ol_lo[:, :LANES] * (1.0 / l_lo), ol_hi[:, :LANES] * (1.0 / l_hi))
            lse_pair = jnp.where(low_half, m[2 * hp] + jnp.log(l_lo), m[2 * hp + 1] + jnp.log(l_hi))
            o_ref[hp, rows_q, :] = o_pair
            lse_ref[hp, rows_q, :] = lse_pair
        return carry

    lax.fori_loop(0, rate * nblk, unit, 0)


def _band_bias(nblk):
    qi = jnp.arange(ATTN_BLOCK)[:, None]
    ci = jnp.arange(ATTN_BLOCK)[None, :]
    causal = jnp.where(ci <= qi, 0.0, -jnp.inf).astype(F32)
    if nblk == 1:
        return jnp.stack([causal, causal])
    band = jnp.where(ci >= qi, 0.0, -jnp.inf).astype(F32)
    closed = jnp.full((ATTN_BLOCK, ATTN_BLOCK), -jnp.inf, F32)
    return jnp.stack([jnp.concatenate([causal, closed], axis=1), jnp.concatenate([band, causal], axis=1)])


def _prompt_attention(q, kv, g, n, s):
    rate = DIL_RATES[g]
    nblk = s // rate // ATTN_BLOCK
    nk = 2 * ATTN_BLOCK if nblk > 1 else ATTN_BLOCK
    out_sds = jax.ShapeDtypeStruct((HEAD_PAIRS, n * s, LANES), F32)
    seq = lambda b: (0, b, 0)
    o_spec = pl.BlockSpec((HEAD_PAIRS, s, LANES), seq)
    return pl.pallas_call(
        functools.partial(_attn_kernel, rate=rate, nblk=nblk),
        grid=(n,),
        in_specs=[
            o_spec,
            pl.BlockSpec((2 * HEAD_PAIRS, s, LANES), seq),
            _const_spec((2, ATTN_BLOCK, nk)),
        ],
        out_specs=[o_spec, o_spec],
        out_shape=[out_sds, out_sds],
        scratch_shapes=[pltpu.VMEM((HEADS, ATTN_BLOCK, nk), F32),
                        pltpu.VMEM((HEADS, ATTN_BLOCK, nk), BF16)],
        compiler_params=_params(("parallel",)),
        name=f"band_attn_g{g}",
    )(q, kv, _band_bias(nblk))


def _group_weights(lses):
    lmax = jnp.maximum(jnp.maximum(lses[0], lses[1]), lses[2])
    es = [jnp.exp(l - lmax) for l in lses]
    inv = 1.0 / (es[0] + es[1] + es[2])
    return [e * inv for e in es]


def _conv_tail(y, cb, clg, clb):
    z = _layer_norm(y + cb, clg, clb)
    return z * jax.nn.sigmoid(z)


def _out_proj_ln(x, attn, conv, wo_ref, lng, lnb):
    y = (jnp.dot(attn.astype(BF16), wo_ref[0:ATTN_WIDTH, :], preferred_element_type=F32)
         + jnp.dot(conv.astype(BF16), wo_ref[ATTN_WIDTH:ATTN_WIDTH + CONV_CH, :],
                   preferred_element_type=F32))
    return _layer_norm(DN_ALPHA * x + y, lng, lnb)


def _mix_kernel(x_ref, o0_ref, o1_ref, o2_ref, l0_ref, l1_ref, l2_ref, gc_ref, gp_ref,
                cw_ref, cb_ref, clg_ref, clb_ref, wo_ref, lng_ref, lnb_ref, h_ref, ext_ref):
    tm = x_ref.shape[0]
    first = pl.program_id(1) == 0
    ext_ref[0:CONV_HALO, :] = jnp.where(first, 0.0, gp_ref[...])
    ext_ref[CONV_HALO:CONV_HALO + tm, :] = gc_ref[...]
    base = CONV_HALO - (CONV_WIDTH - 1)
    acc = cw_ref[0:1, :] * ext_ref[base:base + tm, :]
    for k in range(1, CONV_WIDTH):
        acc = acc + cw_ref[k:k + 1, :] * ext_ref[base + k:base + k + tm, :]
    conv = _conv_tail(acc, cb_ref[...], clg_ref[...], clb_ref[...])
    pieces = []
    for hp in range(HEAD_PAIRS):
        w = _group_weights((l0_ref[hp], l1_ref[hp], l2_ref[hp]))
        pieces.append(w[0] * o0_ref[hp] + w[1] * o1_ref[hp] + w[2] * o2_ref[hp])
    attn = jnp.concatenate(pieces, axis=1)
    h_ref[...] = _out_proj_ln(x_ref[...], attn, conv, wo_ref, lng_ref[...], lnb_ref[...])


def _prompt_mix(x2d, os_, lses, glu, conv_w, conv_b, clg, clb, wo_bf, lng, lnb, n, s, tm):
    tps = s // tm
    hb = tm // CONV_HALO
    row = lambda b, t: (b * tps + t, 0)
    aw = pl.BlockSpec((HEAD_PAIRS, tm, LANES), lambda b, t: (0, b * tps + t, 0))
    return pl.pallas_call(
        _mix_kernel,
        grid=(n, tps),
        in_specs=[
            pl.BlockSpec((tm, D_MODEL), row),
            aw, aw, aw, aw, aw, aw,
            pl.BlockSpec((tm, CONV_CH), row),
            pl.BlockSpec((CONV_HALO, CONV_CH),
                         lambda b, t: (jnp.maximum((b * tps + t) * hb - 1, 0), 0)),
            _const_spec((CONV_WIDTH, CONV_CH)),
            _const_spec((1, CONV_CH)), _const_spec((1, CONV_CH)), _const_spec((1, CONV_CH)),
            _const_spec((ATTN_WIDTH + CONV_CH, D_MODEL)),
            _const_spec((1, D_MODEL)), _const_spec((1, D_MODEL)),
        ],
        out_specs=pl.BlockSpec((tm, D_MODEL), row),
        out_shape=jax.ShapeDtypeStruct((n * s, D_MODEL), F32),
        scratch_shapes=[pltpu.VMEM((CONV_HALO + tm, CONV_CH), F32)],
        compiler_params=_params(("parallel", "parallel")),
        name="mix_out",
    )(x2d, *os_, *lses, glu, glu, conv_w, conv_b, clg, clb, wo_bf, lng, lnb)


def _routing_rows(logit_rows, bias_ref):
    m = logit_rows[0]
    for r in logit_rows[1:]:
        m = jnp.maximum(m, r)
    ex = [jnp.exp(r - m) for r in logit_rows]
    tot = ex[0]
    for e in ex[1:]:
        tot = tot + e
    scores = [e / tot for e in ex]
    sel = [scores[e] + bias_ref[e:e + 1, :] for e in range(N_EXPERTS)]
    grp = []
    for g in range(N_EXPERT_GROUPS):
        v = sel[g * EXPERTS_PER_GROUP:(g + 1) * EXPERTS_PER_GROUP]
        best = v[0] + v[1]
        for i in range(EXPERTS_PER_GROUP):
            for j in range(i + 1, EXPERTS_PER_GROUP):
                if (i, j) != (0, 1):
                    best = jnp.maximum(best, v[i] + v[j])
        grp.append(best)
    gmax = grp[0]
    for v in grp[1:]:
        gmax = jnp.maximum(gmax, v)
    taken = None
    in_group = []
    for g in range(N_EXPERT_GROUPS):
        hit = grp[g] == gmax
        if taken is None:
            in_group.append(hit)
            taken = hit
        else:
            in_group.append(jnp.logical_and(hit, jnp.logical_not(taken)))
            taken = jnp.logical_or(taken, hit)
    gates = []
    for e in range(N_EXPERTS):
        g = e // EXPERTS_PER_GROUP
        rank = jnp.zeros_like(sel[e])
        for o in range(g * EXPERTS_PER_GROUP, (g + 1) * EXPERTS_PER_GROUP):
            if o == e:
                continue
            ahead = sel[o] > sel[e]
            if o < e:
                ahead = jnp.logical_or(ahead, sel[o] == sel[e])
            rank = rank + ahead.astype(F32)
        chosen = jnp.logical_and(in_group[g], rank < float(2))
        gates.append(jnp.where(chosen, scores[e], 0.0))
    den = gates[0]
    for v in gates[1:]:
        den = den + v
    return [v / den for v in gates]


def _moe_kernel(h_ref, rw_ref, rb_ref, w1_ref, w3_ref, w2_ref, lng_ref, lnb_ref, out_ref, ct_ref):
    h = h_ref[...]
    hb = h.astype(BF16)
    logits = jnp.dot(hb, rw_ref[...], preferred_element_type=F32)
    lt = logits.T
    rows = [lt[e:e + 1, :] for e in range(N_EXPERTS)]
    comb_rows = _routing_rows(rows, rb_ref)
    ct_ref[...] = jnp.zeros_like(ct_ref)
    for e in range(N_EXPERTS):
        ct_ref[e:e + 1, :] = comb_rows[e]
    comb = ct_ref[...].T
    acc = jnp.zeros(h.shape, F32)
    for e in range(N_EXPERTS):
        a = jnp.dot(hb, w1_ref[e], preferred_element_type=F32)
        b = jnp.dot(hb, w3_ref[e], preferred_element_type=F32)
        gated = (a * jax.nn.sigmoid(a)) * b * comb[:, e:e + 1]
        acc = acc + jnp.dot(gated.astype(BF16), w2_ref[e], preferred_element_type=F32)
    out_ref[...] = _layer_norm(DN_ALPHA * h + acc, lng_ref[...], lnb_ref[...])


def _moe(h2d, rw_pad, rb, w1, w3, w2, lng, lnb, tm):
    m = h2d.shape[0]
    row = lambda i: (i, 0)
    return pl.pallas_call(
        _moe_kernel,
        grid=(m // tm,),
        in_specs=[
            pl.BlockSpec((tm, D_MODEL), row),
            _const_spec((D_MODEL, LANES)),
            _const_spec((N_EXPERTS, 1)),
            _resident_spec((N_EXPERTS, D_MODEL, D_EXPERT)),
            _resident_spec((N_EXPERTS, D_MODEL, D_EXPERT)),
            _resident_spec((N_EXPERTS, D_EXPERT, D_MODEL)),
            _const_spec((1, D_MODEL)), _const_spec((1, D_MODEL)),
        ],
        out_specs=pl.BlockSpec((tm, D_MODEL), row),
        out_shape=jax.ShapeDtypeStruct((m, D_MODEL), F32),
        scratch_shapes=[pltpu.VMEM((LANES, tm), F32)],
        compiler_params=_params(("parallel",)),
        name="moe_ffn",
    )(h2d, rw_pad, rb, w1, w3, w2, lng, lnb)


def _pool_project(parts, pw_ref, pb_ref, ps_ref):
    cols = []
    for gi in range(len(POOL_WINDOWS)):
        lo = gi * POOL_CH
        y = jnp.dot(parts[gi].astype(BF16), pw_ref[gi], preferred_element_type=F32)
        cols.append((y + pb_ref[gi:gi + 1, :]) * ps_ref[:, lo:lo + POOL_CH])
    return jnp.concatenate(cols, axis=1)


def _pool_kernel(x_ref, xp_ref, pw_ref, pb_ref, ps_ref, lng_ref, lnb_ref, out_ref, ext_ref):
    tm = x_ref.shape[0]
    t = pl.program_id(1)
    ext_ref[0:POOL_HALO, :] = jnp.where(t == 0, 0.0, xp_ref[...])
    ext_ref[POOL_HALO:POOL_HALO + tm, :] = x_ref[...]
    pos = t * tm + lax.broadcasted_iota(jnp.int32, (tm, 1), 0)
    parts = []
    for gi, w in enumerate(POOL_WINDOWS):
        lo = gi * POOL_CH
        cur = x_ref[:, lo:lo + POOL_CH]
        tot = cur
        for j in range(1, w):
            tot = tot + ext_ref[POOL_HALO - j:POOL_HALO - j + tm, lo:lo + POOL_CH]
        cnt = jnp.minimum(w, pos + 1).astype(F32)
        parts.append(tot / cnt - cur)
    y = _pool_project(parts, pw_ref, pb_ref, ps_ref)
    out_ref[...] = _layer_norm(DN_ALPHA * x_ref[...] + y, lng_ref[...], lnb_ref[...])


def _prompt_pool(h2d, pw_bf, pb, ps, lng, lnb, n, s, tm):
    tps = s // tm
    hb = tm // POOL_HALO
    row = lambda b, t: (b * tps + t, 0)
    ng = len(POOL_WINDOWS)
    return pl.pallas_call(
        _pool_kernel,
        grid=(n, tps),
        in_specs=[
            pl.BlockSpec((tm, D_MODEL), row),
            pl.BlockSpec((POOL_HALO, D_MODEL),
                         lambda b, t: (jnp.maximum((b * tps + t) * hb - 1, 0), 0)),
            _const_spec((ng, POOL_CH, POOL_CH)),
            _const_spec((ng, POOL_CH)),
            _const_spec((1, D_MODEL)), _const_spec((1, D_MODEL)), _const_spec((1, D_MODEL)),
        ],
        out_specs=pl.BlockSpec((tm, D_MODEL), row),
        out_shape=jax.ShapeDtypeStruct((n * s, D_MODEL), F32),
        scratch_shapes=[pltpu.VMEM((POOL_HALO + tm, D_MODEL), F32)],
        compiler_params=_params(("parallel", "parallel")),
        name="pool_mix",
    )(h2d, h2d, pw_bf, pb, ps, lng, lnb)


def _dec_proj_kernel(x_ref, w_ref, b_ref, cs_ref, qkvt_ref, glu_ref):
    xb = x_ref[...].astype(BF16)
    cos, sin = cs_ref[0], cs_ref[1]
    chunk = ATTN_WIDTH

    def proj(c0, width):
        return (jnp.dot(xb, w_ref[:, c0:c0 + width], preferred_element_type=F32)
                + b_ref[:, c0:c0 + width])

    for ci in range(QKV_COLS // chunk):
        pt = proj(ci * chunk, chunk).T
        if ci < 2 * N_DIL:
            pieces = []
            for hh in range(HEADS):
                base = hh * HEAD_DIM
                x1 = pt[base:base + ROT_HALF, :]
                x2 = pt[base + ROT_HALF:base + ROT_DIM, :]
                pieces += [x1 * cos - x2 * sin, x2 * cos + x1 * sin, pt[base + ROT_DIM:base + HEAD_DIM, :]]
            pt = jnp.concatenate(pieces, axis=0)
            if ci < N_DIL:
                pt = pt * ATTN_SCALE
        qkvt_ref[ci * chunk:(ci + 1) * chunk, :] = pt
    ga = proj(QKV_COLS, CONV_CH)
    gb = proj(QKV_COLS + CONV_CH, CONV_CH)
    glu_ref[...] = ga * jax.nn.sigmoid(gb)


def _decode_project(x2d, w_bf, b2d, cs):
    nseq = x2d.shape[0]
    args = (x2d, w_bf, b2d, cs)
    return pl.pallas_call(
        _dec_proj_kernel,
        grid=(1,),
        in_specs=[_const_spec(a.shape) for a in args],
        out_specs=[_const_spec((QKV_COLS, nseq)), _const_spec((nseq, CONV_CH))],
        out_shape=[jax.ShapeDtypeStruct((QKV_COLS, nseq), F32),
                   jax.ShapeDtypeStruct((nseq, CONV_CH), F32)],
        compiler_params=_params(("arbitrary",)),
        name="decode_proj",
    )(*args)


def _cache_block(c_ref, nc_ref, ot_ref, lt_ref, qkvt_ref, g, seq, head0):
    hb, w = c_ref.shape[2], c_ref.shape[4]
    nseq = qkvt_ref.shape[1]
    rate = DIL_RATES[g]
    rows = hb * HEAD_DIM
    r0 = pl.multiple_of(head0 * HEAD_DIM, rows)
    mine = lax.broadcasted_iota(jnp.int32, (rows, nseq), 1) == seq

    def column(base):
        x = qkvt_ref[pl.ds(base + r0, rows), :]
        return jnp.sum(jnp.where(mine, x, 0.0), axis=1, keepdims=True).reshape(hb, HEAD_DIM, 1)

    qc = column(g * ATTN_WIDTH)
    kc = column((N_DIL + g) * ATTN_WIDTH)
    vc = column((2 * N_DIL + g) * ATTN_WIDTH)
    kt = c_ref[0, 0]
    vt = c_ref[0, 1]
    tok = lax.broadcasted_iota(jnp.int32, (1, 1, w), 2)
    in_window = (tok & (rate - 1)) == 0
    newest = tok == w - 1

    sc = jnp.where(in_window, jnp.sum(kt * qc, axis=1, keepdims=True), -jnp.inf)
    sn = jnp.sum(qc * kc, axis=1, keepdims=True)
    m = jnp.maximum(jnp.max(sc, axis=2, keepdims=True), sn)
    p = jnp.exp(sc - m)
    pn = jnp.exp(sn - m)
    l = jnp.sum(p, axis=2, keepdims=True) + pn
    inv = 1.0 / l
    o = jnp.sum(vt * (p * inv), axis=2, keepdims=True) + vc * (pn * inv)
    lse = jnp.broadcast_to(m + jnp.log(l), (hb, HEAD_DIM, 1))
    acc_rows = pl.ds(r0, rows)
    ot_ref[acc_rows, :] = jnp.where(mine, o.reshape(rows, 1), ot_ref[acc_rows, :])
    lt_ref[acc_rows, :] = jnp.where(mine, lse.reshape(rows, 1), lt_ref[acc_rows, :])

    def shifted(old, new_col):
        rolled = pltpu.roll(old.reshape(rows, w), w - 1, 1).reshape(hb, HEAD_DIM, w)
        return jnp.where(newest, new_col, rolled)

    nc_ref[0, 0] = shifted(kt, kc)
    nc_ref[0, 1] = shifted(vt, vc)


def _moe_stream_kernel(*refs, groups, seq_base):
    ng = len(groups)
    (h_ref, rw_ref, rb_ref, w1_ref, w3_ref, w2_ref, lng_ref, lnb_ref, qkvt_ref) = refs[:9]
    c_refs = refs[9:9 + ng]
    n_in = 9 + ng + (ng if seq_base else 0)
    out_ref = refs[n_in]
    stream_out = refs[n_in + 1:n_in + 1 + 3 * ng]
    comb_ref, ct_ref = refs[n_in + 1 + 3 * ng:]
    i, k = pl.program_id(0), pl.program_id(1)
    flat = i * MOE_SUBSTEPS + k
    hb = h_ref[...].astype(BF16)

    @pl.when(k == 0)
    def _():
        logits = jnp.dot(hb, rw_ref[...], preferred_element_type=F32)
        lt = logits.T
        comb_rows = _routing_rows([lt[e:e + 1, :] for e in range(N_EXPERTS)], rb_ref)
        ct_ref[...] = jnp.zeros_like(ct_ref)
        for e in range(N_EXPERTS):
            ct_ref[e:e + 1, :] = comb_rows[e]
        comb_ref[...] = ct_ref[...].T
        out_ref[...] = jnp.zeros_like(out_ref)

    @pl.when(flat == 0)
    def _():
        for s in range(ng):
            for t in range(2):
                acc = stream_out[3 * s + 1 + t]
                acc[...] = jnp.zeros_like(acc)

    comb = comb_ref[...]
    lane = lax.broadcasted_iota(jnp.int32, comb.shape, 1)
    ffn = None
    for ee in range(EXPERTS_PER_SUBSTEP):
        e = k * EXPERTS_PER_SUBSTEP + ee
        a = jnp.dot(hb, w1_ref[e], preferred_element_type=F32)
        b = jnp.dot(hb, w3_ref[e], preferred_element_type=F32)
        gate = jnp.sum(jnp.where(lane == e, comb, 0.0), axis=1, keepdims=True)
        gated = (a * jax.nn.sigmoid(a)) * b * gate
        part = jnp.dot(gated.astype(BF16), w2_ref[ee], preferred_element_type=F32)
        ffn = part if ffn is None else ffn + part
    out_ref[...] += ffn

    blocks_per_seq = HEADS // CACHE_HEAD_BLOCK
    for s, g in enumerate(groups):
        _cache_block(c_refs[s], stream_out[3 * s], stream_out[3 * s + 1], stream_out[3 * s + 2], qkvt_ref,
                     g, seq_base + flat // blocks_per_seq, (flat % blocks_per_seq) * CACHE_HEAD_BLOCK)

    @pl.when(k == MOE_SUBSTEPS - 1)
    def _():
        out_ref[...] = _layer_norm(DN_ALPHA * h_ref[...] + out_ref[...], lng_ref[...], lnb_ref[...])


def _moe_stream(h2d, rw_pad, rb, w1, w3, w2, lng, lnb, qkvt, caches_t, tm, seq_base, prev):
    m = h2d.shape[0]
    nseq = qkvt.shape[1]
    groups = tuple(range(N_DIL))
    blocks_per_seq = HEADS // CACHE_HEAD_BLOCK
    first_block = seq_base * blocks_per_seq
    assert ((m // tm) * MOE_SUBSTEPS) % blocks_per_seq == 0
    assert first_block + (m // tm) * MOE_SUBSTEPS <= nseq * blocks_per_seq
    row = lambda i, k: (i, 0)
    acc_spec = _const_spec((ATTN_WIDTH, nseq))
    acc_sds = jax.ShapeDtypeStruct((ATTN_WIDTH, nseq), F32)

    def block_index(i, k):
        b = first_block + i * MOE_SUBSTEPS + k
        return (b // blocks_per_seq, 0, b % blocks_per_seq, 0, 0)

    cache_specs, stream_specs, stream_shapes = [], [], []
    for c in caches_t:
        spec = pl.BlockSpec((1, 2, CACHE_HEAD_BLOCK, HEAD_DIM, c.shape[4]), block_index)
        cache_specs.append(spec)
        stream_specs += [spec, acc_spec, acc_spec]
        stream_shapes += [jax.ShapeDtypeStruct(c.shape, F32), acc_sds, acc_sds]
    extra_in, extra_specs, aliases = [], [], {}
    if seq_base:
        n_fixed = 9 + len(caches_t)
        for s, nc in enumerate(prev):
            aliases[n_fixed + s] = 1 + 3 * s
            extra_in.append(nc)
            extra_specs.append(pl.BlockSpec(memory_space=pl.ANY))
    return pl.pallas_call(
        functools.partial(_moe_stream_kernel, groups=groups, seq_base=seq_base),
        grid=(m // tm, MOE_SUBSTEPS),
        in_specs=[
            pl.BlockSpec((tm, D_MODEL), row),
            _const_spec((D_MODEL, LANES)),
            _const_spec((N_EXPERTS, 1)),
            _resident_spec((N_EXPERTS, D_MODEL, D_EXPERT)),
            _resident_spec((N_EXPERTS, D_MODEL, D_EXPERT)),
            pl.BlockSpec((EXPERTS_PER_SUBSTEP, D_EXPERT, D_MODEL), lambda i, k: (k, 0, 0)),
            _const_spec((1, D_MODEL)), _const_spec((1, D_MODEL)),
            _resident_spec(qkvt.shape),
        ] + cache_specs + extra_specs,
        out_specs=[pl.BlockSpec((tm, D_MODEL), row)] + stream_specs,
        out_shape=[jax.ShapeDtypeStruct((m, D_MODEL), F32)] + stream_shapes,
        input_output_aliases=aliases,
        scratch_shapes=[pltpu.VMEM((tm, LANES), F32), pltpu.VMEM((LANES, tm), F32)],
        compiler_params=pltpu.CompilerParams(dimension_semantics=("arbitrary", "arbitrary"),
                                             vmem_limit_bytes=STREAM_VMEM_LIMIT_BYTES),
        name=f"moe_stream_from{seq_base}",
    )(h2d, rw_pad, rb, w1, w3, w2, lng, lnb, qkvt, *caches_t, *extra_in)


def _dec_mix_kernel(x_ref, *refs, split):
    acc_refs, (st_ref, glu_ref, cw_ref, cb_ref, clg_ref, clb_ref, wo_ref, lng_ref, lnb_ref,
               h_ref, nst_ref) = refs[:4 * N_DIL], refs[4 * N_DIL:]
    npre = CONV_WIDTH - 1
    glu = glu_ref[...]
    acc = cw_ref[npre:npre + 1, :] * glu
    for k in range(npre):
        acc = acc + cw_ref[k:k + 1, :] * st_ref[k]
    conv = _conv_tail(acc, cb_ref[...], clg_ref[...], clb_ref[...])
    first = lax.broadcasted_iota(jnp.int32, (ATTN_WIDTH, x_ref.shape[0]), 1) < split

    def both(idx):
        return jnp.where(first, acc_refs[idx][...], acc_refs[2 * N_DIL + idx][...])

    os_ = [both(g) for g in range(N_DIL)]
    w = _group_weights([both(N_DIL + g) for g in range(N_DIL)])
    attn = (w[0] * os_[0] + w[1] * os_[1] + w[2] * os_[2]).T
    h_ref[...] = _out_proj_ln(x_ref[...], attn, conv, wo_ref, lng_ref[...], lnb_ref[...])
    nst_ref[0:npre - 1] = st_ref[1:npre]
    nst_ref[npre - 1] = glu


def _decode_mix(x2d, accs, split, state_t, glu, conv_w, conv_b, clg, clb, wo_bf, lng, lnb):
    nseq = x2d.shape[0]
    args = (x2d, *accs, state_t, glu, conv_w, conv_b, clg, clb, wo_bf, lng, lnb)
    return pl.pallas_call(
        functools.partial(_dec_mix_kernel, split=split),
        grid=(1,),
        in_specs=[_const_spec(a.shape) for a in args],
        out_specs=[_const_spec((nseq, D_MODEL)), _const_spec(state_t.shape)],
        out_shape=[jax.ShapeDtypeStruct((nseq, D_MODEL), F32),
                   jax.ShapeDtypeStruct(state_t.shape, F32)],
        compiler_params=_params(("arbitrary",)),
        name="decode_mix",
    )(*args)


def _dec_pool_kernel(x_ref, st_ref, pw_ref, pb_ref, ps_ref, lng_ref, lnb_ref, out_ref, nst_ref):
    x = x_ref[...]
    parts = []
    for gi, w in enumerate(POOL_WINDOWS):
        lo = gi * POOL_CH
        cur = x[:, lo:lo + POOL_CH]
        tot = cur
        for j in range(1, w):
            tot = tot + st_ref[POOL_PREFIX - j, :, lo:lo + POOL_CH]
        parts.append(tot / float(w) - cur)
    y = _pool_project(parts, pw_ref, pb_ref, ps_ref)
    out_ref[...] = _layer_norm(DN_ALPHA * x + y, lng_ref[...], lnb_ref[...])
    nst_ref[0:POOL_PREFIX - 1] = st_ref[1:POOL_PREFIX]
    nst_ref[POOL_PREFIX - 1] = x


def _decode_pool(h2d, state_t, pw_bf, pb, ps, lng, lnb):
    nseq = h2d.shape[0]
    args = (h2d, state_t, pw_bf, pb, ps, lng, lnb)
    return pl.pallas_call(
        _dec_pool_kernel,
        grid=(1,),
        in_specs=[_const_spec(a.shape) for a in args],
        out_specs=[_const_spec((nseq, D_MODEL)), _const_spec(state_t.shape)],
        out_shape=[jax.ShapeDtypeStruct((nseq, D_MODEL), F32),
                   jax.ShapeDtypeStruct(state_t.shape, F32)],
        compiler_params=_params(("arbitrary",)),
        name="decode_pool",
    )(*args)


def _token_minor(a):
    return jnp.transpose(a, (0, 2, 3, 4, 1))


def _token_major(a):
    return jnp.transpose(a, (0, 4, 1, 2, 3))


def kernel(x_prompt, x_sample, cache_attn_w128, cache_attn_w512, cache_attn_w2048, state_conv, state_pool,
           w_in, b_in, conv_w, conv_b, conv_ln_g, conv_ln_b, w_out, pool_w, pool_b, pool_scale,
           ln_mix_g, ln_mix_b, ln_ffn_g, ln_ffn_b, router_w, router_bias, moe_w1, moe_w3, moe_w2):
    n, s, d = x_prompt.shape
    nseq = x_sample.shape[0]
    past = cache_attn_w2048.shape[2]
    assert d == D_MODEL and x_sample.shape[1] == 1 and s % ROW_TILE == 0 and s == DIL_WINDOWS[2]
    assert cache_attn_w128.shape[0] == 1 and past == DIL_WINDOWS[2]
    caches = (cache_attn_w128[0], cache_attn_w512[0], cache_attn_w2048[0])

    w_in_bf = w_in[0].astype(BF16)
    w_out_bf = w_out[0].astype(BF16)
    pool_w_bf = pool_w[0].astype(BF16)
    w1_bf, w3_bf, w2_bf = moe_w1.astype(BF16), moe_w3.astype(BF16), moe_w2.astype(BF16)
    rw_pad = jnp.pad(router_w, ((0, 0), (0, LANES - N_EXPERTS))).astype(BF16)
    rb = router_bias.astype(F32).reshape(N_EXPERTS, 1)
    r2 = lambda v: v.reshape(1, -1)
    b_in2 = r2(b_in[0])
    cb, clg, clb = r2(conv_b[0]), r2(conv_ln_g[0]), r2(conv_ln_b[0])
    ps = r2(pool_scale[0])

    def moe(h2d, layer, tm):
        return _moe(h2d, rw_pad, rb, w1_bf[layer], w3_bf[layer], w2_bf[layer],
                    r2(ln_ffn_g[layer]), r2(ln_ffn_b[layer]), tm)

    xs = x_sample.reshape(nseq, d)
    ang = _rope_angles(past + jnp.arange(1, dtype=jnp.int32))
    cs = jnp.stack([jnp.broadcast_to(jnp.cos(ang).T, (ROT_HALF, nseq)),
                    jnp.broadcast_to(jnp.sin(ang).T, (ROT_HALF, nseq))])
    qkvt, glus = _decode_project(xs, w_in_bf, b_in2, cs)
    caches_t = [_token_minor(c) for c in caches]

    seqs_per_layer = (n * s // ROW_TILE) * MOE_SUBSTEPS * CACHE_HEAD_BLOCK // HEADS
    assert DEPTH * seqs_per_layer == nseq

    def moe_stream(h2d, layer, prev):
        res = _moe_stream(h2d, rw_pad, rb, w1_bf[layer], w3_bf[layer], w2_bf[layer],
                          r2(ln_ffn_g[layer]), r2(ln_ffn_b[layer]), qkvt, caches_t, ROW_TILE,
                          layer * seqs_per_layer, prev)
        return res[0], [tuple(res[1 + 3 * g:4 + 3 * g]) for g in range(N_DIL)]

    xp = x_prompt.reshape(n * s, d)
    tabs_p = _rope_tables(jnp.arange(s, dtype=jnp.int32))
    (q0, q1, q2, kv0, kv1, kv2, glu, kt0, kt1, kt2) = _project(xp, w_in_bf, b_in2, tabs_p, n, s, ROW_TILE)
    os_, lses = [], []
    for g, (q, kv) in enumerate(((q0, kv0), (q1, kv1), (q2, kv2))):
        o, l = _prompt_attention(q, kv, g, n, s)
        os_.append(o)
        lses.append(l)
    h = _prompt_mix(xp, os_, lses, glu, conv_w[0], cb, clg, clb, w_out_bf,
                    r2(ln_mix_g[0]), r2(ln_mix_b[0]), n, s, ROW_TILE)
    h, first_pass = moe_stream(h, 0, None)
    p_pool = h.reshape(n, s, d)[:, s - POOL_PREFIX:][None]
    h = _prompt_pool(h, pool_w_bf, pool_b[0], ps, r2(ln_mix_g[1]), r2(ln_mix_b[1]), n, s, ROW_TILE)
    h, second_pass = moe_stream(h, 1, [st[0] for st in first_pass])
    y_prompt = h.reshape(n, s, d)
    p_attn = [_token_major(kt.reshape(n, 2, HEADS, HEAD_DIM, kt.shape[2]))[None] for kt in (kt0, kt1, kt2)]
    p_conv = glu.reshape(n, s, CONV_CH)[:, s - (CONV_WIDTH - 1):][None]

    accs = []
    for streamed in (first_pass, second_pass):
        accs += [st[1] for st in streamed] + [st[2] for st in streamed]
    s_attn = [_token_major(st[0])[None] for st in second_pass]
    conv_t = jnp.transpose(state_conv[0], (1, 0, 2))
    hs, new_conv = _decode_mix(xs, accs, seqs_per_layer, conv_t, glus, conv_w[0], cb, clg, clb, w_out_bf,
                               r2(ln_mix_g[0]), r2(ln_mix_b[0]))
    hs = moe(hs, 0, nseq)
    pool_t = jnp.transpose(state_pool[0], (1, 0, 2))
    hs, new_pool = _decode_pool(hs, pool_t, pool_w_bf, pool_b[0], ps,
                                r2(ln_mix_g[1]), r2(ln_mix_b[1]))
    y_sample = moe(hs, 1, nseq).reshape(nseq, 1, d)
    s_conv = jnp.transpose(new_conv, (1, 0, 2))[None]
    s_pool = jnp.transpose(new_pool, (1, 0, 2))[None]

    return (y_prompt, y_sample, p_attn[0], p_attn[1], p_attn[2], p_conv, p_pool,
            s_attn[0], s_attn[1], s_attn[2], s_conv, s_pool)
```

```python
import functools

import jax
import jax.numpy as jnp
from jax import lax
from jax.experimental import pallas as pl
from jax.experimental.pallas import tpu as pltpu

F32 = jnp.float32
BF16 = jnp.bfloat16

D_MODEL = 1024
HEAD_DIM = 64
HEADS = 8
ATTN_WIDTH = HEADS * HEAD_DIM
N_DIL = 3
DIL_WINDOWS = (128, 512, 2048)
DIL_RATES = (1, 4, 16)
ATTN_BLOCK = 128
ATTN_SCALE = HEAD_DIM ** -0.5
ROT_DIM = HEAD_DIM // 4
ROT_HALF = ROT_DIM // 2
ROPE_THETA = 500000.0
QKV_COLS = 3 * N_DIL * ATTN_WIDTH
CONV_CH = D_MODEL // 4
CONV_WIDTH = 31
IN_COLS = QKV_COLS + 2 * CONV_CH
POOL_WINDOWS = (2, 4, 8, 16)
POOL_CH = D_MODEL // len(POOL_WINDOWS)
POOL_PREFIX = max(POOL_WINDOWS) - 1
N_EXPERTS = 16
N_EXPERT_GROUPS = 4
EXPERTS_PER_GROUP = 4
D_EXPERT = 256
DEPTH = 2
DN_ALPHA = (2.0 * DEPTH) ** 0.25
LN_EPS = 1e-5

LANES = 128
SUBLANES = 8
VMEM_LIMIT_BYTES = 56 * 1024 * 1024
ROW_TILE = 512
CONV_HALO = 32
CONV_SLACK = 16
POOL_HALO = 32
STREAM_VMEM_LIMIT_BYTES = 60000 * 1024
MOE_SUBSTEPS = 4
EXPERTS_PER_SUBSTEP = N_EXPERTS // MOE_SUBSTEPS
CACHE_HEAD_BLOCK = 4
HEAD_PAIRS = ATTN_WIDTH // LANES

NT_DIMS = (((1,), (1,)), ((), ()))


def _params(sem):
    return pltpu.CompilerParams(dimension_semantics=sem, vmem_limit_bytes=VMEM_LIMIT_BYTES)


def _const_spec(shape):
    nd = len(shape)
    return pl.BlockSpec(shape, lambda *_: (0,) * nd)


def _resident_spec(shape):
    nd = len(shape)
    return pl.BlockSpec(shape, lambda *_: (0,) * nd, pipeline_mode=pl.Buffered(1))


def _layer_norm(x, g, b):
    mu = jnp.mean(x, axis=-1, keepdims=True)
    xc = x - mu
    var = jnp.mean(xc * xc, axis=-1, keepdims=True)
    return xc * lax.rsqrt(var + LN_EPS) * g + b


def _proj_kernel(x_ref, w_ref, b_ref, tab_ref, q0_ref, q1_ref, q2_ref, kv0_ref, kv1_ref, kv2_ref,
                 glu_ref, kt0_ref, kt1_ref, kt2_ref):
    tm = x_ref.shape[0]
    last = pl.program_id(1) == pl.num_programs(1) - 1
    xb = x_ref[...].astype(BF16)
    cosm, sin_lo, sin_hi = tab_ref[0], tab_ref[1], tab_ref[2]

    def proj(c0, width):
        return (jnp.dot(xb, w_ref[:, c0:c0 + width], preferred_element_type=F32)
                + b_ref[:, c0:c0 + width])

    def rope(t):
        parts = []
        for j in range(t.shape[1] // LANES):
            v = t[:, j * LANES:(j + 1) * LANES]
            parts.append(v * cosm
                         + pltpu.roll(v, LANES - ROT_HALF, 1) * sin_lo
                         + pltpu.roll(v, ROT_HALF, 1) * sin_hi)
        return jnp.concatenate(parts, axis=1)

    q_refs = (q0_ref, q1_ref, q2_ref)
    kv_refs = (kv0_ref, kv1_ref, kv2_ref)
    for g in range(N_DIL):
        c = g * ATTN_WIDTH
        q = rope(proj(c, ATTN_WIDTH)) * ATTN_SCALE
        k = rope(proj(N_DIL * ATTN_WIDTH + c, ATTN_WIDTH))
        v = proj(2 * N_DIL * ATTN_WIDTH + c, ATTN_WIDTH)
        for hp in range(HEAD_PAIRS):
            cols = slice(hp * LANES, (hp + 1) * LANES)
            q_refs[g][hp] = q[:, cols]
            kv_refs[g][hp] = k[:, cols]
            kv_refs[g][HEAD_PAIRS + hp] = v[:, cols]
        if g == 2:
            kt2_ref[0, 0:ATTN_WIDTH, :] = k.T
            kt2_ref[0, ATTN_WIDTH:2 * ATTN_WIDTH, :] = v.T
        elif g == 1:
            @pl.when(last)
            def _():
                kt1_ref[0, 0:ATTN_WIDTH, :] = k.T
                kt1_ref[0, ATTN_WIDTH:2 * ATTN_WIDTH, :] = v.T
        else:
            @pl.when(last)
            def _():
                keep = DIL_WINDOWS[0]
                kt0_ref[0, 0:ATTN_WIDTH, :] = k[tm - keep:, :].T
                kt0_ref[0, ATTN_WIDTH:2 * ATTN_WIDTH, :] = v[tm - keep:, :].T
    ga = proj(QKV_COLS, CONV_CH)
    gb = proj(QKV_COLS + CONV_CH, CONV_CH)
    glu_ref[...] = ga * jax.nn.sigmoid(gb)


def _project(x2d, w_bf, b2d, tables, n, s, tm):
    assert tm == DIL_WINDOWS[1] and tm >= DIL_WINDOWS[0]
    tps = s // tm
    m = n * s
    row = lambda b, t: (b * tps + t, 0)
    seq = lambda b, t: (b, 0, 0)
    kv_rows = 2 * ATTN_WIDTH
    slab = lambda b, t: (0, b * tps + t, 0)
    out_specs = ([pl.BlockSpec((HEAD_PAIRS, tm, LANES), slab)] * N_DIL
                 + [pl.BlockSpec((2 * HEAD_PAIRS, tm, LANES), slab)] * N_DIL
                 + [pl.BlockSpec((tm, CONV_CH), row),
                    pl.BlockSpec((1, kv_rows, DIL_WINDOWS[0]), seq),
                    pl.BlockSpec((1, kv_rows, DIL_WINDOWS[1]), seq),
                    pl.BlockSpec((1, kv_rows, tm), lambda b, t: (b, 0, t))])
    out_shape = ([jax.ShapeDtypeStruct((HEAD_PAIRS, m, LANES), F32)] * N_DIL
                 + [jax.ShapeDtypeStruct((2 * HEAD_PAIRS, m, LANES), F32)] * N_DIL
                 + [jax.ShapeDtypeStruct((m, CONV_CH), F32),
                    jax.ShapeDtypeStruct((n, kv_rows, DIL_WINDOWS[0]), F32),
                    jax.ShapeDtypeStruct((n, kv_rows, DIL_WINDOWS[1]), F32),
                    jax.ShapeDtypeStruct((n, kv_rows, s), F32)])
    return pl.pallas_call(
        _proj_kernel,
        grid=(n, tps),
        in_specs=[
            pl.BlockSpec((tm, D_MODEL), row),
            _resident_spec((D_MODEL, IN_COLS)),
            _const_spec((1, IN_COLS)),
            pl.BlockSpec((3, tm, LANES), lambda b, t: (0, t, 0)),
        ],
        out_specs=out_specs,
        out_shape=out_shape,
        compiler_params=_params(("parallel", "arbitrary")),
        name="in_proj",
    )(x2d, w_bf, b2d, tables)


def _rope_angles(pos):
    inv_freq = ROPE_THETA ** (-jnp.arange(ROT_HALF, dtype=F32) * 2.0 / ROT_DIM)
    return pos.astype(F32)[:, None] * inv_freq[None, :]


def _rope_tables(pos):
    t = pos.shape[0]
    ang = _rope_angles(pos)
    cos, sin = jnp.cos(ang), jnp.sin(ang)
    rest = HEAD_DIM - ROT_DIM
    c64 = jnp.concatenate([cos, cos, jnp.ones((t, rest), F32)], axis=1)
    lo64 = jnp.concatenate([-sin, jnp.zeros((t, HEAD_DIM - ROT_HALF), F32)], axis=1)
    hi64 = jnp.concatenate([jnp.zeros((t, ROT_HALF), F32), sin, jnp.zeros((t, rest), F32)], axis=1)
    rep = LANES // HEAD_DIM
    return jnp.stack([jnp.tile(c64, (1, rep)), jnp.tile(lo64, (1, rep)), jnp.tile(hi64, (1, rep))])


def _attn_kernel(q_ref, kv_ref, bias_ref, o_ref, lse_ref, s_ref, p_ref, *, rate, nblk):
    nk = s_ref.shape[3]
    lane = lax.broadcasted_iota(jnp.int32, (ATTN_BLOCK, LANES), 1)
    low_half = lane < HEAD_DIM
    keep_lo = low_half.astype(BF16)
    keep_hi = 1 - keep_lo
    ones_rhs = jnp.ones((nk, LANES), BF16)

    def strided(start, size):
        return pl.ds(start, size, stride=rate) if rate > 1 else pl.ds(start, size)

    def unit_rows(u):
        res, b = (u // nblk, u % nblk) if nblk > 1 else (u, 0)
        rows_q = strided(b * (ATTN_BLOCK * rate) + res, ATTN_BLOCK)
        rows_k = strided(jnp.maximum(b - 1, 0) * (ATTN_BLOCK * rate) + res, nk)
        return rows_q, rows_k, bias_ref[jnp.minimum(b, 1)]

    def scores(slot, rows_q, rows_k, bias):
        for hp in range(HEAD_PAIRS):
            qp = q_ref[hp, rows_q, :].astype(BF16)
            kp = kv_ref[hp, rows_k, :].astype(BF16)
            for half, keep in enumerate((keep_lo, keep_hi)):
                s = lax.dot_general(qp * keep, kp, NT_DIMS, preferred_element_type=F32)
                s_ref[slot, 2 * hp + half] = s + bias

    def values(slot, rows_q, rows_k, m):
        for hp in range(HEAD_PAIRS):
            vp = kv_ref[HEAD_PAIRS + hp, rows_k, :].astype(BF16)
            rhs = jnp.concatenate([vp, ones_rhs], axis=1)
            ol_lo = jnp.dot(p_ref[slot, 2 * hp], rhs, preferred_element_type=F32)
            ol_hi = jnp.dot(p_ref[slot, 2 * hp + 1], rhs, preferred_element_type=F32)
            l_lo, l_hi = ol_lo[:, LANES:], ol_hi[:, LANES:]
            o_pair = jnp.where(low_half, ol_lo[:, :LANES] * (1.0 / l_lo), ol_hi[:, :LANES] * (1.0 / l_hi))
            lse_pair = jnp.where(low_half, m[2 * hp] + jnp.log(l_lo), m[2 * hp + 1] + jnp.log(l_hi))
            o_ref[hp, rows_q, :] = o_pair
            lse_ref[hp, rows_q, :] = lse_pair

    def unit_pair(i, carry):
        first, second = unit_rows(2 * i), unit_rows(2 * i + 1)
        scores(0, *first)
        scores(1, *second)
        sc = s_ref[...]
        m = jnp.max(sc, axis=-1, keepdims=True)
        p_ref[...] = jnp.exp(sc - m).astype(BF16)
        values(0, first[0], first[1], m[0])
        values(1, second[0], second[1], m[1])
        return carry

    lax.fori_loop(0, rate * nblk // 2, unit_pair, 0)


def _band_bias(nblk):
    qi = jnp.arange(ATTN_BLOCK)[:, None]
    ci = jnp.arange(ATTN_BLOCK)[None, :]
    causal = jnp.where(ci <= qi, 0.0, -jnp.inf).astype(F32)
    if nblk == 1:
        return jnp.stack([causal, causal])
    band = jnp.where(ci >= qi, 0.0, -jnp.inf).astype(F32)
    closed = jnp.full((ATTN_BLOCK, ATTN_BLOCK), -jnp.inf, F32)
    return jnp.stack([jnp.concatenate([causal, closed], axis=1), jnp.concatenate([band, causal], axis=1)])


def _prompt_attention(q, kv, g, n, s):
    rate = DIL_RATES[g]
    nblk = s // rate // ATTN_BLOCK
    nk = 2 * ATTN_BLOCK if nblk > 1 else ATTN_BLOCK
    out_sds = jax.ShapeDtypeStruct((HEAD_PAIRS, n * s, LANES), F32)
    seq = lambda b: (0, b, 0)
    o_spec = pl.BlockSpec((HEAD_PAIRS, s, LANES), seq)
    return pl.pallas_call(
        functools.partial(_attn_kernel, rate=rate, nblk=nblk),
        grid=(n,),
        in_specs=[
            o_spec,
            pl.BlockSpec((2 * HEAD_PAIRS, s, LANES), seq),
            _const_spec((2, ATTN_BLOCK, nk)),
        ],
        out_specs=[o_spec, o_spec],
        out_shape=[out_sds, out_sds],
        scratch_shapes=[pltpu.VMEM((2, HEADS, ATTN_BLOCK, nk), F32),
                        pltpu.VMEM((2, HEADS, ATTN_BLOCK, nk), BF16)],
        compiler_params=_params(("parallel",)),
        name=f"band_attn_g{g}",
    )(q, kv, _band_bias(nblk))


def _group_weights(lses):
    lmax = jnp.maximum(jnp.maximum(lses[0], lses[1]), lses[2])
    es = [jnp.exp(l - lmax) for l in lses]
    inv = 1.0 / (es[0] + es[1] + es[2])
    return [e * inv for e in es]


def _conv_tail(y, cb, clg, clb):
    z = _layer_norm(y + cb, clg, clb)
    return z * jax.nn.sigmoid(z)


def _out_proj_ln(x, attn, conv, wo_ref, lng, lnb):
    y = (jnp.dot(attn.astype(BF16), wo_ref[0:ATTN_WIDTH, :], preferred_element_type=F32)
         + jnp.dot(conv.astype(BF16), wo_ref[ATTN_WIDTH:ATTN_WIDTH + CONV_CH, :],
                   preferred_element_type=F32))
    return _layer_norm(DN_ALPHA * x + y, lng, lnb)


def _mix_kernel(x_ref, o0_ref, o1_ref, o2_ref, l0_ref, l1_ref, l2_ref, gc_ref, gp_ref,
                cw_ref, cb_ref, clg_ref, clb_ref, wo_ref, lng_ref, lnb_ref, h_ref, ext_ref, phase_ref):
    tm = x_ref.shape[0]
    first = pl.program_id(1) == 0
    ext_ref[0:CONV_HALO, :] = jnp.where(first, 0.0, gp_ref[...])
    ext_ref[CONV_HALO:CONV_HALO + tm, :] = gc_ref[...]
    ext_ref[CONV_HALO + tm:, :] = jnp.zeros((CONV_SLACK, CONV_CH), F32)
    base = CONV_HALO - (CONV_WIDTH - 1)
    span = tm + CONV_SLACK
    acc = None
    for r in range(SUBLANES):
        part = None
        for k in range(r, CONV_WIDTH, SUBLANES):
            term = cw_ref[k:k + 1, :] * ext_ref[k - r:k - r + span, :]
            part = term if part is None else part + term
        phase_ref[r] = part
        shifted = phase_ref[r, base + r:base + r + tm, :]
        acc = shifted if acc is None else acc + shifted
    conv = _conv_tail(acc, cb_ref[...], clg_ref[...], clb_ref[...])
    pieces = []
    for hp in range(HEAD_PAIRS):
        w = _group_weights((l0_ref[hp], l1_ref[hp], l2_ref[hp]))
        pieces.append(w[0] * o0_ref[hp] + w[1] * o1_ref[hp] + w[2] * o2_ref[hp])
    attn = jnp.concatenate(pieces, axis=1)
    h_ref[...] = _out_proj_ln(x_ref[...], attn, conv, wo_ref, lng_ref[...], lnb_ref[...])


def _prompt_mix(x2d, os_, lses, glu, conv_w, conv_b, clg, clb, wo_bf, lng, lnb, n, s, tm):
    tps = s // tm
    hb = tm // CONV_HALO
    row = lambda b, t: (b * tps + t, 0)
    aw = pl.BlockSpec((HEAD_PAIRS, tm, LANES), lambda b, t: (0, b * tps + t, 0))
    return pl.pallas_call(
        _mix_kernel,
        grid=(n, tps),
        in_specs=[
            pl.BlockSpec((tm, D_MODEL), row),
            aw, aw, aw, aw, aw, aw,
            pl.BlockSpec((tm, CONV_CH), row),
            pl.BlockSpec((CONV_HALO, CONV_CH),
                         lambda b, t: (jnp.maximum((b * tps + t) * hb - 1, 0), 0)),
            _const_spec((CONV_WIDTH, CONV_CH)),
            _const_spec((1, CONV_CH)), _const_spec((1, CONV_CH)), _const_spec((1, CONV_CH)),
            _const_spec((ATTN_WIDTH + CONV_CH, D_MODEL)),
            _const_spec((1, D_MODEL)), _const_spec((1, D_MODEL)),
        ],
        out_specs=pl.BlockSpec((tm, D_MODEL), row),
        out_shape=jax.ShapeDtypeStruct((n * s, D_MODEL), F32),
        scratch_shapes=[pltpu.VMEM((CONV_HALO + tm + CONV_SLACK, CONV_CH), F32),
                        pltpu.VMEM((SUBLANES, tm + CONV_SLACK, CONV_CH), F32)],
        compiler_params=_params(("parallel", "parallel")),
        name="mix_out",
    )(x2d, *os_, *lses, glu, glu, conv_w, conv_b, clg, clb, wo_bf, lng, lnb)


def _routing_rows(logit_rows, bias_ref):
    m = logit_rows[0]
    for r in logit_rows[1:]:
        m = jnp.maximum(m, r)
    ex = [jnp.exp(r - m) for r in logit_rows]
    tot = ex[0]
    for e in ex[1:]:
        tot = tot + e
    scores = [e / tot for e in ex]
    sel = [scores[e] + bias_ref[e:e + 1, :] for e in range(N_EXPERTS)]
    grp = []
    for g in range(N_EXPERT_GROUPS):
        v = sel[g * EXPERTS_PER_GROUP:(g + 1) * EXPERTS_PER_GROUP]
        best = v[0] + v[1]
        for i in range(EXPERTS_PER_GROUP):
            for j in range(i + 1, EXPERTS_PER_GROUP):
                if (i, j) != (0, 1):
                    best = jnp.maximum(best, v[i] + v[j])
        grp.append(best)
    gmax = grp[0]
    for v in grp[1:]:
        gmax = jnp.maximum(gmax, v)
    taken = None
    in_group = []
    for g in range(N_EXPERT_GROUPS):
        hit = grp[g] == gmax
        if taken is None:
            in_group.append(hit)
            taken = hit
        else:
            in_group.append(jnp.logical_and(hit, jnp.logical_not(taken)))
            taken = jnp.logical_or(taken, hit)
    gates = []
    for e in range(N_EXPERTS):
        g = e // EXPERTS_PER_GROUP
        rank = jnp.zeros_like(sel[e])
        for o in range(g * EXPERTS_PER_GROUP, (g + 1) * EXPERTS_PER_GROUP):
            if o == e:
                continue
            ahead = sel[o] > sel[e]
            if o < e:
                ahead = jnp.logical_or(ahead, sel[o] == sel[e])
            rank = rank + ahead.astype(F32)
        chosen = jnp.logical_and(in_group[g], rank < float(2))
        gates.append(jnp.where(chosen, scores[e], 0.0))
    den = gates[0]
    for v in gates[1:]:
        den = den + v
    return [v / den for v in gates]


def _pool_project(parts, pw_ref, pb_ref, ps_ref):
    cols = []
    for gi in range(len(POOL_WINDOWS)):
        lo = gi * POOL_CH
        y = jnp.dot(parts[gi].astype(BF16), pw_ref[gi], preferred_element_type=F32)
        cols.append((y + pb_ref[gi:gi + 1, :]) * ps_ref[:, lo:lo + POOL_CH])
    return jnp.concatenate(cols, axis=1)


def _pool_kernel(x_ref, xp_ref, pw_ref, pb_ref, ps_ref, lng_ref, lnb_ref, out_ref, ext_ref, lv_ref):
    tm = x_ref.shape[0]
    t = pl.program_id(1)
    ext_ref[0:POOL_HALO, :] = jnp.where(t == 0, 0.0, xp_ref[...])
    ext_ref[POOL_HALO:POOL_HALO + tm, :] = x_ref[...]
    end = POOL_HALO + tm
    src = ext_ref
    for i, shift in enumerate((1, 2, 4)):
        r0, c0 = SUBLANES * (i + 1), i * POOL_CH
        lv_ref[i, r0:end, c0:] = src[r0:end, c0:] + src[r0 - shift:end - shift, c0:]
        src = lv_ref.at[i]
    pos = t * tm + lax.broadcasted_iota(jnp.int32, (tm, 1), 0)
    parts = []
    for gi, w in enumerate(POOL_WINDOWS):
        lo = gi * POOL_CH
        cur = x_ref[:, lo:lo + POOL_CH]
        if gi < 3:
            tot = lv_ref[gi, POOL_HALO:end, lo:lo + POOL_CH]
        else:
            tot = (lv_ref[2, POOL_HALO:end, lo:lo + POOL_CH]
                   + lv_ref[2, POOL_HALO - SUBLANES:end - SUBLANES, lo:lo + POOL_CH])
        cnt = jnp.minimum(w, pos + 1).astype(F32)
        parts.append(tot / cnt - cur)
    y = _pool_project(parts, pw_ref, pb_ref, ps_ref)
    out_ref[...] = _layer_norm(DN_ALPHA * x_ref[...] + y, lng_ref[...], lnb_ref[...])


def _prompt_pool(h2d, pw_bf, pb, ps, lng, lnb, n, s, tm):
    tps = s // tm
    hb = tm // POOL_HALO
    row = lambda b, t: (b * tps + t, 0)
    ng = len(POOL_WINDOWS)
    return pl.pallas_call(
        _pool_kernel,
        grid=(n, tps),
        in_specs=[
            pl.BlockSpec((tm, D_MODEL), row),
            pl.BlockSpec((POOL_HALO, D_MODEL),
                         lambda b, t: (jnp.maximum((b * tps + t) * hb - 1, 0), 0)),
            _const_spec((ng, POOL_CH, POOL_CH)),
            _const_spec((ng, POOL_CH)),
            _const_spec((1, D_MODEL)), _const_spec((1, D_MODEL)), _const_spec((1, D_MODEL)),
        ],
        out_specs=pl.BlockSpec((tm, D_MODEL), row),
        out_shape=jax.ShapeDtypeStruct((n * s, D_MODEL), F32),
        scratch_shapes=[pltpu.VMEM((POOL_HALO + tm, D_MODEL), F32),
                        pltpu.VMEM((3, POOL_HALO + tm, D_MODEL), F32)],
        compiler_params=_params(("parallel", "parallel")),
        name="pool_mix",
    )(h2d, h2d, pw_bf, pb, ps, lng, lnb)


def _dec_proj_kernel(x_ref, w_ref, b_ref, cs_ref, qkvt_ref, glu_ref):
    xb = x_ref[...].astype(BF16)
    cos, sin = cs_ref[0], cs_ref[1]
    chunk = ATTN_WIDTH

    def proj(c0, width):
        return (jnp.dot(xb, w_ref[:, c0:c0 + width], preferred_element_type=F32)
                + b_ref[:, c0:c0 + width])

    for ci in range(QKV_COLS // chunk):
        pt = proj(ci * chunk, chunk).T
        if ci < 2 * N_DIL:
            pieces = []
            for hh in range(HEADS):
                base = hh * HEAD_DIM
                x1 = pt[base:base + ROT_HALF, :]
                x2 = pt[base + ROT_HALF:base + ROT_DIM, :]
                pieces += [x1 * cos - x2 * sin, x2 * cos + x1 * sin, pt[base + ROT_DIM:base + HEAD_DIM, :]]
            pt = jnp.concatenate(pieces, axis=0)
            if ci < N_DIL:
                pt = pt * ATTN_SCALE
        qkvt_ref[ci * chunk:(ci + 1) * chunk, :] = pt
    ga = proj(QKV_COLS, CONV_CH)
    gb = proj(QKV_COLS + CONV_CH, CONV_CH)
    glu_ref[...] = ga * jax.nn.sigmoid(gb)


def _decode_project(x2d, w_bf, b2d, cs):
    nseq = x2d.shape[0]
    args = (x2d, w_bf, b2d, cs)
    return pl.pallas_call(
        _dec_proj_kernel,
        grid=(1,),
        in_specs=[_const_spec(a.shape) for a in args],
        out_specs=[_const_spec((QKV_COLS, nseq)), _const_spec((nseq, CONV_CH))],
        out_shape=[jax.ShapeDtypeStruct((QKV_COLS, nseq), F32),
                   jax.ShapeDtypeStruct((nseq, CONV_CH), F32)],
        compiler_params=_params(("arbitrary",)),
        name="decode_proj",
    )(*args)


def _cache_block(c_ref, nc_ref, ot_ref, lt_ref, qkvt_ref, g, seq, head0):
    hb, w = c_ref.shape[2], c_ref.shape[4]
    nseq = qkvt_ref.shape[1]
    rate = DIL_RATES[g]
    rows = hb * HEAD_DIM
    r0 = pl.multiple_of(head0 * HEAD_DIM, rows)
    mine = lax.broadcasted_iota(jnp.int32, (rows, nseq), 1) == seq

    def column(base):
        x = qkvt_ref[pl.ds(base + r0, rows), :]
        return jnp.sum(jnp.where(mine, x, 0.0), axis=1, keepdims=True).reshape(hb, HEAD_DIM, 1)

    qc = column(g * ATTN_WIDTH)
    kc = column((N_DIL + g) * ATTN_WIDTH)
    vc = column((2 * N_DIL + g) * ATTN_WIDTH)
    kt = c_ref[0, 0]
    vt = c_ref[0, 1]
    tok = lax.broadcasted_iota(jnp.int32, (1, 1, w), 2)
    in_window = (tok & (rate - 1)) == 0
    newest = tok == w - 1

    sc = jnp.where(in_window, jnp.sum(kt * qc, axis=1, keepdims=True), -jnp.inf)
    sn = jnp.sum(qc * kc, axis=1, keepdims=True)
    m = jnp.maximum(jnp.max(sc, axis=2, keepdims=True), sn)
    p = jnp.exp(sc - m)
    pn = jnp.exp(sn - m)
    l = jnp.sum(p, axis=2, keepdims=True) + pn
    inv = 1.0 / l
    o = jnp.sum(vt * (p * inv), axis=2, keepdims=True) + vc * (pn * inv)
    lse = jnp.broadcast_to(m + jnp.log(l), (hb, HEAD_DIM, 1))
    acc_rows = pl.ds(r0, rows)
    ot_ref[acc_rows, :] = jnp.where(mine, o.reshape(rows, 1), ot_ref[acc_rows, :])
    lt_ref[acc_rows, :] = jnp.where(mine, lse.reshape(rows, 1), lt_ref[acc_rows, :])

    def shifted(old, new_col):
        rolled = pltpu.roll(old.reshape(rows, w), w - 1, 1).reshape(hb, HEAD_DIM, w)
        return jnp.where(newest, new_col, rolled)

    nc_ref[0, 0] = shifted(kt, kc)
    nc_ref[0, 1] = shifted(vt, vc)


def _moe_stream_kernel(*refs, groups, seq_base, resident):
    ng = len(groups)
    (h_ref, rw_ref, rb_ref, w1_ref, w3_ref, w2_ref, lng_ref, lnb_ref, qkvt_ref) = refs[:9]
    c_refs = refs[9:9 + ng]
    n_in = 9 + ng + (ng if seq_base else 0)
    out_ref = refs[n_in]
    stream_out = refs[n_in + 1:n_in + 1 + 3 * ng]
    comb_ref, ct_ref = refs[n_in + 1 + 3 * ng:]
    i, k = pl.program_id(0), pl.program_id(1)
    flat = i * MOE_SUBSTEPS + k
    hb = h_ref[...].astype(BF16)

    @pl.when(k == 0)
    def _():
        logits = jnp.dot(hb, rw_ref[...], preferred_element_type=F32)
        lt = logits.T
        comb_rows = _routing_rows([lt[e:e + 1, :] for e in range(N_EXPERTS)], rb_ref)
        ct_ref[...] = jnp.zeros_like(ct_ref)
        for e in range(N_EXPERTS):
            ct_ref[e:e + 1, :] = comb_rows[e]
        comb_ref[...] = ct_ref[...].T
        out_ref[...] = jnp.zeros_like(out_ref)

    @pl.when(flat == 0)
    def _():
        for s in range(ng):
            for t in range(2):
                acc = stream_out[3 * s + 1 + t]
                acc[...] = jnp.zeros_like(acc)

    comb = comb_ref[...]
    lane = lax.broadcasted_iota(jnp.int32, comb.shape, 1)
    ffn = None
    for ee in range(EXPERTS_PER_SUBSTEP):
        e = k * EXPERTS_PER_SUBSTEP + ee
        we = e if resident else ee
        a = jnp.dot(hb, w1_ref[we], preferred_element_type=F32)
        b = jnp.dot(hb, w3_ref[we], preferred_element_type=F32)
        gate = jnp.sum(jnp.where(lane == e, comb, 0.0), axis=1, keepdims=True)
        gated = (a * jax.nn.sigmoid(a)) * b * gate
        part = jnp.dot(gated.astype(BF16), w2_ref[ee], preferred_element_type=F32)
        ffn = part if ffn is None else ffn + part
    out_ref[...] += ffn

    blocks_per_seq = HEADS // CACHE_HEAD_BLOCK
    for s, g in enumerate(groups):
        _cache_block(c_refs[s], stream_out[3 * s], stream_out[3 * s + 1], stream_out[3 * s + 2], qkvt_ref,
                     g, seq_base + flat // blocks_per_seq, (flat % blocks_per_seq) * CACHE_HEAD_BLOCK)

    @pl.when(k == MOE_SUBSTEPS - 1)
    def _():
        out_ref[...] = _layer_norm(DN_ALPHA * h_ref[...] + out_ref[...], lng_ref[...], lnb_ref[...])


def _moe_stream(h2d, rw_pad, rb, w1, w3, w2, lng, lnb, qkvt, caches_t, tm, seq_base, prev):
    m = h2d.shape[0]
    nseq = qkvt.shape[1]
    groups = tuple(range(len(caches_t)))
    resident = bool(caches_t)
    blocks_per_seq = HEADS // CACHE_HEAD_BLOCK
    first_block = seq_base * blocks_per_seq
    if caches_t:
        assert len(caches_t) == N_DIL and ((m // tm) * MOE_SUBSTEPS) % blocks_per_seq == 0
        assert first_block + (m // tm) * MOE_SUBSTEPS <= nseq * blocks_per_seq
    row = lambda i, k: (i, 0)
    step_block = lambda i, k: (k, 0, 0)
    w13_spec = (_resident_spec((N_EXPERTS, D_MODEL, D_EXPERT)) if resident
                else pl.BlockSpec((EXPERTS_PER_SUBSTEP, D_MODEL, D_EXPERT), step_block))
    acc_spec = _const_spec((ATTN_WIDTH, nseq))
    acc_sds = jax.ShapeDtypeStruct((ATTN_WIDTH, nseq), F32)

    def block_index(i, k):
        b = first_block + i * MOE_SUBSTEPS + k
        return (b // blocks_per_seq, 0, b % blocks_per_seq, 0, 0)

    cache_specs, stream_specs, stream_shapes = [], [], []
    for c in caches_t:
        spec = pl.BlockSpec((1, 2, CACHE_HEAD_BLOCK, HEAD_DIM, c.shape[4]), block_index)
        cache_specs.append(spec)
        stream_specs += [spec, acc_spec, acc_spec]
        stream_shapes += [jax.ShapeDtypeStruct(c.shape, F32), acc_sds, acc_sds]
    extra_in, extra_specs, aliases = [], [], {}
    if seq_base:
        n_fixed = 9 + len(caches_t)
        for s, nc in enumerate(prev):
            aliases[n_fixed + s] = 1 + 3 * s
            extra_in.append(nc)
            extra_specs.append(pl.BlockSpec(memory_space=pl.ANY))
    return pl.pallas_call(
        functools.partial(_moe_stream_kernel, groups=groups, seq_base=seq_base, resident=resident),
        grid=(m // tm, MOE_SUBSTEPS),
        in_specs=[
            pl.BlockSpec((tm, D_MODEL), row),
            _const_spec((D_MODEL, LANES)),
            _const_spec((N_EXPERTS, 1)),
            w13_spec,
            w13_spec,
            pl.BlockSpec((EXPERTS_PER_SUBSTEP, D_EXPERT, D_MODEL), step_block),
            _const_spec((1, D_MODEL)), _const_spec((1, D_MODEL)),
            _resident_spec(qkvt.shape),
        ] + cache_specs + extra_specs,
        out_specs=[pl.BlockSpec((tm, D_MODEL), row)] + stream_specs,
        out_shape=[jax.ShapeDtypeStruct((m, D_MODEL), F32)] + stream_shapes,
        input_output_aliases=aliases,
        scratch_shapes=[pltpu.VMEM((tm, LANES), F32), pltpu.VMEM((LANES, tm), F32)],
        compiler_params=pltpu.CompilerParams(dimension_semantics=("arbitrary", "arbitrary"),
                                             vmem_limit_bytes=STREAM_VMEM_LIMIT_BYTES),
        name=f"moe_stream_from{seq_base}" if caches_t else "moe_ffn",
    )(h2d, rw_pad, rb, w1, w3, w2, lng, lnb, qkvt, *caches_t, *extra_in)


def _dec_mix_kernel(x_ref, *refs, split):
    acc_refs, (st_ref, glu_ref, cw_ref, cb_ref, clg_ref, clb_ref, wo_ref, lng_ref, lnb_ref,
               h_ref, nst_ref) = refs[:4 * N_DIL], refs[4 * N_DIL:]
    npre = CONV_WIDTH - 1
    glu = glu_ref[...]
    acc = cw_ref[npre:npre + 1, :] * glu
    for k in range(npre):
        acc = acc + cw_ref[k:k + 1, :] * st_ref[k]
    conv = _conv_tail(acc, cb_ref[...], clg_ref[...], clb_ref[...])
    first = lax.broadcasted_iota(jnp.int32, (ATTN_WIDTH, x_ref.shape[0]), 1) < split

    def both(idx):
        return jnp.where(first, acc_refs[idx][...], acc_refs[2 * N_DIL + idx][...])

    os_ = [both(g) for g in range(N_DIL)]
    w = _group_weights([both(N_DIL + g) for g in range(N_DIL)])
    attn = (w[0] * os_[0] + w[1] * os_[1] + w[2] * os_[2]).T
    h_ref[...] = _out_proj_ln(x_ref[...], attn, conv, wo_ref, lng_ref[...], lnb_ref[...])
    nst_ref[0:npre - 1] = st_ref[1:npre]
    nst_ref[npre - 1] = glu


def _decode_mix(x2d, accs, split, state_t, glu, conv_w, conv_b, clg, clb, wo_bf, lng, lnb):
    nseq = x2d.shape[0]
    args = (x2d, *accs, state_t, glu, conv_w, conv_b, clg, clb, wo_bf, lng, lnb)
    return pl.pallas_call(
        functools.partial(_dec_mix_kernel, split=split),
        grid=(1,),
        in_specs=[_const_spec(a.shape) for a in args],
        out_specs=[_const_spec((nseq, D_MODEL)), _const_spec(state_t.shape)],
        out_shape=[jax.ShapeDtypeStruct((nseq, D_MODEL), F32),
                   jax.ShapeDtypeStruct(state_t.shape, F32)],
        compiler_params=_params(("arbitrary",)),
        name="decode_mix",
    )(*args)


def _dec_pool_kernel(x_ref, st_ref, pw_ref, pb_ref, ps_ref, lng_ref, lnb_ref, out_ref, nst_ref):
    x = x_ref[...]
    parts = []
    for gi, w in enumerate(POOL_WINDOWS):
        lo = gi * POOL_CH
        cur = x[:, lo:lo + POOL_CH]
        tot = cur
        for j in range(1, w):
            tot = tot + st_ref[POOL_PREFIX - j, :, lo:lo + POOL_CH]
        parts.append(tot / float(w) - cur)
    y = _pool_project(parts, pw_ref, pb_ref, ps_ref)
    out_ref[...] = _layer_norm(DN_ALPHA * x + y, lng_ref[...], lnb_ref[...])
    nst_ref[0:POOL_PREFIX - 1] = st_ref[1:POOL_PREFIX]
    nst_ref[POOL_PREFIX - 1] = x


def _decode_pool(h2d, state_t, pw_bf, pb, ps, lng, lnb):
    nseq = h2d.shape[0]
    args = (h2d, state_t, pw_bf, pb, ps, lng, lnb)
    return pl.pallas_call(
        _dec_pool_kernel,
        grid=(1,),
        in_specs=[_const_spec(a.shape) for a in args],
        out_specs=[_const_spec((nseq, D_MODEL)), _const_spec(state_t.shape)],
        out_shape=[jax.ShapeDtypeStruct((nseq, D_MODEL), F32),
                   jax.ShapeDtypeStruct(state_t.shape, F32)],
        compiler_params=_params(("arbitrary",)),
        name="decode_pool",
    )(*args)


def _token_minor(a):
    return jnp.transpose(a, (0, 2, 3, 4, 1))


def _token_major(a):
    return jnp.transpose(a, (0, 4, 1, 2, 3))


def kernel(x_prompt, x_sample, cache_attn_w128, cache_attn_w512, cache_attn_w2048, state_conv, state_pool,
           w_in, b_in, conv_w, conv_b, conv_ln_g, conv_ln_b, w_out, pool_w, pool_b, pool_scale,
           ln_mix_g, ln_mix_b, ln_ffn_g, ln_ffn_b, router_w, router_bias, moe_w1, moe_w3, moe_w2):
    n, s, d = x_prompt.shape
    nseq = x_sample.shape[0]
    past = cache_attn_w2048.shape[2]
    assert d == D_MODEL and x_sample.shape[1] == 1 and s % ROW_TILE == 0 and s == DIL_WINDOWS[2]
    assert cache_attn_w128.shape[0] == 1 and past == DIL_WINDOWS[2]
    caches = (cache_attn_w128[0], cache_attn_w512[0], cache_attn_w2048[0])

    w_in_bf = w_in[0].astype(BF16)
    w_out_bf = w_out[0].astype(BF16)
    pool_w_bf = pool_w[0].astype(BF16)
    moe_bf = [tuple(w[layer].astype(BF16) for w in (moe_w1, moe_w3, moe_w2)) for layer in range(DEPTH)]
    rw_pad = jnp.pad(router_w, ((0, 0), (0, LANES - N_EXPERTS))).astype(BF16)
    rb = router_bias.astype(F32).reshape(N_EXPERTS, 1)
    r2 = lambda v: v.reshape(1, -1)
    b_in2 = r2(b_in[0])
    cb, clg, clb = r2(conv_b[0]), r2(conv_ln_g[0]), r2(conv_ln_b[0])
    ps = r2(pool_scale[0])

    xs = x_sample.reshape(nseq, d)
    ang = _rope_angles(past + jnp.arange(1, dtype=jnp.int32))
    cs = jnp.stack([jnp.broadcast_to(jnp.cos(ang).T, (ROT_HALF, nseq)),
                    jnp.broadcast_to(jnp.sin(ang).T, (ROT_HALF, nseq))])
    qkvt, glus = _decode_project(xs, w_in_bf, b_in2, cs)
    caches_t = [_token_minor(c) for c in caches]

    seqs_per_layer = (n * s // ROW_TILE) * MOE_SUBSTEPS * CACHE_HEAD_BLOCK // HEADS
    assert DEPTH * seqs_per_layer == nseq

    def moe_stream(h2d, layer, prev):
        res = _moe_stream(h2d, rw_pad, rb, *moe_bf[layer], r2(ln_ffn_g[layer]), r2(ln_ffn_b[layer]),
                          qkvt, caches_t, ROW_TILE, layer * seqs_per_layer, prev)
        return res[0], [tuple(res[1 + 3 * g:4 + 3 * g]) for g in range(N_DIL)]

    def moe(h2d, layer):
        return _moe_stream(h2d, rw_pad, rb, *moe_bf[layer], r2(ln_ffn_g[layer]), r2(ln_ffn_b[layer]),
                           qkvt, [], h2d.shape[0], 0, None)[0]

    xp = x_prompt.reshape(n * s, d)
    tabs_p = _rope_tables(jnp.arange(s, dtype=jnp.int32))
    (q0, q1, q2, kv0, kv1, kv2, glu, kt0, kt1, kt2) = _project(xp, w_in_bf, b_in2, tabs_p, n, s, ROW_TILE)
    os_, lses = [], []
    for g, (q, kv) in enumerate(((q0, kv0), (q1, kv1), (q2, kv2))):
        o, l = _prompt_attention(q, kv, g, n, s)
        os_.append(o)
        lses.append(l)
    h = _prompt_mix(xp, os_, lses, glu, conv_w[0], cb, clg, clb, w_out_bf,
                    r2(ln_mix_g[0]), r2(ln_mix_b[0]), n, s, ROW_TILE)
    h, first_pass = moe_stream(h, 0, None)
    p_pool = h.reshape(n, s, d)[:, s - POOL_PREFIX:][None]
    h = _prompt_pool(h, pool_w_bf, pool_b[0], ps, r2(ln_mix_g[1]), r2(ln_mix_b[1]), n, s, ROW_TILE)
    h, second_pass = moe_stream(h, 1, [st[0] for st in first_pass])
    y_prompt = h.reshape(n, s, d)
    p_attn = [_token_major(kt.reshape(n, 2, HEADS, HEAD_DIM, kt.shape[2]))[None] for kt in (kt0, kt1, kt2)]
    p_conv = glu.reshape(n, s, CONV_CH)[:, s - (CONV_WIDTH - 1):][None]

    accs = []
    for streamed in (first_pass, second_pass):
        accs += [st[1] for st in streamed] + [st[2] for st in streamed]
    s_attn = [_token_major(st[0])[None] for st in second_pass]
    conv_t = jnp.transpose(state_conv[0], (1, 0, 2))
    hs, new_conv = _decode_mix(xs, accs, seqs_per_layer, conv_t, glus, conv_w[0], cb, clg, clb, w_out_bf,
                               r2(ln_mix_g[0]), r2(ln_mix_b[0]))
    hs = moe(hs, 0)
    pool_t = jnp.transpose(state_pool[0], (1, 0, 2))
    hs, new_pool = _decode_pool(hs, pool_t, pool_w_bf, pool_b[0], ps,
                                r2(ln_mix_g[1]), r2(ln_mix_b[1]))
    y_sample = moe(hs, 1).reshape(nseq, 1, d)
    s_conv = jnp.transpose(new_conv, (1, 0, 2))[None]
    s_pool = jnp.transpose(new_pool, (1, 0, 2))[None]

    return (y_prompt, y_sample, p_attn[0], p_attn[1], p_attn[2], p_conv, p_pool,
            s_attn[0], s_attn[1], s_attn[2], s_conv, s_pool)
```

```python
import functools

import jax
import jax.numpy as jnp
from jax import lax
from jax.experimental import pallas as pl
from jax.experimental.pallas import tpu as pltpu

F32 = jnp.float32
BF16 = jnp.bfloat16

D_MODEL = 1024
HEAD_DIM = 64
HEADS = 8
ATTN_WIDTH = HEADS * HEAD_DIM
N_DIL = 3
DIL_WINDOWS = (128, 512, 2048)
DIL_RATES = (1, 4, 16)
ATTN_BLOCK = 128
ATTN_SCALE = HEAD_DIM ** -0.5
ROT_DIM = HEAD_DIM // 4
ROT_HALF = ROT_DIM // 2
ROPE_THETA = 500000.0
QKV_COLS = 3 * N_DIL * ATTN_WIDTH
CONV_CH = D_MODEL // 4
CONV_WIDTH = 31
IN_COLS = QKV_COLS + 2 * CONV_CH
POOL_WINDOWS = (2, 4, 8, 16)
POOL_CH = D_MODEL // len(POOL_WINDOWS)
POOL_PREFIX = max(POOL_WINDOWS) - 1
N_EXPERTS = 16
N_EXPERT_GROUPS = 4
EXPERTS_PER_GROUP = 4
D_EXPERT = 256
DEPTH = 2
DN_ALPHA = (2.0 * DEPTH) ** 0.25
LN_EPS = 1e-5

LANES = 128
SUBLANES = 8
VMEM_LIMIT_BYTES = 56 * 1024 * 1024
ROW_TILE = 512
CONV_HALO = 32
CONV_SLACK = 16
POOL_HALO = 32
STREAM_VMEM_LIMIT_BYTES = 60000 * 1024
MOE_SUBSTEPS = 4
EXPERTS_PER_SUBSTEP = N_EXPERTS // MOE_SUBSTEPS
MOE_FIXED_INPUTS = 9
CACHE_HEAD_BLOCK = 4
HEAD_PAIRS = ATTN_WIDTH // LANES
ATTN_UNITS_PER_STAGE = 8

NT_DIMS = (((1,), (1,)), ((), ()))


def _params(sem):
    return pltpu.CompilerParams(dimension_semantics=sem, vmem_limit_bytes=VMEM_LIMIT_BYTES)


def _const_spec(shape):
    nd = len(shape)
    return pl.BlockSpec(shape, lambda *_: (0,) * nd)


def _resident_spec(shape):
    nd = len(shape)
    return pl.BlockSpec(shape, lambda *_: (0,) * nd, pipeline_mode=pl.Buffered(1))


def _layer_norm(x, g, b):
    mu = jnp.mean(x, axis=-1, keepdims=True)
    xc = x - mu
    var = jnp.mean(xc * xc, axis=-1, keepdims=True)
    return xc * lax.rsqrt(var + LN_EPS) * g + b


def _proj_kernel(x_ref, w_ref, b_ref, tab_ref, q0_ref, q1_ref, q2_ref, kv0_ref, kv1_ref, kv2_ref,
                 glu_ref, kt0_ref, kt1_ref, kt2_ref):
    tm = x_ref.shape[0]
    last = pl.program_id(1) == pl.num_programs(1) - 1
    xb = x_ref[...].astype(BF16)
    cosm, sin_lo, sin_hi = tab_ref[0], tab_ref[1], tab_ref[2]

    def proj(c0, width):
        return (jnp.dot(xb, w_ref[:, c0:c0 + width], preferred_element_type=F32)
                + b_ref[:, c0:c0 + width])

    def rope(t):
        parts = []
        for j in range(t.shape[1] // LANES):
            v = t[:, j * LANES:(j + 1) * LANES]
            parts.append(v * cosm
                         + pltpu.roll(v, LANES - ROT_HALF, 1) * sin_lo
                         + pltpu.roll(v, ROT_HALF, 1) * sin_hi)
        return jnp.concatenate(parts, axis=1)

    q_refs = (q0_ref, q1_ref, q2_ref)
    kv_refs = (kv0_ref, kv1_ref, kv2_ref)
    for g in range(N_DIL):
        c = g * ATTN_WIDTH
        q = rope(proj(c, ATTN_WIDTH)) * ATTN_SCALE
        k = rope(proj(N_DIL * ATTN_WIDTH + c, ATTN_WIDTH))
        v = proj(2 * N_DIL * ATTN_WIDTH + c, ATTN_WIDTH)
        for hp in range(HEAD_PAIRS):
            cols = slice(hp * LANES, (hp + 1) * LANES)
            q_refs[g][hp] = q[:, cols]
            kv_refs[g][hp] = k[:, cols]
            kv_refs[g][HEAD_PAIRS + hp] = v[:, cols]
        if g == 2:
            kt2_ref[0, 0:ATTN_WIDTH, :] = k.T
            kt2_ref[0, ATTN_WIDTH:2 * ATTN_WIDTH, :] = v.T
    ga = proj(QKV_COLS, CONV_CH)
    gb = proj(QKV_COLS + CONV_CH, CONV_CH)
    glu_ref[...] = ga * jax.nn.sigmoid(gb)

    @pl.when(last)
    def _():
        for kt_ref, kv_ref, keep in ((kt0_ref, kv0_ref, DIL_WINDOWS[0]), (kt1_ref, kv1_ref, DIL_WINDOWS[1])):
            for j in range(2 * HEAD_PAIRS):
                kt_ref[0, j * LANES:(j + 1) * LANES, :] = kv_ref[j, tm - keep:, :].T


def _project(x2d, w_bf, b2d, tables, n, s, tm):
    assert tm == DIL_WINDOWS[1] and tm >= DIL_WINDOWS[0]
    tps = s // tm
    m = n * s
    row = lambda b, t: (b * tps + t, 0)
    seq = lambda b, t: (b, 0, 0)
    kv_rows = 2 * ATTN_WIDTH
    slab = lambda b, t: (0, b * tps + t, 0)
    out_specs = ([pl.BlockSpec((HEAD_PAIRS, tm, LANES), slab)] * N_DIL
                 + [pl.BlockSpec((2 * HEAD_PAIRS, tm, LANES), slab)] * N_DIL
                 + [pl.BlockSpec((tm, CONV_CH), row),
                    pl.BlockSpec((1, kv_rows, DIL_WINDOWS[0]), seq),
                    pl.BlockSpec((1, kv_rows, DIL_WINDOWS[1]), seq),
                    pl.BlockSpec((1, kv_rows, tm), lambda b, t: (b, 0, t))])
    out_shape = ([jax.ShapeDtypeStruct((HEAD_PAIRS, m, LANES), F32)] * N_DIL
                 + [jax.ShapeDtypeStruct((2 * HEAD_PAIRS, m, LANES), F32)] * N_DIL
                 + [jax.ShapeDtypeStruct((m, CONV_CH), F32),
                    jax.ShapeDtypeStruct((n, kv_rows, DIL_WINDOWS[0]), F32),
                    jax.ShapeDtypeStruct((n, kv_rows, DIL_WINDOWS[1]), F32),
                    jax.ShapeDtypeStruct((n, kv_rows, s), F32)])
    return pl.pallas_call(
        _proj_kernel,
        grid=(n, tps),
        in_specs=[
            pl.BlockSpec((tm, D_MODEL), row),
            _resident_spec((D_MODEL, IN_COLS)),
            _const_spec((1, IN_COLS)),
            pl.BlockSpec((3, tm, LANES), lambda b, t: (0, t, 0)),
        ],
        out_specs=out_specs,
        out_shape=out_shape,
        compiler_params=_params(("parallel", "arbitrary")),
        name="in_proj",
    )(x2d, w_bf, b2d, tables)


def _rope_angles(pos):
    inv_freq = ROPE_THETA ** (-jnp.arange(ROT_HALF, dtype=F32) * 2.0 / ROT_DIM)
    return pos.astype(F32)[:, None] * inv_freq[None, :]


def _rope_tables(pos):
    t = pos.shape[0]
    ang = _rope_angles(pos)
    cos, sin = jnp.cos(ang), jnp.sin(ang)
    rest = HEAD_DIM - ROT_DIM
    c64 = jnp.concatenate([cos, cos, jnp.ones((t, rest), F32)], axis=1)
    lo64 = jnp.concatenate([-sin, jnp.zeros((t, HEAD_DIM - ROT_HALF), F32)], axis=1)
    hi64 = jnp.concatenate([jnp.zeros((t, ROT_HALF), F32), sin, jnp.zeros((t, rest), F32)], axis=1)
    rep = LANES // HEAD_DIM
    return jnp.stack([jnp.tile(c64, (1, rep)), jnp.tile(lo64, (1, rep)), jnp.tile(hi64, (1, rep))])


def _attn_kernel(q_ref, kv_ref, bias_ref, o_ref, lse_ref, s_ref, p_ref, *, rate, nblk):
    nk = s_ref.shape[3]
    lane = lax.broadcasted_iota(jnp.int32, (ATTN_BLOCK, LANES), 1)
    low_half = lane < HEAD_DIM
    keep_lo = low_half.astype(BF16)
    keep_hi = 1 - keep_lo
    ones_rhs = jnp.ones((nk, LANES), BF16)

    def strided(start, size):
        return pl.ds(start, size, stride=rate) if rate > 1 else pl.ds(start, size)

    def unit_rows(u):
        res, b = (u // nblk, u % nblk) if nblk > 1 else (u, 0)
        rows_q = strided(b * (ATTN_BLOCK * rate) + res, ATTN_BLOCK)
        rows_k = strided(jnp.maximum(b - 1, 0) * (ATTN_BLOCK * rate) + res, nk)
        return rows_q, rows_k, bias_ref[jnp.minimum(b, 1)]

    def scores(slot, rows_q, rows_k, bias):
        for hp in range(HEAD_PAIRS):
            qp = q_ref[hp, rows_q, :].astype(BF16)
            kp = kv_ref[hp, rows_k, :].astype(BF16)
            for half, keep in enumerate((keep_lo, keep_hi)):
                s = lax.dot_general(qp * keep, kp, NT_DIMS, preferred_element_type=F32)
                s_ref[slot, 2 * hp + half] = s + bias

    def values(slot, rows_q, rows_k, m):
        for hp in range(HEAD_PAIRS):
            vp = kv_ref[HEAD_PAIRS + hp, rows_k, :].astype(BF16)
            rhs = jnp.concatenate([vp, ones_rhs], axis=1)
            ol_lo = jnp.dot(p_ref[slot, 2 * hp], rhs, preferred_element_type=F32)
            ol_hi = jnp.dot(p_ref[slot, 2 * hp + 1], rhs, preferred_element_type=F32)
            l_lo, l_hi = ol_lo[:, LANES:], ol_hi[:, LANES:]
            o_pair = jnp.where(low_half, ol_lo[:, :LANES] * (1.0 / l_lo), ol_hi[:, :LANES] * (1.0 / l_hi))
            lse_pair = jnp.where(low_half, m[2 * hp] + jnp.log(l_lo), m[2 * hp + 1] + jnp.log(l_hi))
            o_ref[hp, rows_q, :] = o_pair
            lse_ref[hp, rows_q, :] = lse_pair

    nslot = s_ref.shape[0]

    def unit_group(i, carry):
        units = [unit_rows(nslot * i + j) for j in range(nslot)]
        for j, u in enumerate(units):
            scores(j, *u)
        sc = s_ref[...]
        m = jnp.max(sc, axis=-1, keepdims=True)
        p_ref[...] = jnp.exp(sc - m).astype(BF16)
        for j, u in enumerate(units):
            values(j, u[0], u[1], m[j])
        return carry

    lax.fori_loop(0, rate * nblk // nslot, unit_group, 0)


def _band_bias(nblk):
    qi = jnp.arange(ATTN_BLOCK)[:, None]
    ci = jnp.arange(ATTN_BLOCK)[None, :]
    causal = jnp.where(ci <= qi, 0.0, -jnp.inf).astype(F32)
    if nblk == 1:
        return jnp.stack([causal, causal])
    band = jnp.where(ci >= qi, 0.0, -jnp.inf).astype(F32)
    closed = jnp.full((ATTN_BLOCK, ATTN_BLOCK), -jnp.inf, F32)
    return jnp.stack([jnp.concatenate([causal, closed], axis=1), jnp.concatenate([band, causal], axis=1)])


def _prompt_attention(q, kv, g, n, s):
    rate = DIL_RATES[g]
    nblk = s // rate // ATTN_BLOCK
    nk = 2 * ATTN_BLOCK if nblk > 1 else ATTN_BLOCK
    out_sds = jax.ShapeDtypeStruct((HEAD_PAIRS, n * s, LANES), F32)
    seq = lambda b: (0, b, 0)
    o_spec = pl.BlockSpec((HEAD_PAIRS, s, LANES), seq)
    return pl.pallas_call(
        functools.partial(_attn_kernel, rate=rate, nblk=nblk),
        grid=(n,),
        in_specs=[
            o_spec,
            pl.BlockSpec((2 * HEAD_PAIRS, s, LANES), seq),
            _const_spec((2, ATTN_BLOCK, nk)),
        ],
        out_specs=[o_spec, o_spec],
        out_shape=[out_sds, out_sds],
        scratch_shapes=[pltpu.VMEM((ATTN_UNITS_PER_STAGE, HEADS, ATTN_BLOCK, nk), F32),
                        pltpu.VMEM((ATTN_UNITS_PER_STAGE, HEADS, ATTN_BLOCK, nk), BF16)],
        compiler_params=_params(("parallel",)),
        name=f"band_attn_g{g}",
    )(q, kv, _band_bias(nblk))


def _group_weights(lses):
    lmax = jnp.maximum(jnp.maximum(lses[0], lses[1]), lses[2])
    es = [jnp.exp(l - lmax) for l in lses]
    inv = 1.0 / (es[0] + es[1] + es[2])
    return [e * inv for e in es]


def _conv_tail(y, cb, clg, clb):
    z = _layer_norm(y + cb, clg, clb)
    return z * jax.nn.sigmoid(z)


def _out_proj_ln(x, attn, conv, wo_ref, lng, lnb):
    y = (jnp.dot(attn.astype(BF16), wo_ref[0:ATTN_WIDTH, :], preferred_element_type=F32)
         + jnp.dot(conv.astype(BF16), wo_ref[ATTN_WIDTH:ATTN_WIDTH + CONV_CH, :],
                   preferred_element_type=F32))
    return _layer_norm(DN_ALPHA * x + y, lng, lnb)


def _mix_kernel(x_ref, o0_ref, o1_ref, o2_ref, l0_ref, l1_ref, l2_ref, gc_ref, gp_ref,
                cw_ref, cb_ref, clg_ref, clb_ref, wo_ref, lng_ref, lnb_ref, h_ref, ext_ref, phase_ref):
    tm = x_ref.shape[0]
    first = pl.program_id(1) == 0
    ext_ref[0:CONV_HALO, :] = jnp.where(first, 0.0, gp_ref[...])
    ext_ref[CONV_HALO:CONV_HALO + tm, :] = gc_ref[...]
    ext_ref[CONV_HALO + tm:, :] = jnp.zeros((CONV_SLACK, CONV_CH), F32)
    base = CONV_HALO - (CONV_WIDTH - 1)
    span = tm + CONV_SLACK
    acc = None
    for r in range(SUBLANES):
        part = None
        for k in range(r, CONV_WIDTH, SUBLANES):
            term = cw_ref[k:k + 1, :] * ext_ref[k - r:k - r + span, :]
            part = term if part is None else part + term
        phase_ref[r] = part
        shifted = phase_ref[r, base + r:base + r + tm, :]
        acc = shifted if acc is None else acc + shifted
    conv = _conv_tail(acc, cb_ref[...], clg_ref[...], clb_ref[...])
    pieces = []
    for hp in range(HEAD_PAIRS):
        w = _group_weights((l0_ref[hp], l1_ref[hp], l2_ref[hp]))
        pieces.append(w[0] * o0_ref[hp] + w[1] * o1_ref[hp] + w[2] * o2_ref[hp])
    attn = jnp.concatenate(pieces, axis=1)
    h_ref[...] = _out_proj_ln(x_ref[...], attn, conv, wo_ref, lng_ref[...], lnb_ref[...])


def _prompt_mix(x2d, os_, lses, glu, conv_w, conv_b, clg, clb, wo_bf, lng, lnb, n, s, tm):
    tps = s // tm
    hb = tm // CONV_HALO
    row = lambda b, t: (b * tps + t, 0)
    aw = pl.BlockSpec((HEAD_PAIRS, tm, LANES), lambda b, t: (0, b * tps + t, 0))
    return pl.pallas_call(
        _mix_kernel,
        grid=(n, tps),
        in_specs=[
            pl.BlockSpec((tm, D_MODEL), row),
            aw, aw, aw, aw, aw, aw,
            pl.BlockSpec((tm, CONV_CH), row),
            pl.BlockSpec((CONV_HALO, CONV_CH),
                         lambda b, t: (jnp.maximum((b * tps + t) * hb - 1, 0), 0)),
            _const_spec((CONV_WIDTH, CONV_CH)),
            _const_spec((1, CONV_CH)), _const_spec((1, CONV_CH)), _const_spec((1, CONV_CH)),
            _const_spec((ATTN_WIDTH + CONV_CH, D_MODEL)),
            _const_spec((1, D_MODEL)), _const_spec((1, D_MODEL)),
        ],
        out_specs=pl.BlockSpec((tm, D_MODEL), row),
        out_shape=jax.ShapeDtypeStruct((n * s, D_MODEL), F32),
        scratch_shapes=[pltpu.VMEM((CONV_HALO + tm + CONV_SLACK, CONV_CH), F32),
                        pltpu.VMEM((SUBLANES, tm + CONV_SLACK, CONV_CH), F32)],
        compiler_params=_params(("parallel", "parallel")),
        name="mix_out",
    )(x2d, *os_, *lses, glu, glu, conv_w, conv_b, clg, clb, wo_bf, lng, lnb)


def _routing_rows(logit_rows, bias_ref):
    m = logit_rows[0]
    for r in logit_rows[1:]:
        m = jnp.maximum(m, r)
    ex = [jnp.exp(r - m) for r in logit_rows]
    tot = ex[0]
    for e in ex[1:]:
        tot = tot + e
    scores = [e / tot for e in ex]
    sel = [scores[e] + bias_ref[e:e + 1, :] for e in range(N_EXPERTS)]
    grp = []
    for g in range(N_EXPERT_GROUPS):
        v = sel[g * EXPERTS_PER_GROUP:(g + 1) * EXPERTS_PER_GROUP]
        best = v[0] + v[1]
        for i in range(EXPERTS_PER_GROUP):
            for j in range(i + 1, EXPERTS_PER_GROUP):
                if (i, j) != (0, 1):
                    best = jnp.maximum(best, v[i] + v[j])
        grp.append(best)
    gmax = grp[0]
    for v in grp[1:]:
        gmax = jnp.maximum(gmax, v)
    taken = None
    in_group = []
    for g in range(N_EXPERT_GROUPS):
        hit = grp[g] == gmax
        if taken is None:
            in_group.append(hit)
            taken = hit
        else:
            in_group.append(jnp.logical_and(hit, jnp.logical_not(taken)))
            taken = jnp.logical_or(taken, hit)
    gates = []
    for e in range(N_EXPERTS):
        g = e // EXPERTS_PER_GROUP
        rank = jnp.zeros_like(sel[e])
        for o in range(g * EXPERTS_PER_GROUP, (g + 1) * EXPERTS_PER_GROUP):
            if o == e:
                continue
            ahead = sel[o] > sel[e]
            if o < e:
                ahead = jnp.logical_or(ahead, sel[o] == sel[e])
            rank = rank + ahead.astype(F32)
        chosen = jnp.logical_and(in_group[g], rank < float(2))
        gates.append(jnp.where(chosen, scores[e], 0.0))
    den = gates[0]
    for v in gates[1:]:
        den = den + v
    return [v / den for v in gates]


def _pool_project(parts, pw_ref, pb_ref, ps_ref):
    cols = []
    for gi in range(len(POOL_WINDOWS)):
        lo = gi * POOL_CH
        y = jnp.dot(parts[gi].astype(BF16), pw_ref[gi], preferred_element_type=F32)
        cols.append((y + pb_ref[gi:gi + 1, :]) * ps_ref[:, lo:lo + POOL_CH])
    return jnp.concatenate(cols, axis=1)


def _pool_kernel(x_ref, xp_ref, pw_ref, pb_ref, ps_ref, lng_ref, lnb_ref, out_ref, ext_ref, lv_ref):
    tm = x_ref.shape[0]
    t = pl.program_id(1)
    ext_ref[0:POOL_HALO, :] = jnp.where(t == 0, 0.0, xp_ref[...])
    ext_ref[POOL_HALO:POOL_HALO + tm, :] = x_ref[...]
    end = POOL_HALO + tm
    src = ext_ref
    for i, shift in enumerate((1, 2, 4)):
        r0, c0 = SUBLANES * (i + 1), i * POOL_CH
        lv_ref[i, r0:end, c0:] = src[r0:end, c0:] + src[r0 - shift:end - shift, c0:]
        src = lv_ref.at[i]
    pos = t * tm + lax.broadcasted_iota(jnp.int32, (tm, 1), 0)
    parts = []
    for gi, w in enumerate(POOL_WINDOWS):
        lo = gi * POOL_CH
        cur = x_ref[:, lo:lo + POOL_CH]
        if gi < 3:
            tot = lv_ref[gi, POOL_HALO:end, lo:lo + POOL_CH]
        else:
            tot = (lv_ref[2, POOL_HALO:end, lo:lo + POOL_CH]
                   + lv_ref[2, POOL_HALO - SUBLANES:end - SUBLANES, lo:lo + POOL_CH])
        cnt = jnp.minimum(w, pos + 1).astype(F32)
        parts.append(tot / cnt - cur)
    y = _pool_project(parts, pw_ref, pb_ref, ps_ref)
    out_ref[...] = _layer_norm(DN_ALPHA * x_ref[...] + y, lng_ref[...], lnb_ref[...])


def _prompt_pool(h2d, pw_bf, pb, ps, lng, lnb, n, s, tm):
    tps = s // tm
    hb = tm // POOL_HALO
    row = lambda b, t: (b * tps + t, 0)
    ng = len(POOL_WINDOWS)
    return pl.pallas_call(
        _pool_kernel,
        grid=(n, tps),
        in_specs=[
            pl.BlockSpec((tm, D_MODEL), row),
            pl.BlockSpec((POOL_HALO, D_MODEL),
                         lambda b, t: (jnp.maximum((b * tps + t) * hb - 1, 0), 0)),
            _const_spec((ng, POOL_CH, POOL_CH)),
            _const_spec((ng, POOL_CH)),
            _const_spec((1, D_MODEL)), _const_spec((1, D_MODEL)), _const_spec((1, D_MODEL)),
        ],
        out_specs=pl.BlockSpec((tm, D_MODEL), row),
        out_shape=jax.ShapeDtypeStruct((n * s, D_MODEL), F32),
        scratch_shapes=[pltpu.VMEM((POOL_HALO + tm, D_MODEL), F32),
                        pltpu.VMEM((3, POOL_HALO + tm, D_MODEL), F32)],
        compiler_params=_params(("parallel", "parallel")),
        name="pool_mix",
    )(h2d, h2d, pw_bf, pb, ps, lng, lnb)


def _dec_proj_kernel(x_ref, w_ref, b_ref, cs_ref, qkvt_ref, glu_ref):
    xb = x_ref[...].astype(BF16)
    cos, sin = cs_ref[0], cs_ref[1]
    chunk = ATTN_WIDTH

    def proj(c0, width):
        return (jnp.dot(xb, w_ref[:, c0:c0 + width], preferred_element_type=F32)
                + b_ref[:, c0:c0 + width])

    for ci in range(QKV_COLS // chunk):
        pt = proj(ci * chunk, chunk).T
        if ci < 2 * N_DIL:
            pieces = []
            for hh in range(HEADS):
                base = hh * HEAD_DIM
                x1 = pt[base:base + ROT_HALF, :]
                x2 = pt[base + ROT_HALF:base + ROT_DIM, :]
                pieces += [x1 * cos - x2 * sin, x2 * cos + x1 * sin, pt[base + ROT_DIM:base + HEAD_DIM, :]]
            pt = jnp.concatenate(pieces, axis=0)
            if ci < N_DIL:
                pt = pt * ATTN_SCALE
        qkvt_ref[ci * chunk:(ci + 1) * chunk, :] = pt
    ga = proj(QKV_COLS, CONV_CH)
    gb = proj(QKV_COLS + CONV_CH, CONV_CH)
    glu_ref[...] = ga * jax.nn.sigmoid(gb)


def _decode_project(x2d, w_bf, b2d, cs):
    nseq = x2d.shape[0]
    args = (x2d, w_bf, b2d, cs)
    return pl.pallas_call(
        _dec_proj_kernel,
        grid=(1,),
        in_specs=[_const_spec(a.shape) for a in args],
        out_specs=[_const_spec((QKV_COLS, nseq)), _const_spec((nseq, CONV_CH))],
        out_shape=[jax.ShapeDtypeStruct((QKV_COLS, nseq), F32),
                   jax.ShapeDtypeStruct((nseq, CONV_CH), F32)],
        compiler_params=_params(("arbitrary",)),
        name="decode_proj",
    )(*args)


def _cache_block(c_ref, nc_ref, ot_ref, lt_ref, qkvt_ref, g, seq, head0):
    hb, w = c_ref.shape[2], c_ref.shape[4]
    nseq = qkvt_ref.shape[1]
    rate = DIL_RATES[g]
    rows = hb * HEAD_DIM
    r0 = pl.multiple_of(head0 * HEAD_DIM, rows)
    mine = lax.broadcasted_iota(jnp.int32, (rows, nseq), 1) == seq

    def column(base):
        x = qkvt_ref[pl.ds(base + r0, rows), :]
        return jnp.sum(jnp.where(mine, x, 0.0), axis=1, keepdims=True).reshape(hb, HEAD_DIM, 1)

    qc = column(g * ATTN_WIDTH)
    kc = column((N_DIL + g) * ATTN_WIDTH)
    vc = column((2 * N_DIL + g) * ATTN_WIDTH)
    kt = c_ref[0, 0]
    vt = c_ref[0, 1]
    tok = lax.broadcasted_iota(jnp.int32, (1, 1, w), 2)
    in_window = (tok & (rate - 1)) == 0
    newest = tok == w - 1

    sc = jnp.where(in_window, jnp.sum(kt * qc, axis=1, keepdims=True), -jnp.inf)
    sn = jnp.sum(qc * kc, axis=1, keepdims=True)
    m = jnp.maximum(jnp.max(sc, axis=2, keepdims=True), sn)
    p = jnp.exp(sc - m)
    pn = jnp.exp(sn - m)
    l = jnp.sum(p, axis=2, keepdims=True) + pn
    inv = 1.0 / l
    o = jnp.sum(vt * (p * inv), axis=2, keepdims=True) + vc * (pn * inv)
    lse = jnp.broadcast_to(m + jnp.log(l), (hb, HEAD_DIM, 1))
    acc_rows = pl.ds(r0, rows)
    ot_ref[acc_rows, :] = jnp.where(mine, o.reshape(rows, 1), ot_ref[acc_rows, :])
    lt_ref[acc_rows, :] = jnp.where(mine, lse.reshape(rows, 1), lt_ref[acc_rows, :])

    def shifted(old, new_col):
        rolled = pltpu.roll(old.reshape(rows, w), w - 1, 1).reshape(hb, HEAD_DIM, w)
        return jnp.where(newest, new_col, rolled)

    nc_ref[0, 0] = shifted(kt, kc)
    nc_ref[0, 1] = shifted(vt, vc)


def _moe_stream_kernel(*refs, groups, seq_base, resident):
    ng = len(groups)
    (h_ref, rw_ref, rb_ref, w1_ref, w3_ref, w2_ref, lng_ref, lnb_ref, qkvt_ref) = refs[:MOE_FIXED_INPUTS]
    c_refs = refs[MOE_FIXED_INPUTS:MOE_FIXED_INPUTS + ng]
    n_in = MOE_FIXED_INPUTS + ng + (ng if seq_base else 0)
    out_ref = refs[n_in]
    stream_out = refs[n_in + 1:n_in + 1 + 3 * ng]
    comb_ref, ct_ref = refs[n_in + 1 + 3 * ng:]
    i, k = pl.program_id(0), pl.program_id(1)
    flat = i * MOE_SUBSTEPS + k
    hb = h_ref[...].astype(BF16)

    @pl.when(k == 0)
    def _():
        logits = jnp.dot(hb, rw_ref[...], preferred_element_type=F32)
        lt = logits.T
        comb_rows = _routing_rows([lt[e:e + 1, :] for e in range(N_EXPERTS)], rb_ref)
        ct_ref[...] = jnp.zeros_like(ct_ref)
        for e in range(N_EXPERTS):
            ct_ref[e:e + 1, :] = comb_rows[e]
        comb_ref[...] = ct_ref[...].T
        out_ref[...] = jnp.zeros_like(out_ref)

    @pl.when(flat == 0)
    def _():
        for s in range(ng):
            for t in range(2):
                acc = stream_out[3 * s + 1 + t]
                acc[...] = jnp.zeros_like(acc)

    comb = comb_ref[...]
    lane = lax.broadcasted_iota(jnp.int32, comb.shape, 1)
    ffn = None
    for ee in range(EXPERTS_PER_SUBSTEP):
        e = k * EXPERTS_PER_SUBSTEP + ee
        we = e if resident else ee
        a = jnp.dot(hb, w1_ref[we], preferred_element_type=F32)
        b = jnp.dot(hb, w3_ref[we], preferred_element_type=F32)
        gate = jnp.sum(jnp.where(lane == e, comb, 0.0), axis=1, keepdims=True)
        gated = (a * jax.nn.sigmoid(a)) * b * gate
        part = jnp.dot(gated.astype(BF16), w2_ref[ee], preferred_element_type=F32)
        ffn = part if ffn is None else ffn + part
    out_ref[...] += ffn

    blocks_per_seq = HEADS // CACHE_HEAD_BLOCK
    for s, g in enumerate(groups):
        _cache_block(c_refs[s], stream_out[3 * s], stream_out[3 * s + 1], stream_out[3 * s + 2], qkvt_ref,
                     g, seq_base + flat // blocks_per_seq, (flat % blocks_per_seq) * CACHE_HEAD_BLOCK)

    @pl.when(k == MOE_SUBSTEPS - 1)
    def _():
        out_ref[...] = _layer_norm(DN_ALPHA * h_ref[...] + out_ref[...], lng_ref[...], lnb_ref[...])


def _moe_stream(h2d, rw_pad, rb, w1, w3, w2, lng, lnb, qkvt, caches_t, tm, seq_base, prev):
    m = h2d.shape[0]
    nseq = qkvt.shape[1]
    groups = tuple(range(len(caches_t)))
    resident = bool(caches_t)
    blocks_per_seq = HEADS // CACHE_HEAD_BLOCK
    first_block = seq_base * blocks_per_seq
    if caches_t:
        assert len(caches_t) == N_DIL and ((m // tm) * MOE_SUBSTEPS) % blocks_per_seq == 0
        assert first_block + (m // tm) * MOE_SUBSTEPS <= nseq * blocks_per_seq
    row = lambda i, k: (i, 0)
    step_block = lambda i, k: (k, 0, 0)
    w13_spec = (_resident_spec((N_EXPERTS, D_MODEL, D_EXPERT)) if resident
                else pl.BlockSpec((EXPERTS_PER_SUBSTEP, D_MODEL, D_EXPERT), step_block))
    acc_spec = _const_spec((ATTN_WIDTH, nseq))
    acc_sds = jax.ShapeDtypeStruct((ATTN_WIDTH, nseq), F32)

    def block_index(i, k):
        b = first_block + i * MOE_SUBSTEPS + k
        return (b // blocks_per_seq, 0, b % blocks_per_seq, 0, 0)

    cache_specs, stream_specs, stream_shapes = [], [], []
    for c in caches_t:
        spec = pl.BlockSpec((1, 2, CACHE_HEAD_BLOCK, HEAD_DIM, c.shape[4]), block_index)
        cache_specs.append(spec)
        stream_specs += [spec, acc_spec, acc_spec]
        stream_shapes += [jax.ShapeDtypeStruct(c.shape, F32), acc_sds, acc_sds]
    extra_in, extra_specs, aliases = [], [], {}
    if seq_base:
        n_fixed = MOE_FIXED_INPUTS + len(caches_t)
        for s, nc in enumerate(prev):
            aliases[n_fixed + s] = 1 + 3 * s
            extra_in.append(nc)
            extra_specs.append(pl.BlockSpec(memory_space=pl.ANY))
    return pl.pallas_call(
        functools.partial(_moe_stream_kernel, groups=groups, seq_base=seq_base, resident=resident),
        grid=(m // tm, MOE_SUBSTEPS),
        in_specs=[
            pl.BlockSpec((tm, D_MODEL), row),
            _const_spec((D_MODEL, LANES)),
            _const_spec((N_EXPERTS, 1)),
            w13_spec,
            w13_spec,
            pl.BlockSpec((EXPERTS_PER_SUBSTEP, D_EXPERT, D_MODEL), step_block),
            _const_spec((1, D_MODEL)), _const_spec((1, D_MODEL)),
            _resident_spec(qkvt.shape),
        ] + cache_specs + extra_specs,
        out_specs=[pl.BlockSpec((tm, D_MODEL), row)] + stream_specs,
        out_shape=[jax.ShapeDtypeStruct((m, D_MODEL), F32)] + stream_shapes,
        input_output_aliases=aliases,
        scratch_shapes=[pltpu.VMEM((tm, LANES), F32), pltpu.VMEM((LANES, tm), F32)],
        compiler_params=pltpu.CompilerParams(dimension_semantics=("arbitrary", "arbitrary"),
                                             vmem_limit_bytes=STREAM_VMEM_LIMIT_BYTES),
        name=f"moe_stream_from{seq_base}" if caches_t else "moe_ffn",
    )(h2d, rw_pad, rb, w1, w3, w2, lng, lnb, qkvt, *caches_t, *extra_in)


def _dec_mix_kernel(x_ref, *refs, split):
    acc_refs, (st_ref, glu_ref, cw_ref, cb_ref, clg_ref, clb_ref, wo_ref, lng_ref, lnb_ref,
               h_ref, nst_ref) = refs[:4 * N_DIL], refs[4 * N_DIL:]
    npre = CONV_WIDTH - 1
    glu = glu_ref[...]
    acc = cw_ref[npre:npre + 1, :] * glu
    for k in range(npre):
        acc = acc + cw_ref[k:k + 1, :] * st_ref[k]
    conv = _conv_tail(acc, cb_ref[...], clg_ref[...], clb_ref[...])
    first = lax.broadcasted_iota(jnp.int32, (ATTN_WIDTH, x_ref.shape[0]), 1) < split

    def both(idx):
        return jnp.where(first, acc_refs[idx][...], acc_refs[2 * N_DIL + idx][...])

    os_ = [both(g) for g in range(N_DIL)]
    w = _group_weights([both(N_DIL + g) for g in range(N_DIL)])
    attn = (w[0] * os_[0] + w[1] * os_[1] + w[2] * os_[2]).T
    h_ref[...] = _out_proj_ln(x_ref[...], attn, conv, wo_ref, lng_ref[...], lnb_ref[...])
    nst_ref[0:npre - 1] = st_ref[1:npre]
    nst_ref[npre - 1] = glu


def _decode_mix(x2d, accs, split, state_t, glu, conv_w, conv_b, clg, clb, wo_bf, lng, lnb):
    nseq = x2d.shape[0]
    args = (x2d, *accs, state_t, glu, conv_w, conv_b, clg, clb, wo_bf, lng, lnb)
    return pl.pallas_call(
        functools.partial(_dec_mix_kernel, split=split),
        grid=(1,),
        in_specs=[_const_spec(a.shape) for a in args],
        out_specs=[_const_spec((nseq, D_MODEL)), _const_spec(state_t.shape)],
        out_shape=[jax.ShapeDtypeStruct((nseq, D_MODEL), F32),
                   jax.ShapeDtypeStruct(state_t.shape, F32)],
        compiler_params=_params(("arbitrary",)),
        name="decode_mix",
    )(*args)


def _dec_pool_kernel(x_ref, st_ref, pw_ref, pb_ref, ps_ref, lng_ref, lnb_ref, out_ref, nst_ref):
    x = x_ref[...]
    parts = []
    for gi, w in enumerate(POOL_WINDOWS):
        lo = gi * POOL_CH
        cur = x[:, lo:lo + POOL_CH]
        tot = cur
        for j in range(1, w):
            tot = tot + st_ref[POOL_PREFIX - j, :, lo:lo + POOL_CH]
        parts.append(tot / float(w) - cur)
    y = _pool_project(parts, pw_ref, pb_ref, ps_ref)
    out_ref[...] = _layer_norm(DN_ALPHA * x + y, lng_ref[...], lnb_ref[...])
    nst_ref[0:POOL_PREFIX - 1] = st_ref[1:POOL_PREFIX]
    nst_ref[POOL_PREFIX - 1] = x


def _decode_pool(h2d, state_t, pw_bf, pb, ps, lng, lnb):
    nseq = h2d.shape[0]
    args = (h2d, state_t, pw_bf, pb, ps, lng, lnb)
    return pl.pallas_call(
        _dec_pool_kernel,
        grid=(1,),
        in_specs=[_const_spec(a.shape) for a in args],
        out_specs=[_const_spec((nseq, D_MODEL)), _const_spec(state_t.shape)],
        out_shape=[jax.ShapeDtypeStruct((nseq, D_MODEL), F32),
                   jax.ShapeDtypeStruct(state_t.shape, F32)],
        compiler_params=_params(("arbitrary",)),
        name="decode_pool",
    )(*args)


def _token_minor(a):
    return jnp.transpose(a, (0, 2, 3, 4, 1))


def _token_major(a):
    return jnp.transpose(a, (0, 4, 1, 2, 3))


def kernel(x_prompt, x_sample, cache_attn_w128, cache_attn_w512, cache_attn_w2048, state_conv, state_pool,
           w_in, b_in, conv_w, conv_b, conv_ln_g, conv_ln_b, w_out, pool_w, pool_b, pool_scale,
           ln_mix_g, ln_mix_b, ln_ffn_g, ln_ffn_b, router_w, router_bias, moe_w1, moe_w3, moe_w2):
    n, s, d = x_prompt.shape
    nseq = x_sample.shape[0]
    past = cache_attn_w2048.shape[2]
    assert d == D_MODEL and x_sample.shape[1] == 1 and s % ROW_TILE == 0 and s == DIL_WINDOWS[2]
    assert cache_attn_w128.shape[0] == 1 and past == DIL_WINDOWS[2]
    caches = (cache_attn_w128[0], cache_attn_w512[0], cache_attn_w2048[0])

    w_in_bf = w_in[0].astype(BF16)
    w_out_bf = w_out[0].astype(BF16)
    pool_w_bf = pool_w[0].astype(BF16)
    moe_bf = [tuple(w[layer].astype(BF16) for w in (moe_w1, moe_w3, moe_w2)) for layer in range(DEPTH)]
    rw_pad = jnp.pad(router_w, ((0, 0), (0, LANES - N_EXPERTS))).astype(BF16)
    rb = router_bias.astype(F32).reshape(N_EXPERTS, 1)
    r2 = lambda v: v.reshape(1, -1)
    b_in2 = r2(b_in[0])
    cb, clg, clb = r2(conv_b[0]), r2(conv_ln_g[0]), r2(conv_ln_b[0])
    ps = r2(pool_scale[0])

    xs = x_sample.reshape(nseq, d)
    ang = _rope_angles(past + jnp.arange(1, dtype=jnp.int32))
    cs = jnp.stack([jnp.broadcast_to(jnp.cos(ang).T, (ROT_HALF, nseq)),
                    jnp.broadcast_to(jnp.sin(ang).T, (ROT_HALF, nseq))])
    qkvt, glus = _decode_project(xs, w_in_bf, b_in2, cs)
    caches_t = [_token_minor(c) for c in caches]

    seqs_per_layer = (n * s // ROW_TILE) * MOE_SUBSTEPS * CACHE_HEAD_BLOCK // HEADS
    assert DEPTH * seqs_per_layer == nseq

    def moe_stream(h2d, layer, prev):
        res = _moe_stream(h2d, rw_pad, rb, *moe_bf[layer], r2(ln_ffn_g[layer]), r2(ln_ffn_b[layer]),
                          qkvt, caches_t, ROW_TILE, layer * seqs_per_layer, prev)
        return res[0], [tuple(res[1 + 3 * g:4 + 3 * g]) for g in range(N_DIL)]

    def moe(h2d, layer):
        return _moe_stream(h2d, rw_pad, rb, *moe_bf[layer], r2(ln_ffn_g[layer]), r2(ln_ffn_b[layer]),
                           qkvt, [], h2d.shape[0], 0, None)[0]

    xp = x_prompt.reshape(n * s, d)
    tabs_p = _rope_tables(jnp.arange(s, dtype=jnp.int32))
    (q0, q1, q2, kv0, kv1, kv2, glu, kt0, kt1, kt2) = _project(xp, w_in_bf, b_in2, tabs_p, n, s, ROW_TILE)
    os_, lses = [], []
    for g, (q, kv) in enumerate(((q0, kv0), (q1, kv1), (q2, kv2))):
        o, l = _prompt_attention(q, kv, g, n, s)
        os_.append(o)
        lses.append(l)
    h = _prompt_mix(xp, os_, lses, glu, conv_w[0], cb, clg, clb, w_out_bf,
                    r2(ln_mix_g[0]), r2(ln_mix_b[0]), n, s, ROW_TILE)
    h, first_pass = moe_stream(h, 0, None)
    p_pool = h.reshape(n, s, d)[:, s - POOL_PREFIX:][None]
    h = _prompt_pool(h, pool_w_bf, pool_b[0], ps, r2(ln_mix_g[1]), r2(ln_mix_b[1]), n, s, ROW_TILE)
    h, second_pass = moe_stream(h, 1, [st[0] for st in first_pass])
    y_prompt = h.reshape(n, s, d)
    p_attn = [_token_major(kt.reshape(n, 2, HEADS, HEAD_DIM, kt.shape[2]))[None] for kt in (kt0, kt1, kt2)]
    p_conv = glu.reshape(n, s, CONV_CH)[:, s - (CONV_WIDTH - 1):][None]

    accs = []
    for streamed in (first_pass, second_pass):
        accs += [st[1] for st in streamed] + [st[2] for st in streamed]
    s_attn = [_token_major(st[0])[None] for st in second_pass]
    conv_t = jnp.transpose(state_conv[0], (1, 0, 2))
    hs, new_conv = _decode_mix(xs, accs, seqs_per_layer, conv_t, glus, conv_w[0], cb, clg, clb, w_out_bf,
                               r2(ln_mix_g[0]), r2(ln_mix_b[0]))
    hs = moe(hs, 0)
    pool_t = jnp.transpose(state_pool[0], (1, 0, 2))
    hs, new_pool = _decode_pool(hs, pool_t, pool_w_bf, pool_b[0], ps,
                                r2(ln_mix_g[1]), r2(ln_mix_b[1]))
    y_sample = moe(hs, 1).reshape(nseq, 1, d)
    s_conv = jnp.transpose(new_conv, (1, 0, 2))[None]
    s_pool = jnp.transpose(new_pool, (1, 0, 2))[None]

    return (y_prompt, y_sample, p_attn[0], p_attn[1], p_attn[2], p_conv, p_pool,
            s_attn[0], s_attn[1], s_attn[2], s_conv, s_pool)
```

```python
import functools

import jax
import jax.numpy as jnp
from jax import lax
from jax.experimental import pallas as pl
from jax.experimental.pallas import tpu as pltpu

F32 = jnp.float32
BF16 = jnp.bfloat16

D_MODEL = 1024
HEAD_DIM = 64
HEADS = 8
ATTN_WIDTH = HEADS * HEAD_DIM
N_DIL = 3
DIL_WINDOWS = (128, 512, 2048)
DIL_RATES = (1, 4, 16)
ATTN_BLOCK = 128
ATTN_SCALE = HEAD_DIM ** -0.5
ROT_DIM = HEAD_DIM // 4
ROT_HALF = ROT_DIM // 2
ROPE_THETA = 500000.0
QKV_COLS = 3 * N_DIL * ATTN_WIDTH
CONV_CH = D_MODEL // 4
CONV_WIDTH = 31
IN_COLS = QKV_COLS + 2 * CONV_CH
POOL_WINDOWS = (2, 4, 8, 16)
POOL_CH = D_MODEL // len(POOL_WINDOWS)
POOL_PREFIX = max(POOL_WINDOWS) - 1
N_EXPERTS = 16
N_EXPERT_GROUPS = 4
EXPERTS_PER_GROUP = 4
D_EXPERT = 256
DEPTH = 2
DN_ALPHA = (2.0 * DEPTH) ** 0.25
LN_EPS = 1e-5

LANES = 128
SUBLANES = 8
VMEM_LIMIT_BYTES = 56 * 1024 * 1024
ROW_TILE = 512
CONV_HALO = 32
CONV_SLACK = -(-(CONV_HALO - (CONV_WIDTH - 1) + SUBLANES) // SUBLANES) * SUBLANES
POOL_LEVELS = len(POOL_WINDOWS) - 1
POOL_HALO = SUBLANES * (POOL_LEVELS + 1)
assert POOL_WINDOWS[-1] // 2 == SUBLANES and POOL_HALO >= POOL_PREFIX
STREAM_VMEM_LIMIT_BYTES = 60000 * 1024
MOE_SUBSTEPS = 4
EXPERTS_PER_SUBSTEP = N_EXPERTS // MOE_SUBSTEPS
MOE_FIXED_INPUTS = 9
CACHE_HEAD_BLOCK = 4
HEAD_PAIRS = ATTN_WIDTH // LANES
ATTN_UNITS_PER_STAGE = 8

NT_DIMS = (((1,), (1,)), ((), ()))


def _params(sem):
    return pltpu.CompilerParams(dimension_semantics=sem, vmem_limit_bytes=VMEM_LIMIT_BYTES)


def _const_spec(shape):
    nd = len(shape)
    return pl.BlockSpec(shape, lambda *_: (0,) * nd)


def _resident_spec(shape):
    nd = len(shape)
    return pl.BlockSpec(shape, lambda *_: (0,) * nd, pipeline_mode=pl.Buffered(1))


def _layer_norm(x, g, b):
    mu = jnp.mean(x, axis=-1, keepdims=True)
    xc = x - mu
    var = jnp.mean(xc * xc, axis=-1, keepdims=True)
    return xc * lax.rsqrt(var + LN_EPS) * g + b


def _proj_kernel(x_ref, w_ref, b_ref, tab_ref, q0_ref, q1_ref, q2_ref, kv0_ref, kv1_ref, kv2_ref,
                 glu_ref, kt0_ref, kt1_ref, kt2_ref):
    tm = x_ref.shape[0]
    last = pl.program_id(1) == pl.num_programs(1) - 1
    xb = x_ref[...].astype(BF16)
    cosm, sin_lo, sin_hi = tab_ref[0], tab_ref[1], tab_ref[2]

    def proj(c0, width):
        return (jnp.dot(xb, w_ref[:, c0:c0 + width], preferred_element_type=F32)
                + b_ref[:, c0:c0 + width])

    def rope(t):
        parts = []
        for j in range(t.shape[1] // LANES):
            v = t[:, j * LANES:(j + 1) * LANES]
            parts.append(v * cosm
                         + pltpu.roll(v, LANES - ROT_HALF, 1) * sin_lo
                         + pltpu.roll(v, ROT_HALF, 1) * sin_hi)
        return jnp.concatenate(parts, axis=1)

    q_refs = (q0_ref, q1_ref, q2_ref)
    kv_refs = (kv0_ref, kv1_ref, kv2_ref)
    for g in range(N_DIL):
        c = g * ATTN_WIDTH
        q = rope(proj(c, ATTN_WIDTH)) * ATTN_SCALE
        k = rope(proj(N_DIL * ATTN_WIDTH + c, ATTN_WIDTH))
        v = proj(2 * N_DIL * ATTN_WIDTH + c, ATTN_WIDTH)
        for hp in range(HEAD_PAIRS):
            cols = slice(hp * LANES, (hp + 1) * LANES)
            q_refs[g][hp] = q[:, cols]
            kv_refs[g][hp] = k[:, cols]
            kv_refs[g][HEAD_PAIRS + hp] = v[:, cols]
        if g == 2:
            kt2_ref[0, 0:ATTN_WIDTH, :] = k.T
            kt2_ref[0, ATTN_WIDTH:2 * ATTN_WIDTH, :] = v.T
    ga = proj(QKV_COLS, CONV_CH)
    gb = proj(QKV_COLS + CONV_CH, CONV_CH)
    glu_ref[...] = ga * jax.nn.sigmoid(gb)

    @pl.when(last)
    def _():
        for kt_ref, kv_ref, keep in ((kt0_ref, kv0_ref, DIL_WINDOWS[0]), (kt1_ref, kv1_ref, DIL_WINDOWS[1])):
            for j in range(2 * HEAD_PAIRS):
                kt_ref[0, j * LANES:(j + 1) * LANES, :] = kv_ref[j, tm - keep:, :].T


def _project(x2d, w_bf, b2d, tables, n, s, tm):
    assert tm == DIL_WINDOWS[1] and tm >= DIL_WINDOWS[0]
    tps = s // tm
    m = n * s
    row = lambda b, t: (b * tps + t, 0)
    seq = lambda b, t: (b, 0, 0)
    kv_rows = 2 * ATTN_WIDTH
    slab = lambda b, t: (0, b * tps + t, 0)
    out_specs = ([pl.BlockSpec((HEAD_PAIRS, tm, LANES), slab)] * N_DIL
                 + [pl.BlockSpec((2 * HEAD_PAIRS, tm, LANES), slab)] * N_DIL
                 + [pl.BlockSpec((tm, CONV_CH), row),
                    pl.BlockSpec((1, kv_rows, DIL_WINDOWS[0]), seq),
                    pl.BlockSpec((1, kv_rows, DIL_WINDOWS[1]), seq),
                    pl.BlockSpec((1, kv_rows, tm), lambda b, t: (b, 0, t))])
    out_shape = ([jax.ShapeDtypeStruct((HEAD_PAIRS, m, LANES), F32)] * N_DIL
                 + [jax.ShapeDtypeStruct((2 * HEAD_PAIRS, m, LANES), F32)] * N_DIL
                 + [jax.ShapeDtypeStruct((m, CONV_CH), F32),
                    jax.ShapeDtypeStruct((n, kv_rows, DIL_WINDOWS[0]), F32),
                    jax.ShapeDtypeStruct((n, kv_rows, DIL_WINDOWS[1]), F32),
                    jax.ShapeDtypeStruct((n, kv_rows, s), F32)])
    return pl.pallas_call(
        _proj_kernel,
        grid=(n, tps),
        in_specs=[
            pl.BlockSpec((tm, D_MODEL), row),
            _resident_spec((D_MODEL, IN_COLS)),
            _const_spec((1, IN_COLS)),
            pl.BlockSpec((3, tm, LANES), lambda b, t: (0, t, 0)),
        ],
        out_specs=out_specs,
        out_shape=out_shape,
        compiler_params=_params(("parallel", "arbitrary")),
        name="in_proj",
    )(x2d, w_bf, b2d, tables)


def _rope_angles(pos):
    inv_freq = ROPE_THETA ** (-jnp.arange(ROT_HALF, dtype=F32) * 2.0 / ROT_DIM)
    return pos.astype(F32)[:, None] * inv_freq[None, :]


def _rope_tables(pos):
    t = pos.shape[0]
    ang = _rope_angles(pos)
    cos, sin = jnp.cos(ang), jnp.sin(ang)
    rest = HEAD_DIM - ROT_DIM
    c64 = jnp.concatenate([cos, cos, jnp.ones((t, rest), F32)], axis=1)
    lo64 = jnp.concatenate([-sin, jnp.zeros((t, HEAD_DIM - ROT_HALF), F32)], axis=1)
    hi64 = jnp.concatenate([jnp.zeros((t, ROT_HALF), F32), sin, jnp.zeros((t, rest), F32)], axis=1)
    rep = LANES // HEAD_DIM
    return jnp.stack([jnp.tile(c64, (1, rep)), jnp.tile(lo64, (1, rep)), jnp.tile(hi64, (1, rep))])


def _attn_kernel(q_ref, kv_ref, bias_ref, o_ref, lse_ref, s_ref, p_ref, *, rate, nblk):
    nk = s_ref.shape[3]
    lane = lax.broadcasted_iota(jnp.int32, (ATTN_BLOCK, LANES), 1)
    low_half = lane < HEAD_DIM
    keep_lo = low_half.astype(BF16)
    keep_hi = 1 - keep_lo
    ones_rhs = jnp.ones((nk, LANES), BF16)

    def strided(start, size):
        return pl.ds(start, size, stride=rate) if rate > 1 else pl.ds(start, size)

    def unit_rows(u):
        res, b = (u // nblk, u % nblk) if nblk > 1 else (u, 0)
        rows_q = strided(b * (ATTN_BLOCK * rate) + res, ATTN_BLOCK)
        rows_k = strided(jnp.maximum(b - 1, 0) * (ATTN_BLOCK * rate) + res, nk)
        return rows_q, rows_k, bias_ref[jnp.minimum(b, 1)]

    def scores(slot, rows_q, rows_k, bias):
        for hp in range(HEAD_PAIRS):
            qp = q_ref[hp, rows_q, :].astype(BF16)
            kp = kv_ref[hp, rows_k, :].astype(BF16)
            for half, keep in enumerate((keep_lo, keep_hi)):
                s = lax.dot_general(qp * keep, kp, NT_DIMS, preferred_element_type=F32)
                s_ref[slot, 2 * hp + half] = s + bias

    def values(slot, rows_q, rows_k, m):
        for hp in range(HEAD_PAIRS):
            vp = kv_ref[HEAD_PAIRS + hp, rows_k, :].astype(BF16)
            rhs = jnp.concatenate([vp, ones_rhs], axis=1)
            ol_lo = jnp.dot(p_ref[slot, 2 * hp], rhs, preferred_element_type=F32)
            ol_hi = jnp.dot(p_ref[slot, 2 * hp + 1], rhs, preferred_element_type=F32)
            l_lo, l_hi = ol_lo[:, LANES:], ol_hi[:, LANES:]
            o_pair = jnp.where(low_half, ol_lo[:, :LANES] * (1.0 / l_lo), ol_hi[:, :LANES] * (1.0 / l_hi))
            lse_pair = jnp.where(low_half, m[2 * hp] + jnp.log(l_lo), m[2 * hp + 1] + jnp.log(l_hi))
            o_ref[hp, rows_q, :] = o_pair
            lse_ref[hp, rows_q, :] = lse_pair

    nslot = s_ref.shape[0]

    def unit_group(i, carry):
        units = [unit_rows(nslot * i + j) for j in range(nslot)]
        for j, u in enumerate(units):
            scores(j, *u)
        sc = s_ref[...]
        m = jnp.max(sc, axis=-1, keepdims=True)
        p_ref[...] = jnp.exp(sc - m).astype(BF16)
        for j, u in enumerate(units):
            values(j, u[0], u[1], m[j])
        return carry

    lax.fori_loop(0, rate * nblk // nslot, unit_group, 0)


def _band_bias(nblk):
    qi = jnp.arange(ATTN_BLOCK)[:, None]
    ci = jnp.arange(ATTN_BLOCK)[None, :]
    causal = jnp.where(ci <= qi, 0.0, -jnp.inf).astype(F32)
    if nblk == 1:
        return jnp.stack([causal, causal])
    band = jnp.where(ci >= qi, 0.0, -jnp.inf).astype(F32)
    closed = jnp.full((ATTN_BLOCK, ATTN_BLOCK), -jnp.inf, F32)
    return jnp.stack([jnp.concatenate([causal, closed], axis=1), jnp.concatenate([band, causal], axis=1)])


def _prompt_attention(q, kv, g, n, s):
    rate = DIL_RATES[g]
    nblk = s // rate // ATTN_BLOCK
    nk = 2 * ATTN_BLOCK if nblk > 1 else ATTN_BLOCK
    out_sds = jax.ShapeDtypeStruct((HEAD_PAIRS, n * s, LANES), F32)
    seq = lambda b: (0, b, 0)
    o_spec = pl.BlockSpec((HEAD_PAIRS, s, LANES), seq)
    return pl.pallas_call(
        functools.partial(_attn_kernel, rate=rate, nblk=nblk),
        grid=(n,),
        in_specs=[
            o_spec,
            pl.BlockSpec((2 * HEAD_PAIRS, s, LANES), seq),
            _const_spec((2, ATTN_BLOCK, nk)),
        ],
        out_specs=[o_spec, o_spec],
        out_shape=[out_sds, out_sds],
        scratch_shapes=[pltpu.VMEM((ATTN_UNITS_PER_STAGE, HEADS, ATTN_BLOCK, nk), F32),
                        pltpu.VMEM((ATTN_UNITS_PER_STAGE, HEADS, ATTN_BLOCK, nk), BF16)],
        compiler_params=_params(("parallel",)),
        name=f"band_attn_g{g}",
    )(q, kv, _band_bias(nblk))


def _group_weights(lses):
    lmax = jnp.maximum(jnp.maximum(lses[0], lses[1]), lses[2])
    es = [jnp.exp(l - lmax) for l in lses]
    inv = 1.0 / (es[0] + es[1] + es[2])
    return [e * inv for e in es]


def _conv_tail(y, cb, clg, clb):
    z = _layer_norm(y + cb, clg, clb)
    return z * jax.nn.sigmoid(z)


def _out_proj_ln(x, attn, conv, wo_ref, lng, lnb):
    y = (jnp.dot(attn.astype(BF16), wo_ref[0:ATTN_WIDTH, :], preferred_element_type=F32)
         + jnp.dot(conv.astype(BF16), wo_ref[ATTN_WIDTH:ATTN_WIDTH + CONV_CH, :],
                   preferred_element_type=F32))
    return _layer_norm(DN_ALPHA * x + y, lng, lnb)


def _mix_kernel(x_ref, o0_ref, o1_ref, o2_ref, l0_ref, l1_ref, l2_ref, gc_ref, gp_ref,
                cw_ref, cb_ref, clg_ref, clb_ref, wo_ref, lng_ref, lnb_ref, h_ref, ext_ref, phase_ref):
    tm = x_ref.shape[0]
    first = pl.program_id(1) == 0
    ext_ref[0:CONV_HALO, :] = jnp.where(first, 0.0, gp_ref[...])
    ext_ref[CONV_HALO:CONV_HALO + tm, :] = gc_ref[...]
    ext_ref[CONV_HALO + tm:, :] = jnp.zeros((CONV_SLACK, CONV_CH), F32)
    base = CONV_HALO - (CONV_WIDTH - 1)
    span = tm + CONV_SLACK
    acc = None
    for r in range(SUBLANES):
        part = None
        for k in range(r, CONV_WIDTH, SUBLANES):
            term = cw_ref[k:k + 1, :] * ext_ref[k - r:k - r + span, :]
            part = term if part is None else part + term
        phase_ref[r] = part
        shifted = phase_ref[r, base + r:base + r + tm, :]
        acc = shifted if acc is None else acc + shifted
    conv = _conv_tail(acc, cb_ref[...], clg_ref[...], clb_ref[...])
    pieces = []
    for hp in range(HEAD_PAIRS):
        w = _group_weights((l0_ref[hp], l1_ref[hp], l2_ref[hp]))
        pieces.append(w[0] * o0_ref[hp] + w[1] * o1_ref[hp] + w[2] * o2_ref[hp])
    attn = jnp.concatenate(pieces, axis=1)
    h_ref[...] = _out_proj_ln(x_ref[...], attn, conv, wo_ref, lng_ref[...], lnb_ref[...])


def _prompt_mix(x2d, os_, lses, glu, conv_w, conv_b, clg, clb, wo_bf, lng, lnb, n, s, tm):
    tps = s // tm
    hb = tm // CONV_HALO
    row = lambda b, t: (b * tps + t, 0)
    aw = pl.BlockSpec((HEAD_PAIRS, tm, LANES), lambda b, t: (0, b * tps + t, 0))
    return pl.pallas_call(
        _mix_kernel,
        grid=(n, tps),
        in_specs=[
            pl.BlockSpec((tm, D_MODEL), row),
            aw, aw, aw, aw, aw, aw,
            pl.BlockSpec((tm, CONV_CH), row),
            pl.BlockSpec((CONV_HALO, CONV_CH),
                         lambda b, t: (jnp.maximum((b * tps + t) * hb - 1, 0), 0)),
            _const_spec((CONV_WIDTH, CONV_CH)),
            _const_spec((1, CONV_CH)), _const_spec((1, CONV_CH)), _const_spec((1, CONV_CH)),
            _const_spec((ATTN_WIDTH + CONV_CH, D_MODEL)),
            _const_spec((1, D_MODEL)), _const_spec((1, D_MODEL)),
        ],
        out_specs=pl.BlockSpec((tm, D_MODEL), row),
        out_shape=jax.ShapeDtypeStruct((n * s, D_MODEL), F32),
        scratch_shapes=[pltpu.VMEM((CONV_HALO + tm + CONV_SLACK, CONV_CH), F32),
                        pltpu.VMEM((SUBLANES, tm + CONV_SLACK, CONV_CH), F32)],
        compiler_params=_params(("parallel", "parallel")),
        name="mix_out",
    )(x2d, *os_, *lses, glu, glu, conv_w, conv_b, clg, clb, wo_bf, lng, lnb)


def _routing_rows(logit_rows, bias_ref):
    m = logit_rows[0]
    for r in logit_rows[1:]:
        m = jnp.maximum(m, r)
    ex = [jnp.exp(r - m) for r in logit_rows]
    tot = ex[0]
    for e in ex[1:]:
        tot = tot + e
    scores = [e / tot for e in ex]
    sel = [scores[e] + bias_ref[e:e + 1, :] for e in range(N_EXPERTS)]
    grp = []
    for g in range(N_EXPERT_GROUPS):
        v = sel[g * EXPERTS_PER_GROUP:(g + 1) * EXPERTS_PER_GROUP]
        best = v[0] + v[1]
        for i in range(EXPERTS_PER_GROUP):
            for j in range(i + 1, EXPERTS_PER_GROUP):
                if (i, j) != (0, 1):
                    best = jnp.maximum(best, v[i] + v[j])
        grp.append(best)
    gmax = grp[0]
    for v in grp[1:]:
        gmax = jnp.maximum(gmax, v)
    taken = None
    in_group = []
    for g in range(N_EXPERT_GROUPS):
        hit = grp[g] == gmax
        if taken is None:
            in_group.append(hit)
            taken = hit
        else:
            in_group.append(jnp.logical_and(hit, jnp.logical_not(taken)))
            taken = jnp.logical_or(taken, hit)
    gates = []
    for e in range(N_EXPERTS):
        g = e // EXPERTS_PER_GROUP
        rank = jnp.zeros_like(sel[e])
        for o in range(g * EXPERTS_PER_GROUP, (g + 1) * EXPERTS_PER_GROUP):
            if o == e:
                continue
            ahead = sel[o] > sel[e]
            if o < e:
                ahead = jnp.logical_or(ahead, sel[o] == sel[e])
            rank = rank + ahead.astype(F32)
        chosen = jnp.logical_and(in_group[g], rank < float(2))
        gates.append(jnp.where(chosen, scores[e], 0.0))
    den = gates[0]
    for v in gates[1:]:
        den = den + v
    return [v / den for v in gates]


def _pool_project(parts, pw_ref, pb_ref, ps_ref):
    cols = []
    for gi in range(len(POOL_WINDOWS)):
        lo = gi * POOL_CH
        y = jnp.dot(parts[gi].astype(BF16), pw_ref[gi], preferred_element_type=F32)
        cols.append((y + pb_ref[gi:gi + 1, :]) * ps_ref[:, lo:lo + POOL_CH])
    return jnp.concatenate(cols, axis=1)


def _pool_kernel(x_ref, xp_ref, pw_ref, pb_ref, ps_ref, lng_ref, lnb_ref, out_ref, ext_ref, lv_ref):
    tm = x_ref.shape[0]
    t = pl.program_id(1)
    ext_ref[0:POOL_HALO, :] = jnp.where(t == 0, 0.0, xp_ref[...])
    ext_ref[POOL_HALO:POOL_HALO + tm, :] = x_ref[...]
    end = POOL_HALO + tm
    top = POOL_LEVELS - 1
    src = ext_ref
    for i, w in enumerate(POOL_WINDOWS[:POOL_LEVELS]):
        r0, c0, shift = SUBLANES * (i + 1), i * POOL_CH, w // 2
        lv_ref[i, r0:end, c0:] = src[r0:end, c0:] + src[r0 - shift:end - shift, c0:]
        src = lv_ref.at[i]
    pos = t * tm + lax.broadcasted_iota(jnp.int32, (tm, 1), 0)
    parts = []
    for gi, w in enumerate(POOL_WINDOWS):
        lo = gi * POOL_CH
        cur = x_ref[:, lo:lo + POOL_CH]
        if gi <= top:
            tot = lv_ref[gi, POOL_HALO:end, lo:lo + POOL_CH]
        else:
            tot = (lv_ref[top, POOL_HALO:end, lo:lo + POOL_CH]
                   + lv_ref[top, POOL_HALO - w // 2:end - w // 2, lo:lo + POOL_CH])
        cnt = jnp.minimum(w, pos + 1).astype(F32)
        parts.append(tot / cnt - cur)
    y = _pool_project(parts, pw_ref, pb_ref, ps_ref)
    out_ref[...] = _layer_norm(DN_ALPHA * x_ref[...] + y, lng_ref[...], lnb_ref[...])


def _prompt_pool(h2d, pw_bf, pb, ps, lng, lnb, n, s, tm):
    tps = s // tm
    hb = tm // POOL_HALO
    row = lambda b, t: (b * tps + t, 0)
    ng = len(POOL_WINDOWS)
    return pl.pallas_call(
        _pool_kernel,
        grid=(n, tps),
        in_specs=[
            pl.BlockSpec((tm, D_MODEL), row),
            pl.BlockSpec((POOL_HALO, D_MODEL),
                         lambda b, t: (jnp.maximum((b * tps + t) * hb - 1, 0), 0)),
            _const_spec((ng, POOL_CH, POOL_CH)),
            _const_spec((ng, POOL_CH)),
            _const_spec((1, D_MODEL)), _const_spec((1, D_MODEL)), _const_spec((1, D_MODEL)),
        ],
        out_specs=pl.BlockSpec((tm, D_MODEL), row),
        out_shape=jax.ShapeDtypeStruct((n * s, D_MODEL), F32),
        scratch_shapes=[pltpu.VMEM((POOL_HALO + tm, D_MODEL), F32),
                        pltpu.VMEM((POOL_LEVELS, POOL_HALO + tm, D_MODEL), F32)],
        compiler_params=_params(("parallel", "parallel")),
        name="pool_mix",
    )(h2d, h2d, pw_bf, pb, ps, lng, lnb)


def _dec_proj_kernel(x_ref, w_ref, b_ref, cs_ref, qkvt_ref, glu_ref):
    xb = x_ref[...].astype(BF16)
    cos, sin = cs_ref[0], cs_ref[1]
    chunk = ATTN_WIDTH

    def proj(c0, width):
        return (jnp.dot(xb, w_ref[:, c0:c0 + width], preferred_element_type=F32)
                + b_ref[:, c0:c0 + width])

    for ci in range(QKV_COLS // chunk):
        pt = proj(ci * chunk, chunk).T
        if ci < 2 * N_DIL:
            pieces = []
            for hh in range(HEADS):
                base = hh * HEAD_DIM
                x1 = pt[base:base + ROT_HALF, :]
                x2 = pt[base + ROT_HALF:base + ROT_DIM, :]
                pieces += [x1 * cos - x2 * sin, x2 * cos + x1 * sin, pt[base + ROT_DIM:base + HEAD_DIM, :]]
            pt = jnp.concatenate(pieces, axis=0)
            if ci < N_DIL:
                pt = pt * ATTN_SCALE
        qkvt_ref[ci * chunk:(ci + 1) * chunk, :] = pt
    ga = proj(QKV_COLS, CONV_CH)
    gb = proj(QKV_COLS + CONV_CH, CONV_CH)
    glu_ref[...] = ga * jax.nn.sigmoid(gb)


def _decode_project(x2d, w_bf, b2d, cs):
    nseq = x2d.shape[0]
    args = (x2d, w_bf, b2d, cs)
    return pl.pallas_call(
        _dec_proj_kernel,
        grid=(1,),
        in_specs=[_const_spec(a.shape) for a in args],
        out_specs=[_const_spec((QKV_COLS, nseq)), _const_spec((nseq, CONV_CH))],
        out_shape=[jax.ShapeDtypeStruct((QKV_COLS, nseq), F32),
                   jax.ShapeDtypeStruct((nseq, CONV_CH), F32)],
        compiler_params=_params(("arbitrary",)),
        name="decode_proj",
    )(*args)


def _cache_block(c_ref, nc_ref, ot_ref, lt_ref, qkvt_ref, g, seq, head0):
    hb, w = c_ref.shape[2], c_ref.shape[4]
    nseq = qkvt_ref.shape[1]
    rate = DIL_RATES[g]
    rows = hb * HEAD_DIM
    r0 = pl.multiple_of(head0 * HEAD_DIM, rows)
    mine = lax.broadcasted_iota(jnp.int32, (rows, nseq), 1) == seq

    def column(base):
        x = qkvt_ref[pl.ds(base + r0, rows), :]
        return jnp.sum(jnp.where(mine, x, 0.0), axis=1, keepdims=True).reshape(hb, HEAD_DIM, 1)

    qc = column(g * ATTN_WIDTH)
    kc = column((N_DIL + g) * ATTN_WIDTH)
    vc = column((2 * N_DIL + g) * ATTN_WIDTH)
    kt = c_ref[0, 0]
    vt = c_ref[0, 1]
    tok = lax.broadcasted_iota(jnp.int32, (1, 1, w), 2)
    in_window = (tok & (rate - 1)) == 0
    newest = tok == w - 1

    sc = jnp.where(in_window, jnp.sum(kt * qc, axis=1, keepdims=True), -jnp.inf)
    sn = jnp.sum(qc * kc, axis=1, keepdims=True)
    m = jnp.maximum(jnp.max(sc, axis=2, keepdims=True), sn)
    p = jnp.exp(sc - m)
    pn = jnp.exp(sn - m)
    l = jnp.sum(p, axis=2, keepdims=True) + pn
    inv = 1.0 / l
    o = jnp.sum(vt * (p * inv), axis=2, keepdims=True) + vc * (pn * inv)
    lse = jnp.broadcast_to(m + jnp.log(l), (hb, HEAD_DIM, 1))
    acc_rows = pl.ds(r0, rows)
    ot_ref[acc_rows, :] = jnp.where(mine, o.reshape(rows, 1), ot_ref[acc_rows, :])
    lt_ref[acc_rows, :] = jnp.where(mine, lse.reshape(rows, 1), lt_ref[acc_rows, :])

    def shifted(old, new_col):
        rolled = pltpu.roll(old.reshape(rows, w), w - 1, 1).reshape(hb, HEAD_DIM, w)
        return jnp.where(newest, new_col, rolled)

    nc_ref[0, 0] = shifted(kt, kc)
    nc_ref[0, 1] = shifted(vt, vc)


def _moe_stream_kernel(*refs, groups, seq_base, resident):
    ng = len(groups)
    (h_ref, rw_ref, rb_ref, w1_ref, w3_ref, w2_ref, lng_ref, lnb_ref, qkvt_ref) = refs[:MOE_FIXED_INPUTS]
    c_refs = refs[MOE_FIXED_INPUTS:MOE_FIXED_INPUTS + ng]
    n_in = MOE_FIXED_INPUTS + ng + (ng if seq_base else 0)
    out_ref = refs[n_in]
    stream_out = refs[n_in + 1:n_in + 1 + 3 * ng]
    comb_ref, ct_ref = refs[n_in + 1 + 3 * ng:]
    i, k = pl.program_id(0), pl.program_id(1)
    flat = i * MOE_SUBSTEPS + k
    hb = h_ref[...].astype(BF16)

    @pl.when(k == 0)
    def _():
        logits = jnp.dot(hb, rw_ref[...], preferred_element_type=F32)
        lt = logits.T
        comb_rows = _routing_rows([lt[e:e + 1, :] for e in range(N_EXPERTS)], rb_ref)
        ct_ref[...] = jnp.zeros_like(ct_ref)
        for e in range(N_EXPERTS):
            ct_ref[e:e + 1, :] = comb_rows[e]
        comb_ref[...] = ct_ref[...].T
        out_ref[...] = jnp.zeros_like(out_ref)

    @pl.when(flat == 0)
    def _():
        for s in range(ng):
            for t in range(2):
                acc = stream_out[3 * s + 1 + t]
                acc[...] = jnp.zeros_like(acc)

    comb = comb_ref[...]
    lane = lax.broadcasted_iota(jnp.int32, comb.shape, 1)
    ffn = None
    for ee in range(EXPERTS_PER_SUBSTEP):
        e = k * EXPERTS_PER_SUBSTEP + ee
        we = e if resident else ee
        a = jnp.dot(hb, w1_ref[we], preferred_element_type=F32)
        b = jnp.dot(hb, w3_ref[we], preferred_element_type=F32)
        gate = jnp.sum(jnp.where(lane == e, comb, 0.0), axis=1, keepdims=True)
        gated = (a * jax.nn.sigmoid(a)) * b * gate
        part = jnp.dot(gated.astype(BF16), w2_ref[ee], preferred_element_type=F32)
        ffn = part if ffn is None else ffn + part
    out_ref[...] += ffn

    blocks_per_seq = HEADS // CACHE_HEAD_BLOCK
    for s, g in enumerate(groups):
        _cache_block(c_refs[s], stream_out[3 * s], stream_out[3 * s + 1], stream_out[3 * s + 2], qkvt_ref,
                     g, seq_base + flat // blocks_per_seq, (flat % blocks_per_seq) * CACHE_HEAD_BLOCK)

    @pl.when(k == MOE_SUBSTEPS - 1)
    def _():
        out_ref[...] = _layer_norm(DN_ALPHA * h_ref[...] + out_ref[...], lng_ref[...], lnb_ref[...])


def _moe_stream(h2d, rw_pad, rb, w1, w3, w2, layer, lng, lnb, qkvt, caches_t, tm, seq_base, prev):
    m = h2d.shape[0]
    nseq = qkvt.shape[1]
    groups = tuple(range(len(caches_t)))
    resident = bool(caches_t)
    blocks_per_seq = HEADS // CACHE_HEAD_BLOCK
    first_block = seq_base * blocks_per_seq
    if caches_t:
        assert len(caches_t) == N_DIL and ((m // tm) * MOE_SUBSTEPS) % blocks_per_seq == 0
        assert first_block + (m // tm) * MOE_SUBSTEPS <= nseq * blocks_per_seq
    row = lambda i, k: (i, 0)
    step_block = lambda i, k: (layer, k, 0, 0)
    w13_spec = (pl.BlockSpec((None, N_EXPERTS, D_MODEL, D_EXPERT), lambda i, k: (layer, 0, 0, 0),
                             pipeline_mode=pl.Buffered(1)) if resident
                else pl.BlockSpec((None, EXPERTS_PER_SUBSTEP, D_MODEL, D_EXPERT), step_block))
    acc_spec = _const_spec((ATTN_WIDTH, nseq))
    acc_sds = jax.ShapeDtypeStruct((ATTN_WIDTH, nseq), F32)

    def block_index(i, k):
        b = first_block + i * MOE_SUBSTEPS + k
        return (b // blocks_per_seq, 0, b % blocks_per_seq, 0, 0)

    cache_specs, stream_specs, stream_shapes = [], [], []
    for c in caches_t:
        spec = pl.BlockSpec((1, 2, CACHE_HEAD_BLOCK, HEAD_DIM, c.shape[4]), block_index)
        cache_specs.append(spec)
        stream_specs += [spec, acc_spec, acc_spec]
        stream_shapes += [jax.ShapeDtypeStruct(c.shape, F32), acc_sds, acc_sds]
    extra_in, extra_specs, aliases = [], [], {}
    if seq_base:
        n_fixed = MOE_FIXED_INPUTS + len(caches_t)
        for s, nc in enumerate(prev):
            aliases[n_fixed + s] = 1 + 3 * s
            extra_in.append(nc)
            extra_specs.append(pl.BlockSpec(memory_space=pl.ANY))
    return pl.pallas_call(
        functools.partial(_moe_stream_kernel, groups=groups, seq_base=seq_base, resident=resident),
        grid=(m // tm, MOE_SUBSTEPS),
        in_specs=[
            pl.BlockSpec((tm, D_MODEL), row),
            _const_spec((D_MODEL, LANES)),
            _const_spec((N_EXPERTS, 1)),
            w13_spec,
            w13_spec,
            pl.BlockSpec((None, EXPERTS_PER_SUBSTEP, D_EXPERT, D_MODEL), step_block),
            _const_spec((1, D_MODEL)), _const_spec((1, D_MODEL)),
            _resident_spec(qkvt.shape),
        ] + cache_specs + extra_specs,
        out_specs=[pl.BlockSpec((tm, D_MODEL), row)] + stream_specs,
        out_shape=[jax.ShapeDtypeStruct((m, D_MODEL), F32)] + stream_shapes,
        input_output_aliases=aliases,
        scratch_shapes=[pltpu.VMEM((tm, LANES), F32), pltpu.VMEM((LANES, tm), F32)],
        compiler_params=pltpu.CompilerParams(dimension_semantics=("arbitrary", "arbitrary"),
                                             vmem_limit_bytes=STREAM_VMEM_LIMIT_BYTES),
        name=f"moe_stream_from{seq_base}" if caches_t else "moe_ffn",
    )(h2d, rw_pad, rb, w1, w3, w2, lng, lnb, qkvt, *caches_t, *extra_in)


def _dec_mix_kernel(x_ref, *refs, split):
    acc_refs, (st_ref, glu_ref, cw_ref, cb_ref, clg_ref, clb_ref, wo_ref, lng_ref, lnb_ref,
               h_ref, nst_ref) = refs[:4 * N_DIL], refs[4 * N_DIL:]
    npre = CONV_WIDTH - 1
    glu = glu_ref[...]
    acc = cw_ref[npre:npre + 1, :] * glu
    for k in range(npre):
        acc = acc + cw_ref[k:k + 1, :] * st_ref[k]
    conv = _conv_tail(acc, cb_ref[...], clg_ref[...], clb_ref[...])
    first = lax.broadcasted_iota(jnp.int32, (ATTN_WIDTH, x_ref.shape[0]), 1) < split

    def both(idx):
        return jnp.where(first, acc_refs[idx][...], acc_refs[2 * N_DIL + idx][...])

    os_ = [both(g) for g in range(N_DIL)]
    w = _group_weights([both(N_DIL + g) for g in range(N_DIL)])
    attn = (w[0] * os_[0] + w[1] * os_[1] + w[2] * os_[2]).T
    h_ref[...] = _out_proj_ln(x_ref[...], attn, conv, wo_ref, lng_ref[...], lnb_ref[...])
    nst_ref[0:npre - 1] = st_ref[1:npre]
    nst_ref[npre - 1] = glu


def _decode_mix(x2d, accs, split, state_t, glu, conv_w, conv_b, clg, clb, wo_bf, lng, lnb):
    nseq = x2d.shape[0]
    args = (x2d, *accs, state_t, glu, conv_w, conv_b, clg, clb, wo_bf, lng, lnb)
    return pl.pallas_call(
        functools.partial(_dec_mix_kernel, split=split),
        grid=(1,),
        in_specs=[_const_spec(a.shape) for a in args],
        out_specs=[_const_spec((nseq, D_MODEL)), _const_spec(state_t.shape)],
        out_shape=[jax.ShapeDtypeStruct((nseq, D_MODEL), F32),
                   jax.ShapeDtypeStruct(state_t.shape, F32)],
        compiler_params=_params(("arbitrary",)),
        name="decode_mix",
    )(*args)


def _dec_pool_kernel(x_ref, st_ref, pw_ref, pb_ref, ps_ref, lng_ref, lnb_ref, out_ref, nst_ref):
    x = x_ref[...]
    parts = []
    for gi, w in enumerate(POOL_WINDOWS):
        lo = gi * POOL_CH
        cur = x[:, lo:lo + POOL_CH]
        tot = cur
        for j in range(1, w):
            tot = tot + st_ref[POOL_PREFIX - j, :, lo:lo + POOL_CH]
        parts.append(tot / float(w) - cur)
    y = _pool_project(parts, pw_ref, pb_ref, ps_ref)
    out_ref[...] = _layer_norm(DN_ALPHA * x + y, lng_ref[...], lnb_ref[...])
    nst_ref[0:POOL_PREFIX - 1] = st_ref[1:POOL_PREFIX]
    nst_ref[POOL_PREFIX - 1] = x


def _decode_pool(h2d, state_t, pw_bf, pb, ps, lng, lnb):
    nseq = h2d.shape[0]
    args = (h2d, state_t, pw_bf, pb, ps, lng, lnb)
    return pl.pallas_call(
        _dec_pool_kernel,
        grid=(1,),
        in_specs=[_const_spec(a.shape) for a in args],
        out_specs=[_const_spec((nseq, D_MODEL)), _const_spec(state_t.shape)],
        out_shape=[jax.ShapeDtypeStruct((nseq, D_MODEL), F32),
                   jax.ShapeDtypeStruct(state_t.shape, F32)],
        compiler_params=_params(("arbitrary",)),
        name="decode_pool",
    )(*args)


def _token_minor(a):
    return jnp.transpose(a, (0, 2, 3, 4, 1))


def _token_major(a):
    return jnp.transpose(a, (0, 4, 1, 2, 3))


def kernel(x_prompt, x_sample, cache_attn_w128, cache_attn_w512, cache_attn_w2048, state_conv, state_pool,
           w_in, b_in, conv_w, conv_b, conv_ln_g, conv_ln_b, w_out, pool_w, pool_b, pool_scale,
           ln_mix_g, ln_mix_b, ln_ffn_g, ln_ffn_b, router_w, router_bias, moe_w1, moe_w3, moe_w2):
    n, s, d = x_prompt.shape
    nseq = x_sample.shape[0]
    past = cache_attn_w2048.shape[2]
    assert d == D_MODEL and x_sample.shape[1] == 1 and s % ROW_TILE == 0 and s == DIL_WINDOWS[2]
    assert cache_attn_w128.shape[0] == 1 and past == DIL_WINDOWS[2]
    caches = (cache_attn_w128[0], cache_attn_w512[0], cache_attn_w2048[0])

    w_in_bf = w_in[0].astype(BF16)
    w_out_bf = w_out[0].astype(BF16)
    pool_w_bf = pool_w[0].astype(BF16)
    moe_bf = tuple(w.astype(BF16) for w in (moe_w1, moe_w3, moe_w2))
    rw_pad = jnp.pad(router_w, ((0, 0), (0, LANES - N_EXPERTS))).astype(BF16)
    rb = router_bias.astype(F32).reshape(N_EXPERTS, 1)
    r2 = lambda v: v.reshape(1, -1)
    b_in2 = r2(b_in[0])
    cb, clg, clb = r2(conv_b[0]), r2(conv_ln_g[0]), r2(conv_ln_b[0])
    ps = r2(pool_scale[0])

    xs = x_sample.reshape(nseq, d)
    ang = _rope_angles(past + jnp.arange(1, dtype=jnp.int32))
    cs = jnp.stack([jnp.broadcast_to(jnp.cos(ang).T, (ROT_HALF, nseq)),
                    jnp.broadcast_to(jnp.sin(ang).T, (ROT_HALF, nseq))])
    qkvt, glus = _decode_project(xs, w_in_bf, b_in2, cs)
    caches_t = [_token_minor(c) for c in caches]

    seqs_per_layer = (n * s // ROW_TILE) * MOE_SUBSTEPS * CACHE_HEAD_BLOCK // HEADS
    assert DEPTH * seqs_per_layer == nseq

    def moe_stream(h2d, layer, prev):
        res = _moe_stream(h2d, rw_pad, rb, *moe_bf, layer, r2(ln_ffn_g[layer]), r2(ln_ffn_b[layer]),
                          qkvt, caches_t, ROW_TILE, layer * seqs_per_layer, prev)
        return res[0], [tuple(res[1 + 3 * g:4 + 3 * g]) for g in range(N_DIL)]

    def moe(h2d, layer):
        return _moe_stream(h2d, rw_pad, rb, *moe_bf, layer, r2(ln_ffn_g[layer]), r2(ln_ffn_b[layer]),
                           qkvt, [], h2d.shape[0], 0, None)[0]

    xp = x_prompt.reshape(n * s, d)
    tabs_p = _rope_tables(jnp.arange(s, dtype=jnp.int32))
    (q0, q1, q2, kv0, kv1, kv2, glu, kt0, kt1, kt2) = _project(xp, w_in_bf, b_in2, tabs_p, n, s, ROW_TILE)
    os_, lses = [], []
    for g, (q, kv) in enumerate(((q0, kv0), (q1, kv1), (q2, kv2))):
        o, l = _prompt_attention(q, kv, g, n, s)
        os_.append(o)
        lses.append(l)
    h = _prompt_mix(xp, os_, lses, glu, conv_w[0], cb, clg, clb, w_out_bf,
                    r2(ln_mix_g[0]), r2(ln_mix_b[0]), n, s, ROW_TILE)
    h, first_pass = moe_stream(h, 0, None)
    p_pool = h.reshape(n, s, d)[:, s - POOL_PREFIX:][None]
    h = _prompt_pool(h, pool_w_bf, pool_b[0], ps, r2(ln_mix_g[1]), r2(ln_mix_b[1]), n, s, ROW_TILE)
    h, second_pass = moe_stream(h, 1, [st[0] for st in first_pass])
    y_prompt = h.reshape(n, s, d)
    p_attn = [_token_major(kt.reshape(n, 2, HEADS, HEAD_DIM, kt.shape[2]))[None] for kt in (kt0, kt1, kt2)]
    p_conv = glu.reshape(n, s, CONV_CH)[:, s - (CONV_WIDTH - 1):][None]

    accs = []
    for streamed in (first_pass, second_pass):
        accs += [st[1] for st in streamed] + [st[2] for st in streamed]
    s_attn = [_token_major(st[0])[None] for st in second_pass]
    conv_t = jnp.transpose(state_conv[0], (1, 0, 2))
    hs, new_conv = _decode_mix(xs, accs, seqs_per_layer, conv_t, glus, conv_w[0], cb, clg, clb, w_out_bf,
                               r2(ln_mix_g[0]), r2(ln_mix_b[0]))
    hs = moe(hs, 0)
    pool_t = jnp.transpose(state_pool[0], (1, 0, 2))
    hs, new_pool = _decode_pool(hs, pool_t, pool_w_bf, pool_b[0], ps,
                                r2(ln_mix_g[1]), r2(ln_mix_b[1]))
    y_sample = moe(hs, 1).reshape(nseq, 1, d)
    s_conv = jnp.transpose(new_conv, (1, 0, 2))[None]
    s_pool = jnp.transpose(new_pool, (1, 0, 2))[None]

    return (y_prompt, y_sample, p_attn[0], p_attn[1], p_attn[2], p_conv, p_pool,
            s_attn[0], s_attn[1], s_attn[2], s_conv, s_pool)
```

```python
import functools

import jax
import jax.numpy as jnp
from jax import lax
from jax.experimental import pallas as pl
from jax.experimental.pallas import tpu as pltpu

F32 = jnp.float32
BF16 = jnp.bfloat16

D_MODEL = 1024
HEAD_DIM = 64
HEADS = 8
ATTN_WIDTH = HEADS * HEAD_DIM
N_DIL = 3
DIL_WINDOWS = (128, 512, 2048)
DIL_RATES = (1, 4, 16)
ATTN_BLOCK = 128
ATTN_SCALE = HEAD_DIM ** -0.5
ROT_DIM = HEAD_DIM // 4
ROT_HALF = ROT_DIM // 2
ROPE_THETA = 500000.0
QKV_COLS = 3 * N_DIL * ATTN_WIDTH
CONV_CH = D_MODEL // 4
CONV_WIDTH = 31
IN_COLS = QKV_COLS + 2 * CONV_CH
POOL_WINDOWS = (2, 4, 8, 16)
POOL_CH = D_MODEL // len(POOL_WINDOWS)
POOL_PREFIX = max(POOL_WINDOWS) - 1
N_EXPERTS = 16
N_EXPERT_GROUPS = 4
EXPERTS_PER_GROUP = 4
D_EXPERT = 256
DEPTH = 2
DN_ALPHA = (2.0 * DEPTH) ** 0.25
LN_EPS = 1e-5

LANES = 128
SUBLANES = 8
VMEM_LIMIT_BYTES = 56 * 1024 * 1024
ROW_TILE = 512
CONV_HALO = 32
CONV_SLACK = -(-(CONV_HALO - (CONV_WIDTH - 1) + SUBLANES) // SUBLANES) * SUBLANES
POOL_LEVELS = len(POOL_WINDOWS) - 1
POOL_HALO = SUBLANES * (POOL_LEVELS + 1)
assert POOL_WINDOWS[-1] // 2 == SUBLANES and POOL_HALO >= POOL_PREFIX
STREAM_VMEM_LIMIT_BYTES = 60000 * 1024
MOE_SUBSTEPS = 4
EXPERTS_PER_SUBSTEP = N_EXPERTS // MOE_SUBSTEPS
MOE_FIXED_INPUTS = 9
CACHE_HEAD_BLOCK = 4
HEAD_PAIRS = ATTN_WIDTH // LANES
ATTN_UNITS_PER_STAGE = 8

NT_DIMS = (((1,), (1,)), ((), ()))


def _params(sem):
    return pltpu.CompilerParams(dimension_semantics=sem, vmem_limit_bytes=VMEM_LIMIT_BYTES)


def _const_spec(shape):
    nd = len(shape)
    return pl.BlockSpec(shape, lambda *_: (0,) * nd)


def _resident_spec(shape):
    nd = len(shape)
    return pl.BlockSpec(shape, lambda *_: (0,) * nd, pipeline_mode=pl.Buffered(1))


def _layer_norm(x, g, b):
    mu = jnp.mean(x, axis=-1, keepdims=True)
    xc = x - mu
    var = jnp.mean(xc * xc, axis=-1, keepdims=True)
    return xc * lax.rsqrt(var + LN_EPS) * g + b


def _proj_kernel(x_ref, w_ref, b_ref, tab_ref, q0_ref, q1_ref, q2_ref, kv0_ref, kv1_ref, kv2_ref,
                 glu_ref, kt0_ref, kt1_ref, kt2_ref):
    tm = x_ref.shape[0]
    last = pl.program_id(1) == pl.num_programs(1) - 1
    xb = x_ref[...].astype(BF16)
    cosm, sin_lo, sin_hi = tab_ref[0], tab_ref[1], tab_ref[2]

    def proj(c0, width):
        return (jnp.dot(xb, w_ref[:, c0:c0 + width], preferred_element_type=F32)
                + b_ref[:, c0:c0 + width])

    def rope(t):
        parts = []
        for j in range(t.shape[1] // LANES):
            v = t[:, j * LANES:(j + 1) * LANES]
            parts.append(v * cosm
                         + pltpu.roll(v, LANES - ROT_HALF, 1) * sin_lo
                         + pltpu.roll(v, ROT_HALF, 1) * sin_hi)
        return jnp.concatenate(parts, axis=1)

    q_refs = (q0_ref, q1_ref, q2_ref)
    kv_refs = (kv0_ref, kv1_ref, kv2_ref)
    for g in range(N_DIL):
        c = g * ATTN_WIDTH
        q = rope(proj(c, ATTN_WIDTH)) * ATTN_SCALE
        k = rope(proj(N_DIL * ATTN_WIDTH + c, ATTN_WIDTH))
        v = proj(2 * N_DIL * ATTN_WIDTH + c, ATTN_WIDTH)
        for hp in range(HEAD_PAIRS):
            cols = slice(hp * LANES, (hp + 1) * LANES)
            q_refs[g][hp] = q[:, cols]
            kv_refs[g][hp] = k[:, cols]
            kv_refs[g][HEAD_PAIRS + hp] = v[:, cols]
        if g == 2:
            kt2_ref[0, 0:ATTN_WIDTH, :] = k.T
            kt2_ref[0, ATTN_WIDTH:2 * ATTN_WIDTH, :] = v.T
    ga = proj(QKV_COLS, CONV_CH)
    gb = proj(QKV_COLS + CONV_CH, CONV_CH)
    glu_ref[...] = ga * jax.nn.sigmoid(gb)

    @pl.when(last)
    def _():
        for kt_ref, kv_ref, keep in ((kt0_ref, kv0_ref, DIL_WINDOWS[0]), (kt1_ref, kv1_ref, DIL_WINDOWS[1])):
            for j in range(2 * HEAD_PAIRS):
                kt_ref[0, j * LANES:(j + 1) * LANES, :] = kv_ref[j, tm - keep:, :].T


def _project(x2d, w_bf, b2d, tables, n, s, tm):
    assert tm == DIL_WINDOWS[1] and tm >= DIL_WINDOWS[0]
    tps = s // tm
    m = n * s
    row = lambda b, t: (b * tps + t, 0)
    seq = lambda b, t: (b, 0, 0)
    kv_rows = 2 * ATTN_WIDTH
    slab = lambda b, t: (0, b * tps + t, 0)
    out_specs = ([pl.BlockSpec((HEAD_PAIRS, tm, LANES), slab)] * N_DIL
                 + [pl.BlockSpec((2 * HEAD_PAIRS, tm, LANES), slab)] * N_DIL
                 + [pl.BlockSpec((tm, CONV_CH), row),
                    pl.BlockSpec((1, kv_rows, DIL_WINDOWS[0]), seq),
                    pl.BlockSpec((1, kv_rows, DIL_WINDOWS[1]), seq),
                    pl.BlockSpec((1, kv_rows, tm), lambda b, t: (b, 0, t))])
    out_shape = ([jax.ShapeDtypeStruct((HEAD_PAIRS, m, LANES), F32)] * N_DIL
                 + [jax.ShapeDtypeStruct((2 * HEAD_PAIRS, m, LANES), F32)] * N_DIL
                 + [jax.ShapeDtypeStruct((m, CONV_CH), F32),
                    jax.ShapeDtypeStruct((n, kv_rows, DIL_WINDOWS[0]), F32),
                    jax.ShapeDtypeStruct((n, kv_rows, DIL_WINDOWS[1]), F32),
                    jax.ShapeDtypeStruct((n, kv_rows, s), F32)])
    return pl.pallas_call(
        _proj_kernel,
        grid=(n, tps),
        in_specs=[
            pl.BlockSpec((tm, D_MODEL), row),
            _resident_spec((D_MODEL, IN_COLS)),
            _const_spec((1, IN_COLS)),
            pl.BlockSpec((3, tm, LANES), lambda b, t: (0, t, 0)),
        ],
        out_specs=out_specs,
        out_shape=out_shape,
        compiler_params=_params(("parallel", "arbitrary")),
        name="in_proj",
    )(x2d, w_bf, b2d, tables)


def _rope_angles(pos):
    inv_freq = ROPE_THETA ** (-jnp.arange(ROT_HALF, dtype=F32) * 2.0 / ROT_DIM)
    return pos.astype(F32)[:, None] * inv_freq[None, :]


def _rope_tables(pos):
    t = pos.shape[0]
    ang = _rope_angles(pos)
    cos, sin = jnp.cos(ang), jnp.sin(ang)
    rest = HEAD_DIM - ROT_DIM
    c64 = jnp.concatenate([cos, cos, jnp.ones((t, rest), F32)], axis=1)
    lo64 = jnp.concatenate([-sin, jnp.zeros((t, HEAD_DIM - ROT_HALF), F32)], axis=1)
    hi64 = jnp.concatenate([jnp.zeros((t, ROT_HALF), F32), sin, jnp.zeros((t, rest), F32)], axis=1)
    rep = LANES // HEAD_DIM
    return jnp.stack([jnp.tile(c64, (1, rep)), jnp.tile(lo64, (1, rep)), jnp.tile(hi64, (1, rep))])


def _attn_kernel(q_ref, kv_ref, bias_ref, o_ref, lse_ref, s_ref, p_ref, *, rate, nblk):
    nk = s_ref.shape[3]
    lane = lax.broadcasted_iota(jnp.int32, (ATTN_BLOCK, LANES), 1)
    low_half = lane < HEAD_DIM
    keep_lo = low_half.astype(BF16)
    keep_hi = 1 - keep_lo
    ones_rhs = jnp.ones((nk, LANES), BF16)

    def strided(start, size):
        return pl.ds(start, size, stride=rate) if rate > 1 else pl.ds(start, size)

    def unit_rows(u):
        res, b = (u // nblk, u % nblk) if nblk > 1 else (u, 0)
        rows_q = strided(b * (ATTN_BLOCK * rate) + res, ATTN_BLOCK)
        rows_k = strided(jnp.maximum(b - 1, 0) * (ATTN_BLOCK * rate) + res, nk)
        return rows_q, rows_k, bias_ref[jnp.minimum(b, 1)]

    def scores(slot, rows_q, rows_k, bias):
        for hp in range(HEAD_PAIRS):
            qp = q_ref[hp, rows_q, :].astype(BF16)
            kp = kv_ref[hp, rows_k, :].astype(BF16)
            for half, keep in enumerate((keep_lo, keep_hi)):
                s = lax.dot_general(qp * keep, kp, NT_DIMS, preferred_element_type=F32)
                s_ref[slot, 2 * hp + half] = s + bias

    def values(slot, rows_q, rows_k, m):
        for hp in range(HEAD_PAIRS):
            vp = kv_ref[HEAD_PAIRS + hp, rows_k, :].astype(BF16)
            rhs = jnp.concatenate([vp, ones_rhs], axis=1)
            ol_lo = jnp.dot(p_ref[slot, 2 * hp], rhs, preferred_element_type=F32)
            ol_hi = jnp.dot(p_ref[slot, 2 * hp + 1], rhs, preferred_element_type=F32)
            l_lo, l_hi = ol_lo[:, LANES:], ol_hi[:, LANES:]
            o_pair = jnp.where(low_half, ol_lo[:, :LANES] * (1.0 / l_lo), ol_hi[:, :LANES] * (1.0 / l_hi))
            lse_pair = jnp.where(low_half, m[2 * hp] + jnp.log(l_lo), m[2 * hp + 1] + jnp.log(l_hi))
            o_ref[hp, rows_q, :] = o_pair
            lse_ref[hp, rows_q, :] = lse_pair

    nslot = s_ref.shape[0]

    def unit_group(i, carry):
        units = [unit_rows(nslot * i + j) for j in range(nslot)]
        for j, u in enumerate(units):
            scores(j, *u)
        sc = s_ref[...]
        m = jnp.max(sc, axis=-1, keepdims=True)
        p_ref[...] = jnp.exp(sc - m).astype(BF16)
        for j, u in enumerate(units):
            values(j, u[0], u[1], m[j])
        return carry

    lax.fori_loop(0, rate * nblk // nslot, unit_group, 0)


def _band_bias(nblk):
    qi = jnp.arange(ATTN_BLOCK)[:, None]
    ci = jnp.arange(ATTN_BLOCK)[None, :]
    causal = jnp.where(ci <= qi, 0.0, -jnp.inf).astype(F32)
    if nblk == 1:
        return jnp.stack([causal, causal])
    band = jnp.where(ci >= qi, 0.0, -jnp.inf).astype(F32)
    closed = jnp.full((ATTN_BLOCK, ATTN_BLOCK), -jnp.inf, F32)
    return jnp.stack([jnp.concatenate([causal, closed], axis=1), jnp.concatenate([band, causal], axis=1)])


def _prompt_attention(q, kv, g, n, s):
    rate = DIL_RATES[g]
    nblk = s // rate // ATTN_BLOCK
    nk = 2 * ATTN_BLOCK if nblk > 1 else ATTN_BLOCK
    out_sds = jax.ShapeDtypeStruct((HEAD_PAIRS, n * s, LANES), F32)
    seq = lambda b: (0, b, 0)
    o_spec = pl.BlockSpec((HEAD_PAIRS, s, LANES), seq)
    return pl.pallas_call(
        functools.partial(_attn_kernel, rate=rate, nblk=nblk),
        grid=(n,),
        in_specs=[
            o_spec,
            pl.BlockSpec((2 * HEAD_PAIRS, s, LANES), seq),
            _const_spec((2, ATTN_BLOCK, nk)),
        ],
        out_specs=[o_spec, o_spec],
        out_shape=[out_sds, out_sds],
        scratch_shapes=[pltpu.VMEM((ATTN_UNITS_PER_STAGE, HEADS, ATTN_BLOCK, nk), F32),
                        pltpu.VMEM((ATTN_UNITS_PER_STAGE, HEADS, ATTN_BLOCK, nk), BF16)],
        compiler_params=_params(("parallel",)),
        name=f"band_attn_g{g}",
    )(q, kv, _band_bias(nblk))


def _group_weights(lses):
    lmax = jnp.maximum(jnp.maximum(lses[0], lses[1]), lses[2])
    es = [jnp.exp(l - lmax) for l in lses]
    inv = 1.0 / (es[0] + es[1] + es[2])
    return [e * inv for e in es]


def _conv_tail(y, cb, clg, clb):
    z = _layer_norm(y + cb, clg, clb)
    return z * jax.nn.sigmoid(z)


def _out_proj_ln(x, attn, conv, wo_ref, lng, lnb):
    y = (jnp.dot(attn.astype(BF16), wo_ref[0:ATTN_WIDTH, :], preferred_element_type=F32)
         + jnp.dot(conv.astype(BF16), wo_ref[ATTN_WIDTH:ATTN_WIDTH + CONV_CH, :],
                   preferred_element_type=F32))
    return _layer_norm(DN_ALPHA * x + y, lng, lnb)


def _mix_kernel(x_ref, o0_ref, o1_ref, o2_ref, l0_ref, l1_ref, l2_ref, gc_ref, gp_ref,
                cw_ref, cb_ref, clg_ref, clb_ref, wo_ref, lng_ref, lnb_ref, h_ref, ext_ref, phase_ref):
    tm = x_ref.shape[0]
    first = pl.program_id(1) == 0
    ext_ref[0:CONV_HALO, :] = jnp.where(first, 0.0, gp_ref[...])
    ext_ref[CONV_HALO:CONV_HALO + tm, :] = gc_ref[...]
    ext_ref[CONV_HALO + tm:, :] = jnp.zeros((CONV_SLACK, CONV_CH), F32)
    base = CONV_HALO - (CONV_WIDTH - 1)
    span = tm + CONV_SLACK
    acc = None
    for r in range(SUBLANES):
        part = None
        for k in range(r, CONV_WIDTH, SUBLANES):
            term = cw_ref[k:k + 1, :] * ext_ref[k - r:k - r + span, :]
            part = term if part is None else part + term
        phase_ref[r] = part
        shifted = phase_ref[r, base + r:base + r + tm, :]
        acc = shifted if acc is None else acc + shifted
    conv = _conv_tail(acc, cb_ref[...], clg_ref[...], clb_ref[...])
    pieces = []
    for hp in range(HEAD_PAIRS):
        w = _group_weights((l0_ref[hp], l1_ref[hp], l2_ref[hp]))
        pieces.append(w[0] * o0_ref[hp] + w[1] * o1_ref[hp] + w[2] * o2_ref[hp])
    attn = jnp.concatenate(pieces, axis=1)
    h_ref[...] = _out_proj_ln(x_ref[...], attn, conv, wo_ref, lng_ref[...], lnb_ref[...])


def _prompt_mix(x2d, os_, lses, glu, conv_w, conv_b, clg, clb, wo_bf, lng, lnb, n, s, tm):
    tps = s // tm
    hb = tm // CONV_HALO
    row = lambda b, t: (b * tps + t, 0)
    aw = pl.BlockSpec((HEAD_PAIRS, tm, LANES), lambda b, t: (0, b * tps + t, 0))
    return pl.pallas_call(
        _mix_kernel,
        grid=(n, tps),
        in_specs=[
            pl.BlockSpec((tm, D_MODEL), row),
            aw, aw, aw, aw, aw, aw,
            pl.BlockSpec((tm, CONV_CH), row),
            pl.BlockSpec((CONV_HALO, CONV_CH),
                         lambda b, t: (jnp.maximum((b * tps + t) * hb - 1, 0), 0)),
            _const_spec((CONV_WIDTH, CONV_CH)),
            _const_spec((1, CONV_CH)), _const_spec((1, CONV_CH)), _const_spec((1, CONV_CH)),
            _const_spec((ATTN_WIDTH + CONV_CH, D_MODEL)),
            _const_spec((1, D_MODEL)), _const_spec((1, D_MODEL)),
        ],
        out_specs=pl.BlockSpec((tm, D_MODEL), row),
        out_shape=jax.ShapeDtypeStruct((n * s, D_MODEL), F32),
        scratch_shapes=[pltpu.VMEM((CONV_HALO + tm + CONV_SLACK, CONV_CH), F32),
                        pltpu.VMEM((SUBLANES, tm + CONV_SLACK, CONV_CH), F32)],
        compiler_params=_params(("parallel", "parallel")),
        name="mix_out",
    )(x2d, *os_, *lses, glu, glu, conv_w, conv_b, clg, clb, wo_bf, lng, lnb)


def _routing_rows(logit_rows, bias_ref):
    m = logit_rows[0]
    for r in logit_rows[1:]:
        m = jnp.maximum(m, r)
    ex = [jnp.exp(r - m) for r in logit_rows]
    tot = ex[0]
    for e in ex[1:]:
        tot = tot + e
    scores = [e / tot for e in ex]
    sel = [scores[e] + bias_ref[e:e + 1, :] for e in range(N_EXPERTS)]
    grp = []
    for g in range(N_EXPERT_GROUPS):
        v = sel[g * EXPERTS_PER_GROUP:(g + 1) * EXPERTS_PER_GROUP]
        best = v[0] + v[1]
        for i in range(EXPERTS_PER_GROUP):
            for j in range(i + 1, EXPERTS_PER_GROUP):
                if (i, j) != (0, 1):
                    best = jnp.maximum(best, v[i] + v[j])
        grp.append(best)
    gmax = grp[0]
    for v in grp[1:]:
        gmax = jnp.maximum(gmax, v)
    taken = None
    in_group = []
    for g in range(N_EXPERT_GROUPS):
        hit = grp[g] == gmax
        if taken is None:
            in_group.append(hit)
            taken = hit
        else:
            in_group.append(jnp.logical_and(hit, jnp.logical_not(taken)))
            taken = jnp.logical_or(taken, hit)
    gates = []
    for e in range(N_EXPERTS):
        g = e // EXPERTS_PER_GROUP
        rank = jnp.zeros_like(sel[e])
        for o in range(g * EXPERTS_PER_GROUP, (g + 1) * EXPERTS_PER_GROUP):
            if o == e:
                continue
            ahead = sel[o] > sel[e]
            if o < e:
                ahead = jnp.logical_or(ahead, sel[o] == sel[e])
            rank = rank + ahead.astype(F32)
        chosen = jnp.logical_and(in_group[g], rank < float(2))
        gates.append(jnp.where(chosen, scores[e], 0.0))
    den = gates[0]
    for v in gates[1:]:
        den = den + v
    return [v / den for v in gates]


def _pool_project(parts, pw_ref, pb_ref, ps_ref):
    cols = []
    for gi in range(len(POOL_WINDOWS)):
        lo = gi * POOL_CH
        y = jnp.dot(parts[gi].astype(BF16), pw_ref[gi], preferred_element_type=F32)
        cols.append((y + pb_ref[gi:gi + 1, :]) * ps_ref[:, lo:lo + POOL_CH])
    return jnp.concatenate(cols, axis=1)


def _pool_kernel(x_ref, xp_ref, pw_ref, pb_ref, ps_ref, lng_ref, lnb_ref, out_ref, ext_ref, lv_ref):
    tm = x_ref.shape[0]
    t = pl.program_id(1)
    ext_ref[0:POOL_HALO, :] = jnp.where(t == 0, 0.0, xp_ref[...])
    ext_ref[POOL_HALO:POOL_HALO + tm, :] = x_ref[...]
    end = POOL_HALO + tm
    top = POOL_LEVELS - 1
    src = ext_ref
    for i, w in enumerate(POOL_WINDOWS[:POOL_LEVELS]):
        r0, c0, shift = SUBLANES * (i + 1), i * POOL_CH, w // 2
        lv_ref[i, r0:end, c0:] = src[r0:end, c0:] + src[r0 - shift:end - shift, c0:]
        src = lv_ref.at[i]
    pos = t * tm + lax.broadcasted_iota(jnp.int32, (tm, 1), 0)
    parts = []
    for gi, w in enumerate(POOL_WINDOWS):
        lo = gi * POOL_CH
        cur = x_ref[:, lo:lo + POOL_CH]
        if gi <= top:
            tot = lv_ref[gi, POOL_HALO:end, lo:lo + POOL_CH]
        else:
            tot = (lv_ref[top, POOL_HALO:end, lo:lo + POOL_CH]
                   + lv_ref[top, POOL_HALO - w // 2:end - w // 2, lo:lo + POOL_CH])
        cnt = jnp.minimum(w, pos + 1).astype(F32)
        parts.append(tot / cnt - cur)
    y = _pool_project(parts, pw_ref, pb_ref, ps_ref)
    out_ref[...] = _layer_norm(DN_ALPHA * x_ref[...] + y, lng_ref[...], lnb_ref[...])


def _prompt_pool(h2d, pw_bf, pb, ps, lng, lnb, n, s, tm):
    tps = s // tm
    hb = tm // POOL_HALO
    row = lambda b, t: (b * tps + t, 0)
    ng = len(POOL_WINDOWS)
    return pl.pallas_call(
        _pool_kernel,
        grid=(n, tps),
        in_specs=[
            pl.BlockSpec((tm, D_MODEL), row),
            pl.BlockSpec((POOL_HALO, D_MODEL),
                         lambda b, t: (jnp.maximum((b * tps + t) * hb - 1, 0), 0)),
            _const_spec((ng, POOL_CH, POOL_CH)),
            _const_spec((ng, POOL_CH)),
            _const_spec((1, D_MODEL)), _const_spec((1, D_MODEL)), _const_spec((1, D_MODEL)),
        ],
        out_specs=pl.BlockSpec((tm, D_MODEL), row),
        out_shape=jax.ShapeDtypeStruct((n * s, D_MODEL), F32),
        scratch_shapes=[pltpu.VMEM((POOL_HALO + tm, D_MODEL), F32),
                        pltpu.VMEM((POOL_LEVELS, POOL_HALO + tm, D_MODEL), F32)],
        compiler_params=_params(("parallel", "parallel")),
        name="pool_mix",
    )(h2d, h2d, pw_bf, pb, ps, lng, lnb)


def _dec_proj_kernel(x_ref, w_ref, b_ref, cs_ref, qkvt_ref, glu_ref):
    xb = x_ref[...].astype(BF16)
    cos, sin = cs_ref[0], cs_ref[1]
    chunk = ATTN_WIDTH

    def proj(c0, width):
        return (jnp.dot(xb, w_ref[:, c0:c0 + width], preferred_element_type=F32)
                + b_ref[:, c0:c0 + width])

    for ci in range(QKV_COLS // chunk):
        pt = proj(ci * chunk, chunk).T
        if ci < 2 * N_DIL:
            pieces = []
            for hh in range(HEADS):
                base = hh * HEAD_DIM
                x1 = pt[base:base + ROT_HALF, :]
                x2 = pt[base + ROT_HALF:base + ROT_DIM, :]
                pieces += [x1 * cos - x2 * sin, x2 * cos + x1 * sin, pt[base + ROT_DIM:base + HEAD_DIM, :]]
            pt = jnp.concatenate(pieces, axis=0)
            if ci < N_DIL:
                pt = pt * ATTN_SCALE
        qkvt_ref[ci * chunk:(ci + 1) * chunk, :] = pt
    ga = proj(QKV_COLS, CONV_CH)
    gb = proj(QKV_COLS + CONV_CH, CONV_CH)
    glu_ref[...] = ga * jax.nn.sigmoid(gb)


def _decode_project(x2d, w_bf, b2d, cs):
    nseq = x2d.shape[0]
    args = (x2d, w_bf, b2d, cs)
    return pl.pallas_call(
        _dec_proj_kernel,
        grid=(1,),
        in_specs=[_const_spec(a.shape) for a in args],
        out_specs=[_const_spec((QKV_COLS, nseq)), _const_spec((nseq, CONV_CH))],
        out_shape=[jax.ShapeDtypeStruct((QKV_COLS, nseq), F32),
                   jax.ShapeDtypeStruct((nseq, CONV_CH), F32)],
        compiler_params=_params(("arbitrary",)),
        name="decode_proj",
    )(*args)


def _cache_block(c_ref, nc_ref, ot_ref, lt_ref, qkvt_ref, g, seq, head0):
    hb, w = c_ref.shape[2], c_ref.shape[4]
    nseq = qkvt_ref.shape[1]
    rate = DIL_RATES[g]
    rows = hb * HEAD_DIM
    r0 = pl.multiple_of(head0 * HEAD_DIM, rows)
    mine = lax.broadcasted_iota(jnp.int32, (rows, nseq), 1) == seq

    def column(base):
        x = qkvt_ref[pl.ds(base + r0, rows), :]
        return jnp.sum(jnp.where(mine, x, 0.0), axis=1, keepdims=True).reshape(hb, HEAD_DIM, 1)

    qc = column(g * ATTN_WIDTH)
    kc = column((N_DIL + g) * ATTN_WIDTH)
    vc = column((2 * N_DIL + g) * ATTN_WIDTH)
    kt = c_ref[0, 0]
    vt = c_ref[0, 1]
    tok = lax.broadcasted_iota(jnp.int32, (1, 1, w), 2)
    in_window = (tok & (rate - 1)) == 0
    newest = tok == w - 1

    sc = jnp.where(in_window, jnp.sum(kt * qc, axis=1, keepdims=True), -jnp.inf)
    sn = jnp.sum(qc * kc, axis=1, keepdims=True)
    m = jnp.maximum(jnp.max(sc, axis=2, keepdims=True), sn)
    p = jnp.exp(sc - m)
    pn = jnp.exp(sn - m)
    l = jnp.sum(p, axis=2, keepdims=True) + pn
    inv = 1.0 / l
    o = jnp.sum(vt * (p * inv), axis=2, keepdims=True) + vc * (pn * inv)
    lse = jnp.broadcast_to(m + jnp.log(l), (hb, HEAD_DIM, 1))
    acc_rows = pl.ds(r0, rows)
    ot_ref[acc_rows, :] = jnp.where(mine, o.reshape(rows, 1), ot_ref[acc_rows, :])
    lt_ref[acc_rows, :] = jnp.where(mine, lse.reshape(rows, 1), lt_ref[acc_rows, :])

    def shifted(old, new_col):
        rolled = pltpu.roll(old.reshape(rows, w), w - 1, 1).reshape(hb, HEAD_DIM, w)
        return jnp.where(newest, new_col, rolled)

    nc_ref[0, 0] = shifted(kt, kc)
    nc_ref[0, 1] = shifted(vt, vc)


def _route(hb, rw_ref, rb_ref, ct_ref):
    logits = jnp.dot(hb, rw_ref[...], preferred_element_type=F32)
    lt = logits.T
    comb_rows = _routing_rows([lt[e:e + 1, :] for e in range(N_EXPERTS)], rb_ref)
    ct_ref[...] = jnp.zeros_like(ct_ref)
    for e in range(N_EXPERTS):
        ct_ref[e:e + 1, :] = comb_rows[e]
    return ct_ref[...].T


def _experts_step(hb, comb, k, w1_ref, w3_ref, w2_ref, resident):
    lane = lax.broadcasted_iota(jnp.int32, comb.shape, 1)
    ffn = None
    for ee in range(EXPERTS_PER_SUBSTEP):
        e = k * EXPERTS_PER_SUBSTEP + ee
        we = e if resident else ee
        a = jnp.dot(hb, w1_ref[we], preferred_element_type=F32)
        b = jnp.dot(hb, w3_ref[we], preferred_element_type=F32)
        gate = jnp.sum(jnp.where(lane == e, comb, 0.0), axis=1, keepdims=True)
        gated = (a * jax.nn.sigmoid(a)) * b * gate
        part = jnp.dot(gated.astype(BF16), w2_ref[ee], preferred_element_type=F32)
        ffn = part if ffn is None else ffn + part
    return ffn


def _moe_stream_kernel(*refs, seq_base):
    ng = N_DIL
    (h_ref, rw_ref, rb_ref, w1_ref, w3_ref, w2_ref, lng_ref, lnb_ref, qkvt_ref) = refs[:MOE_FIXED_INPUTS]
    c_refs = refs[MOE_FIXED_INPUTS:MOE_FIXED_INPUTS + ng]
    n_in = MOE_FIXED_INPUTS + ng + (ng if seq_base else 0)
    out_ref = refs[n_in]
    stream_out = refs[n_in + 1:n_in + 1 + 3 * ng]
    comb_ref, ct_ref = refs[n_in + 1 + 3 * ng:]
    i, k = pl.program_id(0), pl.program_id(1)
    flat = i * MOE_SUBSTEPS + k
    hb = h_ref[...].astype(BF16)

    @pl.when(k == 0)
    def _():
        comb_ref[...] = _route(hb, rw_ref, rb_ref, ct_ref)
        out_ref[...] = jnp.zeros_like(out_ref)

    @pl.when(flat == 0)
    def _():
        for s in range(ng):
            for t in range(2):
                acc = stream_out[3 * s + 1 + t]
                acc[...] = jnp.zeros_like(acc)

    out_ref[...] += _experts_step(hb, comb_ref[...], k, w1_ref, w3_ref, w2_ref, resident=True)

    blocks_per_seq = HEADS // CACHE_HEAD_BLOCK
    for g in range(ng):
        _cache_block(c_refs[g], stream_out[3 * g], stream_out[3 * g + 1], stream_out[3 * g + 2], qkvt_ref,
                     g, seq_base + flat // blocks_per_seq, (flat % blocks_per_seq) * CACHE_HEAD_BLOCK)

    @pl.when(k == MOE_SUBSTEPS - 1)
    def _():
        out_ref[...] = _layer_norm(DN_ALPHA * h_ref[...] + out_ref[...], lng_ref[...], lnb_ref[...])


def _moe_stream(h2d, rw_pad, rb, w1, w3, w2, layer, lng, lnb, qkvt, caches_t, tm, seq_base, prev):
    m = h2d.shape[0]
    nseq = qkvt.shape[1]
    blocks_per_seq = HEADS // CACHE_HEAD_BLOCK
    first_block = seq_base * blocks_per_seq
    assert len(caches_t) == N_DIL and ((m // tm) * MOE_SUBSTEPS) % blocks_per_seq == 0
    assert first_block + (m // tm) * MOE_SUBSTEPS <= nseq * blocks_per_seq
    row = lambda i, k: (i, 0)
    w13_spec = pl.BlockSpec((None, N_EXPERTS, D_MODEL, D_EXPERT), lambda i, k: (layer, 0, 0, 0),
                            pipeline_mode=pl.Buffered(1))
    acc_spec = _const_spec((ATTN_WIDTH, nseq))
    acc_sds = jax.ShapeDtypeStruct((ATTN_WIDTH, nseq), F32)

    def block_index(i, k):
        b = first_block + i * MOE_SUBSTEPS + k
        return (b // blocks_per_seq, 0, b % blocks_per_seq, 0, 0)

    cache_specs, stream_specs, stream_shapes = [], [], []
    for c in caches_t:
        spec = pl.BlockSpec((1, 2, CACHE_HEAD_BLOCK, HEAD_DIM, c.shape[4]), block_index)
        cache_specs.append(spec)
        stream_specs += [spec, acc_spec, acc_spec]
        stream_shapes += [jax.ShapeDtypeStruct(c.shape, F32), acc_sds, acc_sds]
    extra_in, extra_specs, aliases = [], [], {}
    if seq_base:
        n_fixed = MOE_FIXED_INPUTS + len(caches_t)
        for s, nc in enumerate(prev):
            aliases[n_fixed + s] = 1 + 3 * s
            extra_in.append(nc)
            extra_specs.append(pl.BlockSpec(memory_space=pl.ANY))
    return pl.pallas_call(
        functools.partial(_moe_stream_kernel, seq_base=seq_base),
        grid=(m // tm, MOE_SUBSTEPS),
        in_specs=[
            pl.BlockSpec((tm, D_MODEL), row),
            _const_spec((D_MODEL, LANES)),
            _const_spec((N_EXPERTS, 1)),
            w13_spec,
            w13_spec,
            pl.BlockSpec((None, EXPERTS_PER_SUBSTEP, D_EXPERT, D_MODEL), lambda i, k: (layer, k, 0, 0)),
            _const_spec((1, D_MODEL)), _const_spec((1, D_MODEL)),
            _resident_spec(qkvt.shape),
        ] + cache_specs + extra_specs,
        out_specs=[pl.BlockSpec((tm, D_MODEL), row)] + stream_specs,
        out_shape=[jax.ShapeDtypeStruct((m, D_MODEL), F32)] + stream_shapes,
        input_output_aliases=aliases,
        scratch_shapes=[pltpu.VMEM((tm, LANES), F32), pltpu.VMEM((LANES, tm), F32)],
        compiler_params=pltpu.CompilerParams(dimension_semantics=("arbitrary", "arbitrary"),
                                             vmem_limit_bytes=STREAM_VMEM_LIMIT_BYTES),
        name=f"moe_stream_from{seq_base}",
    )(h2d, rw_pad, rb, w1, w3, w2, lng, lnb, qkvt, *caches_t, *extra_in)


def _sample_tail_kernel(x_ref, *refs, split):
    acc_refs = refs[:4 * N_DIL]
    (cst_ref, glu_ref, cw_ref, cb_ref, clg_ref, clb_ref, wo_ref, lmg_ref, lmb_ref,
     pst_ref, pw_ref, pb_ref, ps_ref, rw_ref, rb_ref, w1_ref, w3_ref, w2_ref, lfg_ref, lfb_ref,
     y_ref, ncst_ref, npst_ref, h_ref, ffn_ref, comb_ref, ct_ref) = refs[4 * N_DIL:]
    s = pl.program_id(0)
    layer, k = s // MOE_SUBSTEPS, s % MOE_SUBSTEPS

    def start_layer(h):
        h_ref[...] = h
        ffn_ref[...] = jnp.zeros_like(ffn_ref)
        comb_ref[...] = _route(h.astype(BF16), rw_ref, rb_ref, ct_ref)

    @pl.when(s == 0)
    def _():
        npre = CONV_WIDTH - 1
        glu = glu_ref[...]
        acc = cw_ref[npre:npre + 1, :] * glu
        for j in range(npre):
            acc = acc + cw_ref[j:j + 1, :] * cst_ref[j]
        conv = _conv_tail(acc, cb_ref[...], clg_ref[...], clb_ref[...])
        first = lax.broadcasted_iota(jnp.int32, (ATTN_WIDTH, x_ref.shape[0]), 1) < split

        def both(idx):
            return jnp.where(first, acc_refs[idx][...], acc_refs[2 * N_DIL + idx][...])

        os_ = [both(g) for g in range(N_DIL)]
        w = _group_weights([both(N_DIL + g) for g in range(N_DIL)])
        attn = (w[0] * os_[0] + w[1] * os_[1] + w[2] * os_[2]).T
        start_layer(_out_proj_ln(x_ref[...], attn, conv, wo_ref, lmg_ref[0:1, :], lmb_ref[0:1, :]))
        ncst_ref[0:npre - 1] = cst_ref[1:npre]
        ncst_ref[npre - 1] = glu

    @pl.when(s == MOE_SUBSTEPS)
    def _():
        x = h_ref[...]
        parts = []
        for gi, w in enumerate(POOL_WINDOWS):
            lo = gi * POOL_CH
            cur = x[:, lo:lo + POOL_CH]
            tot = cur
            for j in range(1, w):
                tot = tot + pst_ref[POOL_PREFIX - j, :, lo:lo + POOL_CH]
            parts.append(tot / float(w) - cur)
        y = _pool_project(parts, pw_ref, pb_ref, ps_ref)
        start_layer(_layer_norm(DN_ALPHA * x + y, lmg_ref[1:2, :], lmb_ref[1:2, :]))
        npst_ref[0:POOL_PREFIX - 1] = pst_ref[1:POOL_PREFIX]
        npst_ref[POOL_PREFIX - 1] = x

    ffn_ref[...] += _experts_step(h_ref[...].astype(BF16), comb_ref[...], k, w1_ref, w3_ref, w2_ref,
                                  resident=False)

    @pl.when(k == MOE_SUBSTEPS - 1)
    def _():
        out = _layer_norm(DN_ALPHA * h_ref[...] + ffn_ref[...],
                          lfg_ref[pl.ds(layer, 1), :], lfb_ref[pl.ds(layer, 1), :])
        h_ref[...] = out
        y_ref[...] = out


def _sample_tail(x2d, accs, split, conv_t, glu, conv_w, conv_b, clg, clb, wo_bf, ln_mix_g, ln_mix_b,
                 pool_t, pw_bf, pb, ps, rw_pad, rb, w1, w3, w2, ln_ffn_g, ln_ffn_b):
    assert DEPTH == 2
    nseq = x2d.shape[0]
    consts = (x2d, *accs, conv_t, glu, conv_w, conv_b, clg, clb, wo_bf, ln_mix_g, ln_mix_b,
              pool_t, pw_bf, pb, ps, rw_pad, rb)
    step_block = lambda s: (s // MOE_SUBSTEPS, s % MOE_SUBSTEPS, 0, 0)
    w13_spec = pl.BlockSpec((None, EXPERTS_PER_SUBSTEP, D_MODEL, D_EXPERT), step_block)
    w2_spec = pl.BlockSpec((None, EXPERTS_PER_SUBSTEP, D_EXPERT, D_MODEL), step_block)
    return pl.pallas_call(
        functools.partial(_sample_tail_kernel, split=split),
        grid=(DEPTH * MOE_SUBSTEPS,),
        in_specs=([_const_spec(a.shape) for a in consts] + [w13_spec, w13_spec, w2_spec]
                  + [_const_spec(ln_ffn_g.shape), _const_spec(ln_ffn_b.shape)]),
        out_specs=[_const_spec((nseq, D_MODEL)), _const_spec(conv_t.shape), _const_spec(pool_t.shape)],
        out_shape=[jax.ShapeDtypeStruct((nseq, D_MODEL), F32),
                   jax.ShapeDtypeStruct(conv_t.shape, F32),
                   jax.ShapeDtypeStruct(pool_t.shape, F32)],
        scratch_shapes=[pltpu.VMEM((nseq, D_MODEL), F32), pltpu.VMEM((nseq, D_MODEL), F32),
                        pltpu.VMEM((nseq, LANES), F32), pltpu.VMEM((LANES, nseq), F32)],
        compiler_params=_params(("arbitrary",)),
        name="sample_tail",
    )(*consts, w1, w3, w2, ln_ffn_g, ln_ffn_b)


def _token_minor(a):
    return jnp.transpose(a, (0, 2, 3, 4, 1))


def _token_major(a):
    return jnp.transpose(a, (0, 4, 1, 2, 3))


def kernel(x_prompt, x_sample, cache_attn_w128, cache_attn_w512, cache_attn_w2048, state_conv, state_pool,
           w_in, b_in, conv_w, conv_b, conv_ln_g, conv_ln_b, w_out, pool_w, pool_b, pool_scale,
           ln_mix_g, ln_mix_b, ln_ffn_g, ln_ffn_b, router_w, router_bias, moe_w1, moe_w3, moe_w2):
    n, s, d = x_prompt.shape
    nseq = x_sample.shape[0]
    past = cache_attn_w2048.shape[2]
    assert d == D_MODEL and x_sample.shape[1] == 1 and s % ROW_TILE == 0 and s == DIL_WINDOWS[2]
    assert cache_attn_w128.shape[0] == 1 and past == DIL_WINDOWS[2]
    caches = (cache_attn_w128[0], cache_attn_w512[0], cache_attn_w2048[0])

    w_in_bf = w_in[0].astype(BF16)
    w_out_bf = w_out[0].astype(BF16)
    pool_w_bf = pool_w[0].astype(BF16)
    moe_bf = tuple(w.astype(BF16) for w in (moe_w1, moe_w3, moe_w2))
    rw_pad = jnp.pad(router_w, ((0, 0), (0, LANES - N_EXPERTS))).astype(BF16)
    rb = router_bias.astype(F32).reshape(N_EXPERTS, 1)
    r2 = lambda v: v.reshape(1, -1)
    b_in2 = r2(b_in[0])
    cb, clg, clb = r2(conv_b[0]), r2(conv_ln_g[0]), r2(conv_ln_b[0])
    ps = r2(pool_scale[0])

    xs = x_sample.reshape(nseq, d)
    ang = _rope_angles(past + jnp.arange(1, dtype=jnp.int32))
    cs = jnp.stack([jnp.broadcast_to(jnp.cos(ang).T, (ROT_HALF, nseq)),
                    jnp.broadcast_to(jnp.sin(ang).T, (ROT_HALF, nseq))])
    qkvt, glus = _decode_project(xs, w_in_bf, b_in2, cs)
    caches_t = [_token_minor(c) for c in caches]

    seqs_per_layer = (n * s // ROW_TILE) * MOE_SUBSTEPS * CACHE_HEAD_BLOCK // HEADS
    assert DEPTH * seqs_per_layer == nseq

    def moe_stream(h2d, layer, prev):
        res = _moe_stream(h2d, rw_pad, rb, *moe_bf, layer, r2(ln_ffn_g[layer]), r2(ln_ffn_b[layer]),
                          qkvt, caches_t, ROW_TILE, layer * seqs_per_layer, prev)
        return res[0], [tuple(res[1 + 3 * g:4 + 3 * g]) for g in range(N_DIL)]

    xp = x_prompt.reshape(n * s, d)
    tabs_p = _rope_tables(jnp.arange(s, dtype=jnp.int32))
    (q0, q1, q2, kv0, kv1, kv2, glu, kt0, kt1, kt2) = _project(xp, w_in_bf, b_in2, tabs_p, n, s, ROW_TILE)
    os_, lses = [], []
    for g, (q, kv) in enumerate(((q0, kv0), (q1, kv1), (q2, kv2))):
        o, l = _prompt_attention(q, kv, g, n, s)
        os_.append(o)
        lses.append(l)
    h = _prompt_mix(xp, os_, lses, glu, conv_w[0], cb, clg, clb, w_out_bf,
                    r2(ln_mix_g[0]), r2(ln_mix_b[0]), n, s, ROW_TILE)
    h, first_pass = moe_stream(h, 0, None)
    p_pool = h.reshape(n, s, d)[:, s - POOL_PREFIX:][None]
    h = _prompt_pool(h, pool_w_bf, pool_b[0], ps, r2(ln_mix_g[1]), r2(ln_mix_b[1]), n, s, ROW_TILE)
    h, second_pass = moe_stream(h, 1, [st[0] for st in first_pass])
    y_prompt = h.reshape(n, s, d)
    p_attn = [_token_major(kt.reshape(n, 2, HEADS, HEAD_DIM, kt.shape[2]))[None] for kt in (kt0, kt1, kt2)]
    p_conv = glu.reshape(n, s, CONV_CH)[:, s - (CONV_WIDTH - 1):][None]

    accs = []
    for streamed in (first_pass, second_pass):
        accs += [st[1] for st in streamed] + [st[2] for st in streamed]
    s_attn = [_token_major(st[0])[None] for st in second_pass]
    conv_t = jnp.transpose(state_conv[0], (1, 0, 2))
    pool_t = jnp.transpose(state_pool[0], (1, 0, 2))
    ys, new_conv, new_pool = _sample_tail(
        xs, accs, seqs_per_layer, conv_t, glus, conv_w[0], cb, clg, clb, w_out_bf, ln_mix_g, ln_mix_b,
        pool_t, pool_w_bf, pool_b[0], ps, rw_pad, rb, *moe_bf, ln_ffn_g, ln_ffn_b)
    y_sample = ys.reshape(nseq, 1, d)
    s_conv = jnp.transpose(new_conv, (1, 0, 2))[None]
    s_pool = jnp.transpose(new_pool, (1, 0, 2))[None]

    return (y_prompt, y_sample, p_attn[0], p_attn[1], p_attn[2], p_conv, p_pool,
            s_attn[0], s_attn[1], s_attn[2], s_conv, s_pool)
```

```python
import functools

import jax
import jax.numpy as jnp
from jax import lax
from jax.experimental import pallas as pl
from jax.experimental.pallas import tpu as pltpu

F32 = jnp.float32
BF16 = jnp.bfloat16

D_MODEL = 1024
HEAD_DIM = 64
HEADS = 8
ATTN_WIDTH = HEADS * HEAD_DIM
N_DIL = 3
DIL_WINDOWS = (128, 512, 2048)
DIL_RATES = (1, 4, 16)
ATTN_BLOCK = 128
ATTN_SCALE = HEAD_DIM ** -0.5
ROT_DIM = HEAD_DIM // 4
ROT_HALF = ROT_DIM // 2
ROPE_THETA = 500000.0
QKV_COLS = 3 * N_DIL * ATTN_WIDTH
CONV_CH = D_MODEL // 4
CONV_WIDTH = 31
IN_COLS = QKV_COLS + 2 * CONV_CH
POOL_WINDOWS = (2, 4, 8, 16)
POOL_CH = D_MODEL // len(POOL_WINDOWS)
POOL_PREFIX = max(POOL_WINDOWS) - 1
N_EXPERTS = 16
N_EXPERT_GROUPS = 4
EXPERTS_PER_GROUP = 4
D_EXPERT = 256
DEPTH = 2
DN_ALPHA = (2.0 * DEPTH) ** 0.25
LN_EPS = 1e-5

LANES = 128
SUBLANES = 8
VMEM_LIMIT_BYTES = 56 * 1024 * 1024
ROW_TILE = 512
CONV_HALO = 32
CONV_SLACK = -(-(CONV_HALO - (CONV_WIDTH - 1) + SUBLANES) // SUBLANES) * SUBLANES
POOL_LEVELS = len(POOL_WINDOWS) - 1
POOL_HALO = SUBLANES * (POOL_LEVELS + 1)
assert POOL_WINDOWS[-1] // 2 == SUBLANES and POOL_HALO >= POOL_PREFIX
STREAM_VMEM_LIMIT_BYTES = 60000 * 1024
MOE_SUBSTEPS = 4
EXPERTS_PER_SUBSTEP = N_EXPERTS // MOE_SUBSTEPS
MOE_FIXED_INPUTS = 9
CACHE_HEAD_BLOCK = 4
HEAD_PAIRS = ATTN_WIDTH // LANES
ATTN_UNITS_PER_STAGE = 8

NT_DIMS = (((1,), (1,)), ((), ()))


def _params(sem):
    return pltpu.CompilerParams(dimension_semantics=sem, vmem_limit_bytes=VMEM_LIMIT_BYTES)


def _const_spec(shape):
    nd = len(shape)
    return pl.BlockSpec(shape, lambda *_: (0,) * nd)


def _resident_spec(shape):
    nd = len(shape)
    return pl.BlockSpec(shape, lambda *_: (0,) * nd, pipeline_mode=pl.Buffered(1))


def _layer_norm(x, g, b):
    mu = jnp.mean(x, axis=-1, keepdims=True)
    xc = x - mu
    var = jnp.mean(xc * xc, axis=-1, keepdims=True)
    return xc * lax.rsqrt(var + LN_EPS) * g + b


def _proj_kernel(x_ref, w_ref, b_ref, tab_ref, m1_ref, m3_ref, m2_ref, q0_ref, q1_ref, q2_ref,
                 kv0_ref, kv1_ref, kv2_ref, glu_ref, kt0_ref, kt1_ref, kt2_ref, m1b_ref, m3b_ref, m2b_ref):
    for src, dst in ((m1_ref, m1b_ref), (m3_ref, m3b_ref), (m2_ref, m2b_ref)):
        dst[...] = src[...].astype(BF16)
    tm = x_ref.shape[0]
    last = pl.program_id(1) == pl.num_programs(1) - 1
    xb = x_ref[...].astype(BF16)
    cosm, sin_lo, sin_hi = tab_ref[0], tab_ref[1], tab_ref[2]

    def proj(c0, width):
        return (jnp.dot(xb, w_ref[:, c0:c0 + width], preferred_element_type=F32)
                + b_ref[:, c0:c0 + width])

    def rope(t):
        parts = []
        for j in range(t.shape[1] // LANES):
            v = t[:, j * LANES:(j + 1) * LANES]
            parts.append(v * cosm
                         + pltpu.roll(v, LANES - ROT_HALF, 1) * sin_lo
                         + pltpu.roll(v, ROT_HALF, 1) * sin_hi)
        return jnp.concatenate(parts, axis=1)

    q_refs = (q0_ref, q1_ref, q2_ref)
    kv_refs = (kv0_ref, kv1_ref, kv2_ref)
    for g in range(N_DIL):
        c = g * ATTN_WIDTH
        q = rope(proj(c, ATTN_WIDTH)) * ATTN_SCALE
        k = rope(proj(N_DIL * ATTN_WIDTH + c, ATTN_WIDTH))
        v = proj(2 * N_DIL * ATTN_WIDTH + c, ATTN_WIDTH)
        for hp in range(HEAD_PAIRS):
            cols = slice(hp * LANES, (hp + 1) * LANES)
            q_refs[g][hp] = q[:, cols]
            kv_refs[g][hp] = k[:, cols]
            kv_refs[g][HEAD_PAIRS + hp] = v[:, cols]
        if g == 2:
            kt2_ref[0, 0:ATTN_WIDTH, :] = k.T
            kt2_ref[0, ATTN_WIDTH:2 * ATTN_WIDTH, :] = v.T
    ga = proj(QKV_COLS, CONV_CH)
    gb = proj(QKV_COLS + CONV_CH, CONV_CH)
    glu_ref[...] = ga * jax.nn.sigmoid(gb)

    @pl.when(last)
    def _():
        for kt_ref, kv_ref, keep in ((kt0_ref, kv0_ref, DIL_WINDOWS[0]), (kt1_ref, kv1_ref, DIL_WINDOWS[1])):
            for j in range(2 * HEAD_PAIRS):
                kt_ref[0, j * LANES:(j + 1) * LANES, :] = kv_ref[j, tm - keep:, :].T


def _project(x2d, w_bf, b2d, tables, moe_w, n, s, tm):
    assert tm == DIL_WINDOWS[1] and tm >= DIL_WINDOWS[0]
    tps = s // tm
    m = n * s
    n_mats = DEPTH * N_EXPERTS
    assert n_mats % (n * tps) == 0
    per_step = n_mats // (n * tps)
    flat_w = [w.reshape(n_mats, *w.shape[2:]) for w in moe_w]
    w_specs = [pl.BlockSpec((per_step, *w.shape[1:]), lambda b, t: (b * tps + t, 0, 0)) for w in flat_w]
    row = lambda b, t: (b * tps + t, 0)
    seq = lambda b, t: (b, 0, 0)
    kv_rows = 2 * ATTN_WIDTH
    slab = lambda b, t: (0, b * tps + t, 0)
    out_specs = ([pl.BlockSpec((HEAD_PAIRS, tm, LANES), slab)] * N_DIL
                 + [pl.BlockSpec((2 * HEAD_PAIRS, tm, LANES), slab)] * N_DIL
                 + [pl.BlockSpec((tm, CONV_CH), row),
                    pl.BlockSpec((1, kv_rows, DIL_WINDOWS[0]), seq),
                    pl.BlockSpec((1, kv_rows, DIL_WINDOWS[1]), seq),
                    pl.BlockSpec((1, kv_rows, tm), lambda b, t: (b, 0, t))])
    out_shape = ([jax.ShapeDtypeStruct((HEAD_PAIRS, m, LANES), F32)] * N_DIL
                 + [jax.ShapeDtypeStruct((2 * HEAD_PAIRS, m, LANES), F32)] * N_DIL
                 + [jax.ShapeDtypeStruct((m, CONV_CH), F32),
                    jax.ShapeDtypeStruct((n, kv_rows, DIL_WINDOWS[0]), F32),
                    jax.ShapeDtypeStruct((n, kv_rows, DIL_WINDOWS[1]), F32),
                    jax.ShapeDtypeStruct((n, kv_rows, s), F32)])
    res = pl.pallas_call(
        _proj_kernel,
        grid=(n, tps),
        in_specs=[
            pl.BlockSpec((tm, D_MODEL), row),
            _resident_spec((D_MODEL, IN_COLS)),
            _const_spec((1, IN_COLS)),
            pl.BlockSpec((3, tm, LANES), lambda b, t: (0, t, 0)),
        ] + w_specs,
        out_specs=out_specs + w_specs,
        out_shape=out_shape + [jax.ShapeDtypeStruct(w.shape, BF16) for w in flat_w],
        compiler_params=_params(("parallel", "arbitrary")),
        name="in_proj",
    )(x2d, w_bf, b2d, tables, *flat_w)
    n_proj = len(out_shape)
    return res[:n_proj], [wb.reshape(w.shape) for wb, w in zip(res[n_proj:], moe_w)]


def _rope_angles(pos):
    inv_freq = ROPE_THETA ** (-jnp.arange(ROT_HALF, dtype=F32) * 2.0 / ROT_DIM)
    return pos.astype(F32)[:, None] * inv_freq[None, :]


def _rope_tables(pos):
    t = pos.shape[0]
    ang = _rope_angles(pos)
    cos, sin = jnp.cos(ang), jnp.sin(ang)
    rest = HEAD_DIM - ROT_DIM
    c64 = jnp.concatenate([cos, cos, jnp.ones((t, rest), F32)], axis=1)
    lo64 = jnp.concatenate([-sin, jnp.zeros((t, HEAD_DIM - ROT_HALF), F32)], axis=1)
    hi64 = jnp.concatenate([jnp.zeros((t, ROT_HALF), F32), sin, jnp.zeros((t, rest), F32)], axis=1)
    rep = LANES // HEAD_DIM
    return jnp.stack([jnp.tile(c64, (1, rep)), jnp.tile(lo64, (1, rep)), jnp.tile(hi64, (1, rep))])


def _attn_kernel(q_ref, kv_ref, bias_ref, o_ref, lse_ref, s_ref, p_ref, *, rate, nblk):
    nk = s_ref.shape[3]
    lane = lax.broadcasted_iota(jnp.int32, (ATTN_BLOCK, LANES), 1)
    low_half = lane < HEAD_DIM
    keep_lo = low_half.astype(BF16)
    keep_hi = 1 - keep_lo
    ones_rhs = jnp.ones((nk, LANES), BF16)

    def strided(start, size):
        return pl.ds(start, size, stride=rate) if rate > 1 else pl.ds(start, size)

    def unit_rows(u):
        res, b = (u // nblk, u % nblk) if nblk > 1 else (u, 0)
        rows_q = strided(b * (ATTN_BLOCK * rate) + res, ATTN_BLOCK)
        rows_k = strided(jnp.maximum(b - 1, 0) * (ATTN_BLOCK * rate) + res, nk)
        return rows_q, rows_k, bias_ref[jnp.minimum(b, 1)]

    def scores(slot, rows_q, rows_k, bias):
        for hp in range(HEAD_PAIRS):
            qp = q_ref[hp, rows_q, :].astype(BF16)
            kp = kv_ref[hp, rows_k, :].astype(BF16)
            for half, keep in enumerate((keep_lo, keep_hi)):
                s = lax.dot_general(qp * keep, kp, NT_DIMS, preferred_element_type=F32)
                s_ref[slot, 2 * hp + half] = s + bias

    def values(slot, rows_q, rows_k, m):
        for hp in range(HEAD_PAIRS):
            vp = kv_ref[HEAD_PAIRS + hp, rows_k, :].astype(BF16)
            rhs = jnp.concatenate([vp, ones_rhs], axis=1)
            ol_lo = jnp.dot(p_ref[slot, 2 * hp], rhs, preferred_element_type=F32)
            ol_hi = jnp.dot(p_ref[slot, 2 * hp + 1], rhs, preferred_element_type=F32)
            l_lo, l_hi = ol_lo[:, LANES:], ol_hi[:, LANES:]
            o_pair = jnp.where(low_half, ol_lo[:, :LANES] * (1.0 / l_lo), ol_hi[:, :LANES] * (1.0 / l_hi))
            lse_pair = jnp.where(low_half, m[2 * hp] + jnp.log(l_lo), m[2 * hp + 1] + jnp.log(l_hi))
            o_ref[hp, rows_q, :] = o_pair
            lse_ref[hp, rows_q, :] = lse_pair

    nslot = s_ref.shape[0]

    def unit_group(i, carry):
        units = [unit_rows(nslot * i + j) for j in range(nslot)]
        for j, u in enumerate(units):
            scores(j, *u)
        sc = s_ref[...]
        m = jnp.max(sc, axis=-1, keepdims=True)
        p_ref[...] = jnp.exp(sc - m).astype(BF16)
        for j, u in enumerate(units):
            values(j, u[0], u[1], m[j])
        return carry

    lax.fori_loop(0, rate * nblk // nslot, unit_group, 0)


def _band_bias(nblk):
    qi = jnp.arange(ATTN_BLOCK)[:, None]
    ci = jnp.arange(ATTN_BLOCK)[None, :]
    causal = jnp.where(ci <= qi, 0.0, -jnp.inf).astype(F32)
    if nblk == 1:
        return jnp.stack([causal, causal])
    band = jnp.where(ci >= qi, 0.0, -jnp.inf).astype(F32)
    closed = jnp.full((ATTN_BLOCK, ATTN_BLOCK), -jnp.inf, F32)
    return jnp.stack([jnp.concatenate([causal, closed], axis=1), jnp.concatenate([band, causal], axis=1)])


def _prompt_attention(q, kv, g, n, s):
    rate = DIL_RATES[g]
    nblk = s // rate // ATTN_BLOCK
    nk = 2 * ATTN_BLOCK if nblk > 1 else ATTN_BLOCK
    out_sds = jax.ShapeDtypeStruct((HEAD_PAIRS, n * s, LANES), F32)
    seq = lambda b: (0, b, 0)
    o_spec = pl.BlockSpec((HEAD_PAIRS, s, LANES), seq)
    return pl.pallas_call(
        functools.partial(_attn_kernel, rate=rate, nblk=nblk),
        grid=(n,),
        in_specs=[
            o_spec,
            pl.BlockSpec((2 * HEAD_PAIRS, s, LANES), seq),
            _const_spec((2, ATTN_BLOCK, nk)),
        ],
        out_specs=[o_spec, o_spec],
        out_shape=[out_sds, out_sds],
        scratch_shapes=[pltpu.VMEM((ATTN_UNITS_PER_STAGE, HEADS, ATTN_BLOCK, nk), F32),
                        pltpu.VMEM((ATTN_UNITS_PER_STAGE, HEADS, ATTN_BLOCK, nk), BF16)],
        compiler_params=_params(("parallel",)),
        name=f"band_attn_g{g}",
    )(q, kv, _band_bias(nblk))


def _group_weights(lses):
    lmax = jnp.maximum(jnp.maximum(lses[0], lses[1]), lses[2])
    es = [jnp.exp(l - lmax) for l in lses]
    inv = 1.0 / (es[0] + es[1] + es[2])
    return [e * inv for e in es]


def _conv_tail(y, cb, clg, clb):
    z = _layer_norm(y + cb, clg, clb)
    return z * jax.nn.sigmoid(z)


def _out_proj_ln(x, attn, conv, wo_ref, lng, lnb):
    y = (jnp.dot(attn.astype(BF16), wo_ref[0:ATTN_WIDTH, :], preferred_element_type=F32)
         + jnp.dot(conv.astype(BF16), wo_ref[ATTN_WIDTH:ATTN_WIDTH + CONV_CH, :],
                   preferred_element_type=F32))
    return _layer_norm(DN_ALPHA * x + y, lng, lnb)


def _mix_kernel(x_ref, o0_ref, o1_ref, o2_ref, l0_ref, l1_ref, l2_ref, gc_ref, gp_ref,
                cw_ref, cb_ref, clg_ref, clb_ref, wo_ref, lng_ref, lnb_ref, h_ref, ext_ref, phase_ref):
    tm = x_ref.shape[0]
    first = pl.program_id(1) == 0
    ext_ref[0:CONV_HALO, :] = jnp.where(first, 0.0, gp_ref[...])
    ext_ref[CONV_HALO:CONV_HALO + tm, :] = gc_ref[...]
    ext_ref[CONV_HALO + tm:, :] = jnp.zeros((CONV_SLACK, CONV_CH), F32)
    base = CONV_HALO - (CONV_WIDTH - 1)
    span = tm + CONV_SLACK
    acc = None
    for r in range(SUBLANES):
        part = None
        for k in range(r, CONV_WIDTH, SUBLANES):
            term = cw_ref[k:k + 1, :] * ext_ref[k - r:k - r + span, :]
            part = term if part is None else part + term
        phase_ref[r] = part
        shifted = phase_ref[r, base + r:base + r + tm, :]
        acc = shifted if acc is None else acc + shifted
    conv = _conv_tail(acc, cb_ref[...], clg_ref[...], clb_ref[...])
    pieces = []
    for hp in range(HEAD_PAIRS):
        w = _group_weights((l0_ref[hp], l1_ref[hp], l2_ref[hp]))
        pieces.append(w[0] * o0_ref[hp] + w[1] * o1_ref[hp] + w[2] * o2_ref[hp])
    attn = jnp.concatenate(pieces, axis=1)
    h_ref[...] = _out_proj_ln(x_ref[...], attn, conv, wo_ref, lng_ref[...], lnb_ref[...])


def _prompt_mix(x2d, os_, lses, glu, conv_w, conv_b, clg, clb, wo_bf, lng, lnb, n, s, tm):
    tps = s // tm
    hb = tm // CONV_HALO
    row = lambda b, t: (b * tps + t, 0)
    aw = pl.BlockSpec((HEAD_PAIRS, tm, LANES), lambda b, t: (0, b * tps + t, 0))
    return pl.pallas_call(
        _mix_kernel,
        grid=(n, tps),
        in_specs=[
            pl.BlockSpec((tm, D_MODEL), row),
            aw, aw, aw, aw, aw, aw,
            pl.BlockSpec((tm, CONV_CH), row),
            pl.BlockSpec((CONV_HALO, CONV_CH),
                         lambda b, t: (jnp.maximum((b * tps + t) * hb - 1, 0), 0)),
            _const_spec((CONV_WIDTH, CONV_CH)),
            _const_spec((1, CONV_CH)), _const_spec((1, CONV_CH)), _const_spec((1, CONV_CH)),
            _const_spec((ATTN_WIDTH + CONV_CH, D_MODEL)),
            _const_spec((1, D_MODEL)), _const_spec((1, D_MODEL)),
        ],
        out_specs=pl.BlockSpec((tm, D_MODEL), row),
        out_shape=jax.ShapeDtypeStruct((n * s, D_MODEL), F32),
        scratch_shapes=[pltpu.VMEM((CONV_HALO + tm + CONV_SLACK, CONV_CH), F32),
                        pltpu.VMEM((SUBLANES, tm + CONV_SLACK, CONV_CH), F32)],
        compiler_params=_params(("parallel", "parallel")),
        name="mix_out",
    )(x2d, *os_, *lses, glu, glu, conv_w, conv_b, clg, clb, wo_bf, lng, lnb)


def _routing_rows(logit_rows, bias_ref):
    m = logit_rows[0]
    for r in logit_rows[1:]:
        m = jnp.maximum(m, r)
    ex = [jnp.exp(r - m) for r in logit_rows]
    tot = ex[0]
    for e in ex[1:]:
        tot = tot + e
    scores = [e / tot for e in ex]
    sel = [scores[e] + bias_ref[e:e + 1, :] for e in range(N_EXPERTS)]
    grp = []
    for g in range(N_EXPERT_GROUPS):
        v = sel[g * EXPERTS_PER_GROUP:(g + 1) * EXPERTS_PER_GROUP]
        best = v[0] + v[1]
        for i in range(EXPERTS_PER_GROUP):
            for j in range(i + 1, EXPERTS_PER_GROUP):
                if (i, j) != (0, 1):
                    best = jnp.maximum(best, v[i] + v[j])
        grp.append(best)
    gmax = grp[0]
    for v in grp[1:]:
        gmax = jnp.maximum(gmax, v)
    taken = None
    in_group = []
    for g in range(N_EXPERT_GROUPS):
        hit = grp[g] == gmax
        if taken is None:
            in_group.append(hit)
            taken = hit
        else:
            in_group.append(jnp.logical_and(hit, jnp.logical_not(taken)))
            taken = jnp.logical_or(taken, hit)
    gates = []
    for e in range(N_EXPERTS):
        g = e // EXPERTS_PER_GROUP
        rank = jnp.zeros_like(sel[e])
        for o in range(g * EXPERTS_PER_GROUP, (g + 1) * EXPERTS_PER_GROUP):
            if o == e:
                continue
            ahead = sel[o] > sel[e]
            if o < e:
                ahead = jnp.logical_or(ahead, sel[o] == sel[e])
            rank = rank + ahead.astype(F32)
        chosen = jnp.logical_and(in_group[g], rank < float(2))
        gates.append(jnp.where(chosen, scores[e], 0.0))
    den = gates[0]
    for v in gates[1:]:
        den = den + v
    return [v / den for v in gates]


def _pool_project(parts, pw_ref, pb_ref, ps_ref):
    cols = []
    for gi in range(len(POOL_WINDOWS)):
        lo = gi * POOL_CH
        y = jnp.dot(parts[gi].astype(BF16), pw_ref[gi], preferred_element_type=F32)
        cols.append((y + pb_ref[gi:gi + 1, :]) * ps_ref[:, lo:lo + POOL_CH])
    return jnp.concatenate(cols, axis=1)


def _pool_kernel(x_ref, xp_ref, pw_ref, pb_ref, ps_ref, lng_ref, lnb_ref, out_ref, ext_ref, lv_ref):
    tm = x_ref.shape[0]
    t = pl.program_id(1)
    ext_ref[0:POOL_HALO, :] = jnp.where(t == 0, 0.0, xp_ref[...])
    ext_ref[POOL_HALO:POOL_HALO + tm, :] = x_ref[...]
    end = POOL_HALO + tm
    top = POOL_LEVELS - 1
    src = ext_ref
    for i, w in enumerate(POOL_WINDOWS[:POOL_LEVELS]):
        r0, c0, shift = SUBLANES * (i + 1), i * POOL_CH, w // 2
        lv_ref[i, r0:end, c0:] = src[r0:end, c0:] + src[r0 - shift:end - shift, c0:]
        src = lv_ref.at[i]
    pos = t * tm + lax.broadcasted_iota(jnp.int32, (tm, 1), 0)
    parts = []
    for gi, w in enumerate(POOL_WINDOWS):
        lo = gi * POOL_CH
        cur = x_ref[:, lo:lo + POOL_CH]
        if gi <= top:
            tot = lv_ref[gi, POOL_HALO:end, lo:lo + POOL_CH]
        else:
            tot = (lv_ref[top, POOL_HALO:end, lo:lo + POOL_CH]
                   + lv_ref[top, POOL_HALO - w // 2:end - w // 2, lo:lo + POOL_CH])
        cnt = jnp.minimum(w, pos + 1).astype(F32)
        parts.append(tot / cnt - cur)
    y = _pool_project(parts, pw_ref, pb_ref, ps_ref)
    out_ref[...] = _layer_norm(DN_ALPHA * x_ref[...] + y, lng_ref[...], lnb_ref[...])


def _prompt_pool(h2d, pw_bf, pb, ps, lng, lnb, n, s, tm):
    tps = s // tm
    hb = tm // POOL_HALO
    row = lambda b, t: (b * tps + t, 0)
    ng = len(POOL_WINDOWS)
    return pl.pallas_call(
        _pool_kernel,
        grid=(n, tps),
        in_specs=[
            pl.BlockSpec((tm, D_MODEL), row),
            pl.BlockSpec((POOL_HALO, D_MODEL),
                         lambda b, t: (jnp.maximum((b * tps + t) * hb - 1, 0), 0)),
            _const_spec((ng, POOL_CH, POOL_CH)),
            _const_spec((ng, POOL_CH)),
            _const_spec((1, D_MODEL)), _const_spec((1, D_MODEL)), _const_spec((1, D_MODEL)),
        ],
        out_specs=pl.BlockSpec((tm, D_MODEL), row),
        out_shape=jax.ShapeDtypeStruct((n * s, D_MODEL), F32),
        scratch_shapes=[pltpu.VMEM((POOL_HALO + tm, D_MODEL), F32),
                        pltpu.VMEM((POOL_LEVELS, POOL_HALO + tm, D_MODEL), F32)],
        compiler_params=_params(("parallel", "parallel")),
        name="pool_mix",
    )(h2d, h2d, pw_bf, pb, ps, lng, lnb)


def _dec_proj_kernel(x_ref, w_ref, b_ref, cs_ref, qkvt_ref, glu_ref):
    xb = x_ref[...].astype(BF16)
    cos, sin = cs_ref[0], cs_ref[1]
    chunk = ATTN_WIDTH

    def proj(c0, width):
        return (jnp.dot(xb, w_ref[:, c0:c0 + width], preferred_element_type=F32)
                + b_ref[:, c0:c0 + width])

    for ci in range(QKV_COLS // chunk):
        pt = proj(ci * chunk, chunk).T
        if ci < 2 * N_DIL:
            pieces = []
            for hh in range(HEADS):
                base = hh * HEAD_DIM
                x1 = pt[base:base + ROT_HALF, :]
                x2 = pt[base + ROT_HALF:base + ROT_DIM, :]
                pieces += [x1 * cos - x2 * sin, x2 * cos + x1 * sin, pt[base + ROT_DIM:base + HEAD_DIM, :]]
            pt = jnp.concatenate(pieces, axis=0)
            if ci < N_DIL:
                pt = pt * ATTN_SCALE
        qkvt_ref[ci * chunk:(ci + 1) * chunk, :] = pt
    ga = proj(QKV_COLS, CONV_CH)
    gb = proj(QKV_COLS + CONV_CH, CONV_CH)
    glu_ref[...] = ga * jax.nn.sigmoid(gb)


def _decode_project(x2d, w_bf, b2d, cs):
    nseq = x2d.shape[0]
    args = (x2d, w_bf, b2d, cs)
    return pl.pallas_call(
        _dec_proj_kernel,
        grid=(1,),
        in_specs=[_const_spec(a.shape) for a in args],
        out_specs=[_const_spec((QKV_COLS, nseq)), _const_spec((nseq, CONV_CH))],
        out_shape=[jax.ShapeDtypeStruct((QKV_COLS, nseq), F32),
                   jax.ShapeDtypeStruct((nseq, CONV_CH), F32)],
        compiler_params=_params(("arbitrary",)),
        name="decode_proj",
    )(*args)


def _cache_block(c_ref, nc_ref, ot_ref, lt_ref, qkvt_ref, g, seq, head0):
    hb, w = c_ref.shape[2], c_ref.shape[4]
    nseq = qkvt_ref.shape[1]
    rate = DIL_RATES[g]
    rows = hb * HEAD_DIM
    r0 = pl.multiple_of(head0 * HEAD_DIM, rows)
    mine = lax.broadcasted_iota(jnp.int32, (rows, nseq), 1) == seq

    def column(base):
        x = qkvt_ref[pl.ds(base + r0, rows), :]
        return jnp.sum(jnp.where(mine, x, 0.0), axis=1, keepdims=True).reshape(hb, HEAD_DIM, 1)

    qc = column(g * ATTN_WIDTH)
    kc = column((N_DIL + g) * ATTN_WIDTH)
    vc = column((2 * N_DIL + g) * ATTN_WIDTH)
    kt = c_ref[0, 0]
    vt = c_ref[0, 1]
    tok = lax.broadcasted_iota(jnp.int32, (1, 1, w), 2)
    in_window = (tok & (rate - 1)) == 0
    newest = tok == w - 1

    sc = jnp.where(in_window, jnp.sum(kt * qc, axis=1, keepdims=True), -jnp.inf)
    sn = jnp.sum(qc * kc, axis=1, keepdims=True)
    m = jnp.maximum(jnp.max(sc, axis=2, keepdims=True), sn)
    p = jnp.exp(sc - m)
    pn = jnp.exp(sn - m)
    l = jnp.sum(p, axis=2, keepdims=True) + pn
    inv = 1.0 / l
    o = jnp.sum(vt * (p * inv), axis=2, keepdims=True) + vc * (pn * inv)
    lse = jnp.broadcast_to(m + jnp.log(l), (hb, HEAD_DIM, 1))
    acc_rows = pl.ds(r0, rows)
    ot_ref[acc_rows, :] = jnp.where(mine, o.reshape(rows, 1), ot_ref[acc_rows, :])
    lt_ref[acc_rows, :] = jnp.where(mine, lse.reshape(rows, 1), lt_ref[acc_rows, :])

    def shifted(old, new_col):
        rolled = pltpu.roll(old.reshape(rows, w), w - 1, 1).reshape(hb, HEAD_DIM, w)
        return jnp.where(newest, new_col, rolled)

    nc_ref[0, 0] = shifted(kt, kc)
    nc_ref[0, 1] = shifted(vt, vc)


def _route(hb, rw_ref, rb_ref, ct_ref):
    logits = jnp.dot(hb, rw_ref[...], preferred_element_type=F32)
    lt = logits.T
    comb_rows = _routing_rows([lt[e:e + 1, :] for e in range(N_EXPERTS)], rb_ref)
    ct_ref[...] = jnp.zeros_like(ct_ref)
    for e in range(N_EXPERTS):
        ct_ref[e:e + 1, :] = comb_rows[e]
    return ct_ref[...].T


def _experts_step(hb, comb, k, w1_ref, w3_ref, w2_ref, resident):
    lane = lax.broadcasted_iota(jnp.int32, comb.shape, 1)
    ffn = None
    for ee in range(EXPERTS_PER_SUBSTEP):
        e = k * EXPERTS_PER_SUBSTEP + ee
        we = e if resident else ee
        a = jnp.dot(hb, w1_ref[we], preferred_element_type=F32)
        b = jnp.dot(hb, w3_ref[we], preferred_element_type=F32)
        gate = jnp.sum(jnp.where(lane == e, comb, 0.0), axis=1, keepdims=True)
        gated = (a * jax.nn.sigmoid(a)) * b * gate
        part = jnp.dot(gated.astype(BF16), w2_ref[ee], preferred_element_type=F32)
        ffn = part if ffn is None else ffn + part
    return ffn


def _moe_stream_kernel(*refs, seq_base):
    ng = N_DIL
    (h_ref, rw_ref, rb_ref, w1_ref, w3_ref, w2_ref, lng_ref, lnb_ref, qkvt_ref) = refs[:MOE_FIXED_INPUTS]
    c_refs = refs[MOE_FIXED_INPUTS:MOE_FIXED_INPUTS + ng]
    n_in = MOE_FIXED_INPUTS + ng + (ng if seq_base else 0)
    out_ref = refs[n_in]
    stream_out = refs[n_in + 1:n_in + 1 + 3 * ng]
    comb_ref, ct_ref = refs[n_in + 1 + 3 * ng:]
    i, k = pl.program_id(0), pl.program_id(1)
    flat = i * MOE_SUBSTEPS + k
    hb = h_ref[...].astype(BF16)

    @pl.when(k == 0)
    def _():
        comb_ref[...] = _route(hb, rw_ref, rb_ref, ct_ref)
        out_ref[...] = jnp.zeros_like(out_ref)

    @pl.when(flat == 0)
    def _():
        for s in range(ng):
            for t in range(2):
                acc = stream_out[3 * s + 1 + t]
                acc[...] = jnp.zeros_like(acc)

    out_ref[...] += _experts_step(hb, comb_ref[...], k, w1_ref, w3_ref, w2_ref, resident=True)

    blocks_per_seq = HEADS // CACHE_HEAD_BLOCK
    for g in range(ng):
        _cache_block(c_refs[g], stream_out[3 * g], stream_out[3 * g + 1], stream_out[3 * g + 2], qkvt_ref,
                     g, seq_base + flat // blocks_per_seq, (flat % blocks_per_seq) * CACHE_HEAD_BLOCK)

    @pl.when(k == MOE_SUBSTEPS - 1)
    def _():
        out_ref[...] = _layer_norm(DN_ALPHA * h_ref[...] + out_ref[...], lng_ref[...], lnb_ref[...])


def _moe_stream(h2d, rw_pad, rb, w1, w3, w2, layer, lng, lnb, qkvt, caches_t, tm, seq_base, prev):
    m = h2d.shape[0]
    nseq = qkvt.shape[1]
    blocks_per_seq = HEADS // CACHE_HEAD_BLOCK
    first_block = seq_base * blocks_per_seq
    assert len(caches_t) == N_DIL and ((m // tm) * MOE_SUBSTEPS) % blocks_per_seq == 0
    assert first_block + (m // tm) * MOE_SUBSTEPS <= nseq * blocks_per_seq
    row = lambda i, k: (i, 0)
    w13_spec = pl.BlockSpec((None, N_EXPERTS, D_MODEL, D_EXPERT), lambda i, k: (layer, 0, 0, 0),
                            pipeline_mode=pl.Buffered(1))
    acc_spec = _const_spec((ATTN_WIDTH, nseq))
    acc_sds = jax.ShapeDtypeStruct((ATTN_WIDTH, nseq), F32)

    def block_index(i, k):
        b = first_block + i * MOE_SUBSTEPS + k
        return (b // blocks_per_seq, 0, b % blocks_per_seq, 0, 0)

    cache_specs, stream_specs, stream_shapes = [], [], []
    for c in caches_t:
        spec = pl.BlockSpec((1, 2, CACHE_HEAD_BLOCK, HEAD_DIM, c.shape[4]), block_index)
        cache_specs.append(spec)
        stream_specs += [spec, acc_spec, acc_spec]
        stream_shapes += [jax.ShapeDtypeStruct(c.shape, F32), acc_sds, acc_sds]
    extra_in, extra_specs, aliases = [], [], {}
    if seq_base:
        n_fixed = MOE_FIXED_INPUTS + len(caches_t)
        for s, nc in enumerate(prev):
            aliases[n_fixed + s] = 1 + 3 * s
            extra_in.append(nc)
            extra_specs.append(pl.BlockSpec(memory_space=pl.ANY))
    return pl.pallas_call(
        functools.partial(_moe_stream_kernel, seq_base=seq_base),
        grid=(m // tm, MOE_SUBSTEPS),
        in_specs=[
            pl.BlockSpec((tm, D_MODEL), row),
            _const_spec((D_MODEL, LANES)),
            _const_spec((N_EXPERTS, 1)),
            w13_spec,
            w13_spec,
            pl.BlockSpec((None, EXPERTS_PER_SUBSTEP, D_EXPERT, D_MODEL), lambda i, k: (layer, k, 0, 0)),
            _const_spec((1, D_MODEL)), _const_spec((1, D_MODEL)),
            _resident_spec(qkvt.shape),
        ] + cache_specs + extra_specs,
        out_specs=[pl.BlockSpec((tm, D_MODEL), row)] + stream_specs,
        out_shape=[jax.ShapeDtypeStruct((m, D_MODEL), F32)] + stream_shapes,
        input_output_aliases=aliases,
        scratch_shapes=[pltpu.VMEM((tm, LANES), F32), pltpu.VMEM((LANES, tm), F32)],
        compiler_params=pltpu.CompilerParams(dimension_semantics=("arbitrary", "arbitrary"),
                                             vmem_limit_bytes=STREAM_VMEM_LIMIT_BYTES),
        name=f"moe_stream_from{seq_base}",
    )(h2d, rw_pad, rb, w1, w3, w2, lng, lnb, qkvt, *caches_t, *extra_in)


def _sample_tail_kernel(x_ref, *refs, split):
    acc_refs = refs[:4 * N_DIL]
    (cst_ref, glu_ref, cw_ref, cb_ref, clg_ref, clb_ref, wo_ref, lmg_ref, lmb_ref,
     pst_ref, pw_ref, pb_ref, ps_ref, rw_ref, rb_ref, w1_ref, w3_ref, w2_ref, lfg_ref, lfb_ref,
     y_ref, ncst_ref, npst_ref, h_ref, ffn_ref, comb_ref, ct_ref) = refs[4 * N_DIL:]
    s = pl.program_id(0)
    layer, k = s // MOE_SUBSTEPS, s % MOE_SUBSTEPS

    def start_layer(h):
        h_ref[...] = h
        ffn_ref[...] = jnp.zeros_like(ffn_ref)
        comb_ref[...] = _route(h.astype(BF16), rw_ref, rb_ref, ct_ref)

    @pl.when(s == 0)
    def _():
        npre = CONV_WIDTH - 1
        glu = glu_ref[...]
        acc = cw_ref[npre:npre + 1, :] * glu
        for j in range(npre):
            acc = acc + cw_ref[j:j + 1, :] * cst_ref[j]
        conv = _conv_tail(acc, cb_ref[...], clg_ref[...], clb_ref[...])
        first = lax.broadcasted_iota(jnp.int32, (ATTN_WIDTH, x_ref.shape[0]), 1) < split

        def both(idx):
            return jnp.where(first, acc_refs[idx][...], acc_refs[2 * N_DIL + idx][...])

        os_ = [both(g) for g in range(N_DIL)]
        w = _group_weights([both(N_DIL + g) for g in range(N_DIL)])
        attn = (w[0] * os_[0] + w[1] * os_[1] + w[2] * os_[2]).T
        start_layer(_out_proj_ln(x_ref[...], attn, conv, wo_ref, lmg_ref[0:1, :], lmb_ref[0:1, :]))
        ncst_ref[0:npre - 1] = cst_ref[1:npre]
        ncst_ref[npre - 1] = glu

    @pl.when(s == MOE_SUBSTEPS)
    def _():
        x = h_ref[...]
        parts = []
        for gi, w in enumerate(POOL_WINDOWS):
            lo = gi * POOL_CH
            cur = x[:, lo:lo + POOL_CH]
            tot = cur
            for j in range(1, w):
                tot = tot + pst_ref[POOL_PREFIX - j, :, lo:lo + POOL_CH]
            parts.append(tot / float(w) - cur)
        y = _pool_project(parts, pw_ref, pb_ref, ps_ref)
        start_layer(_layer_norm(DN_ALPHA * x + y, lmg_ref[1:2, :], lmb_ref[1:2, :]))
        npst_ref[0:POOL_PREFIX - 1] = pst_ref[1:POOL_PREFIX]
        npst_ref[POOL_PREFIX - 1] = x

    ffn_ref[...] += _experts_step(h_ref[...].astype(BF16), comb_ref[...], k, w1_ref, w3_ref, w2_ref,
                                  resident=False)

    @pl.when(k == MOE_SUBSTEPS - 1)
    def _():
        out = _layer_norm(DN_ALPHA * h_ref[...] + ffn_ref[...],
                          lfg_ref[pl.ds(layer, 1), :], lfb_ref[pl.ds(layer, 1), :])
        h_ref[...] = out
        y_ref[...] = out


def _sample_tail(x2d, accs, split, conv_t, glu, conv_w, conv_b, clg, clb, wo_bf, ln_mix_g, ln_mix_b,
                 pool_t, pw_bf, pb, ps, rw_pad, rb, w1, w3, w2, ln_ffn_g, ln_ffn_b):
    assert DEPTH == 2
    nseq = x2d.shape[0]
    consts = (x2d, *accs, conv_t, glu, conv_w, conv_b, clg, clb, wo_bf, ln_mix_g, ln_mix_b,
              pool_t, pw_bf, pb, ps, rw_pad, rb)
    step_block = lambda s: (s // MOE_SUBSTEPS, s % MOE_SUBSTEPS, 0, 0)
    w13_spec = pl.BlockSpec((None, EXPERTS_PER_SUBSTEP, D_MODEL, D_EXPERT), step_block)
    w2_spec = pl.BlockSpec((None, EXPERTS_PER_SUBSTEP, D_EXPERT, D_MODEL), step_block)
    return pl.pallas_call(
        functools.partial(_sample_tail_kernel, split=split),
        grid=(DEPTH * MOE_SUBSTEPS,),
        in_specs=([_const_spec(a.shape) for a in consts] + [w13_spec, w13_spec, w2_spec]
                  + [_const_spec(ln_ffn_g.shape), _const_spec(ln_ffn_b.shape)]),
        out_specs=[_const_spec((nseq, D_MODEL)), _const_spec(conv_t.shape), _const_spec(pool_t.shape)],
        out_shape=[jax.ShapeDtypeStruct((nseq, D_MODEL), F32),
                   jax.ShapeDtypeStruct(conv_t.shape, F32),
                   jax.ShapeDtypeStruct(pool_t.shape, F32)],
        scratch_shapes=[pltpu.VMEM((nseq, D_MODEL), F32), pltpu.VMEM((nseq, D_MODEL), F32),
                        pltpu.VMEM((nseq, LANES), F32), pltpu.VMEM((LANES, nseq), F32)],
        compiler_params=_params(("arbitrary",)),
        name="sample_tail",
    )(*consts, w1, w3, w2, ln_ffn_g, ln_ffn_b)


def _token_minor(a):
    return jnp.transpose(a, (0, 2, 3, 4, 1))


def _token_major(a):
    return jnp.transpose(a, (0, 4, 1, 2, 3))


def kernel(x_prompt, x_sample, cache_attn_w128, cache_attn_w512, cache_attn_w2048, state_conv, state_pool,
           w_in, b_in, conv_w, conv_b, conv_ln_g, conv_ln_b, w_out, pool_w, pool_b, pool_scale,
           ln_mix_g, ln_mix_b, ln_ffn_g, ln_ffn_b, router_w, router_bias, moe_w1, moe_w3, moe_w2):
    n, s, d = x_prompt.shape
    nseq = x_sample.shape[0]
    past = cache_attn_w2048.shape[2]
    assert d == D_MODEL and x_sample.shape[1] == 1 and s % ROW_TILE == 0 and s == DIL_WINDOWS[2]
    assert cache_attn_w128.shape[0] == 1 and past == DIL_WINDOWS[2]
    caches = (cache_attn_w128[0], cache_attn_w512[0], cache_attn_w2048[0])

    w_in_bf = w_in[0].astype(BF16)
    w_out_bf = w_out[0].astype(BF16)
    pool_w_bf = pool_w[0].astype(BF16)
    rw_pad = jnp.pad(router_w, ((0, 0), (0, LANES - N_EXPERTS))).astype(BF16)
    rb = router_bias.astype(F32).reshape(N_EXPERTS, 1)
    r2 = lambda v: v.reshape(1, -1)
    b_in2 = r2(b_in[0])
    cb, clg, clb = r2(conv_b[0]), r2(conv_ln_g[0]), r2(conv_ln_b[0])
    ps = r2(pool_scale[0])

    xs = x_sample.reshape(nseq, d)
    ang = _rope_angles(past + jnp.arange(1, dtype=jnp.int32))
    cs = jnp.stack([jnp.broadcast_to(jnp.cos(ang).T, (ROT_HALF, nseq)),
                    jnp.broadcast_to(jnp.sin(ang).T, (ROT_HALF, nseq))])
    qkvt, glus = _decode_project(xs, w_in_bf, b_in2, cs)
    caches_t = [_token_minor(c) for c in caches]

    seqs_per_layer = (n * s // ROW_TILE) * MOE_SUBSTEPS * CACHE_HEAD_BLOCK // HEADS
    assert DEPTH * seqs_per_layer == nseq

    def moe_stream(h2d, layer, prev):
        res = _moe_stream(h2d, rw_pad, rb, *moe_bf, layer, r2(ln_ffn_g[layer]), r2(ln_ffn_b[layer]),
                          qkvt, caches_t, ROW_TILE, layer * seqs_per_layer, prev)
        return res[0], [tuple(res[1 + 3 * g:4 + 3 * g]) for g in range(N_DIL)]

    xp = x_prompt.reshape(n * s, d)
    tabs_p = _rope_tables(jnp.arange(s, dtype=jnp.int32))
    (q0, q1, q2, kv0, kv1, kv2, glu, kt0, kt1, kt2), moe_bf = _project(
        xp, w_in_bf, b_in2, tabs_p, (moe_w1, moe_w3, moe_w2), n, s, ROW_TILE)
    os_, lses = [], []
    for g, (q, kv) in enumerate(((q0, kv0), (q1, kv1), (q2, kv2))):
        o, l = _prompt_attention(q, kv, g, n, s)
        os_.append(o)
        lses.append(l)
    h = _prompt_mix(xp, os_, lses, glu, conv_w[0], cb, clg, clb, w_out_bf,
                    r2(ln_mix_g[0]), r2(ln_mix_b[0]), n, s, ROW_TILE)
    h, first_pass = moe_stream(h, 0, None)
    p_pool = h.reshape(n, s, d)[:, s - POOL_PREFIX:][None]
    h = _prompt_pool(h, pool_w_bf, pool_b[0], ps, r2(ln_mix_g[1]), r2(ln_mix_b[1]), n, s, ROW_TILE)
    h, second_pass = moe_stream(h, 1, [st[0] for st in first_pass])
    y_prompt = h.reshape(n, s, d)
    p_attn = [_token_major(kt.reshape(n, 2, HEADS, HEAD_DIM, kt.shape[2]))[None] for kt in (kt0, kt1, kt2)]
    p_conv = glu.reshape(n, s, CONV_CH)[:, s - (CONV_WIDTH - 1):][None]

    accs = []
    for streamed in (first_pass, second_pass):
        accs += [st[1] for st in streamed] + [st[2] for st in streamed]
    s_attn = [_token_major(st[0])[None] for st in second_pass]
    conv_t = jnp.transpose(state_conv[0], (1, 0, 2))
    pool_t = jnp.transpose(state_pool[0], (1, 0, 2))
    ys, new_conv, new_pool = _sample_tail(
        xs, accs, seqs_per_layer, conv_t, glus, conv_w[0], cb, clg, clb, w_out_bf, ln_mix_g, ln_mix_b,
        pool_t, pool_w_bf, pool_b[0], ps, rw_pad, rb, *moe_bf, ln_ffn_g, ln_ffn_b)
    y_sample = ys.reshape(nseq, 1, d)
    s_conv = jnp.transpose(new_conv, (1, 0, 2))[None]
    s_pool = jnp.transpose(new_pool, (1, 0, 2))[None]

    return (y_prompt, y_sample, p_attn[0], p_attn[1], p_attn[2], p_conv, p_pool,
            s_attn[0], s_attn[1], s_attn[2], s_conv, s_pool)
```

```python
import functools

import jax
import jax.numpy as jnp
from jax import lax
from jax.experimental import pallas as pl
from jax.experimental.pallas import tpu as pltpu

F32 = jnp.float32
BF16 = jnp.bfloat16

D_MODEL = 1024
HEAD_DIM = 64
HEADS = 8
ATTN_WIDTH = HEADS * HEAD_DIM
N_DIL = 3
DIL_WINDOWS = (128, 512, 2048)
DIL_RATES = (1, 4, 16)
ATTN_BLOCK = 128
ATTN_SCALE = HEAD_DIM ** -0.5
ROT_DIM = HEAD_DIM // 4
ROT_HALF = ROT_DIM // 2
ROPE_THETA = 500000.0
QKV_COLS = 3 * N_DIL * ATTN_WIDTH
CONV_CH = D_MODEL // 4
CONV_WIDTH = 31
IN_COLS = QKV_COLS + 2 * CONV_CH
POOL_WINDOWS = (2, 4, 8, 16)
POOL_CH = D_MODEL // len(POOL_WINDOWS)
POOL_PREFIX = max(POOL_WINDOWS) - 1
N_EXPERTS = 16
N_EXPERT_GROUPS = 4
EXPERTS_PER_GROUP = 4
D_EXPERT = 256
DEPTH = 2
DN_ALPHA = (2.0 * DEPTH) ** 0.25
LN_EPS = 1e-5

LANES = 128
SUBLANES = 8
VMEM_LIMIT_BYTES = 56 * 1024 * 1024
ROW_TILE = 512
CONV_HALO = 32
CONV_SLACK = -(-(CONV_HALO - (CONV_WIDTH - 1) + SUBLANES) // SUBLANES) * SUBLANES
POOL_LEVELS = len(POOL_WINDOWS) - 1
POOL_HALO = SUBLANES * (POOL_LEVELS + 1)
assert POOL_WINDOWS[-1] // 2 == SUBLANES and POOL_HALO >= POOL_PREFIX
STREAM_VMEM_LIMIT_BYTES = 60000 * 1024
MOE_SUBSTEPS = 4
EXPERTS_PER_SUBSTEP = N_EXPERTS // MOE_SUBSTEPS
MOE_FIXED_INPUTS = 9
CACHE_HEAD_BLOCK = 4
HEAD_PAIRS = ATTN_WIDTH // LANES
ATTN_UNITS_PER_STAGE = 8

NT_DIMS = (((1,), (1,)), ((), ()))


def _params(sem, vmem_limit_bytes=VMEM_LIMIT_BYTES):
    return pltpu.CompilerParams(dimension_semantics=sem, vmem_limit_bytes=vmem_limit_bytes)


def _const_spec(shape):
    nd = len(shape)
    return pl.BlockSpec(shape, lambda *_: (0,) * nd)


def _resident_spec(shape):
    nd = len(shape)
    return pl.BlockSpec(shape, lambda *_: (0,) * nd, pipeline_mode=pl.Buffered(1))


def _layer_norm(x, g, b):
    mu = jnp.mean(x, axis=-1, keepdims=True)
    xc = x - mu
    var = jnp.mean(xc * xc, axis=-1, keepdims=True)
    return xc * lax.rsqrt(var + LN_EPS) * g + b


def _causal_conv(ext_ref, phase_ref, cw_ref, tm):
    base = CONV_HALO - (CONV_WIDTH - 1)
    span = tm + CONV_SLACK
    acc = None
    for r in range(SUBLANES):
        part = None
        for k in range(r, CONV_WIDTH, SUBLANES):
            term = cw_ref[k:k + 1, :] * ext_ref[k - r:k - r + span, :]
            part = term if part is None else part + term
        slot = r % phase_ref.shape[0]
        phase_ref[slot] = part
        shifted = phase_ref[slot, base + r:base + r + tm, :]
        acc = shifted if acc is None else acc + shifted
    return acc


def _proj_kernel(x_ref, w_ref, b_ref, tab_ref, m1_ref, m3_ref, m2_ref, cw_ref, q0_ref, q1_ref, q2_ref,
                 kv0_ref, kv1_ref, kv2_ref, glu_ref, kt0_ref, kt1_ref, kt2_ref, m1b_ref, m3b_ref, m2b_ref,
                 conv_ref, ext_ref, phase_ref):
    for src, dst in ((m1_ref, m1b_ref), (m3_ref, m3b_ref), (m2_ref, m2b_ref)):
        dst[...] = src[...].astype(BF16)
    tm = x_ref.shape[0]
    first = pl.program_id(1) == 0
    last = pl.program_id(1) == pl.num_programs(1) - 1
    xb = x_ref[...].astype(BF16)
    cosm, sin_lo, sin_hi = tab_ref[0], tab_ref[1], tab_ref[2]

    def proj(c0, width):
        return (jnp.dot(xb, w_ref[:, c0:c0 + width], preferred_element_type=F32)
                + b_ref[:, c0:c0 + width])

    @pl.when(first)
    def _():
        ext_ref[0:CONV_HALO, :] = jnp.zeros((CONV_HALO, CONV_CH), F32)

    @pl.when(jnp.logical_not(first))
    def _():
        ext_ref[0:CONV_HALO, :] = ext_ref[tm:tm + CONV_HALO, :]

    ga = proj(QKV_COLS, CONV_CH)
    gb = proj(QKV_COLS + CONV_CH, CONV_CH)
    glu = ga * jax.nn.sigmoid(gb)
    glu_ref[...] = glu
    ext_ref[CONV_HALO:CONV_HALO + tm, :] = glu
    ext_ref[CONV_HALO + tm:, :] = jnp.zeros((CONV_SLACK, CONV_CH), F32)
    conv_ref[...] = _causal_conv(ext_ref, phase_ref, cw_ref, tm)

    def rope(t):
        parts = []
        for j in range(t.shape[1] // LANES):
            v = t[:, j * LANES:(j + 1) * LANES]
            parts.append(v * cosm
                         + pltpu.roll(v, LANES - ROT_HALF, 1) * sin_lo
                         + pltpu.roll(v, ROT_HALF, 1) * sin_hi)
        return jnp.concatenate(parts, axis=1)

    q_refs = (q0_ref, q1_ref, q2_ref)
    kv_refs = (kv0_ref, kv1_ref, kv2_ref)
    for g in range(N_DIL):
        c = g * ATTN_WIDTH
        q = rope(proj(c, ATTN_WIDTH)) * ATTN_SCALE
        k = rope(proj(N_DIL * ATTN_WIDTH + c, ATTN_WIDTH))
        v = proj(2 * N_DIL * ATTN_WIDTH + c, ATTN_WIDTH)
        for hp in range(HEAD_PAIRS):
            cols = slice(hp * LANES, (hp + 1) * LANES)
            q_refs[g][hp] = q[:, cols]
            kv_refs[g][hp] = k[:, cols]
            kv_refs[g][HEAD_PAIRS + hp] = v[:, cols]
        if g == 2:
            kt2_ref[0, 0:ATTN_WIDTH, :] = k.T
            kt2_ref[0, ATTN_WIDTH:2 * ATTN_WIDTH, :] = v.T

    @pl.when(last)
    def _():
        for kt_ref, kv_ref, keep in ((kt0_ref, kv0_ref, DIL_WINDOWS[0]), (kt1_ref, kv1_ref, DIL_WINDOWS[1])):
            for j in range(2 * HEAD_PAIRS):
                kt_ref[0, j * LANES:(j + 1) * LANES, :] = kv_ref[j, tm - keep:, :].T


def _project(x2d, w_bf, b2d, tables, moe_w, conv_w, n, s, tm):
    assert tm == DIL_WINDOWS[1] and tm >= DIL_WINDOWS[0]
    tps = s // tm
    m = n * s
    n_mats = DEPTH * N_EXPERTS
    assert n_mats % (n * tps) == 0
    per_step = n_mats // (n * tps)
    flat_w = [w.reshape(n_mats, *w.shape[2:]) for w in moe_w]
    w_specs = [pl.BlockSpec((per_step, *w.shape[1:]), lambda b, t: (b * tps + t, 0, 0)) for w in flat_w]
    row = lambda b, t: (b * tps + t, 0)
    seq = lambda b, t: (b, 0, 0)
    kv_rows = 2 * ATTN_WIDTH
    slab = lambda b, t: (0, b * tps + t, 0)
    out_specs = ([pl.BlockSpec((HEAD_PAIRS, tm, LANES), slab)] * N_DIL
                 + [pl.BlockSpec((2 * HEAD_PAIRS, tm, LANES), slab)] * N_DIL
                 + [pl.BlockSpec((tm, CONV_CH), row),
                    pl.BlockSpec((1, kv_rows, DIL_WINDOWS[0]), seq),
                    pl.BlockSpec((1, kv_rows, DIL_WINDOWS[1]), seq),
                    pl.BlockSpec((1, kv_rows, tm), lambda b, t: (b, 0, t))])
    out_shape = ([jax.ShapeDtypeStruct((HEAD_PAIRS, m, LANES), F32)] * N_DIL
                 + [jax.ShapeDtypeStruct((2 * HEAD_PAIRS, m, LANES), F32)] * N_DIL
                 + [jax.ShapeDtypeStruct((m, CONV_CH), F32),
                    jax.ShapeDtypeStruct((n, kv_rows, DIL_WINDOWS[0]), F32),
                    jax.ShapeDtypeStruct((n, kv_rows, DIL_WINDOWS[1]), F32),
                    jax.ShapeDtypeStruct((n, kv_rows, s), F32)])
    res = pl.pallas_call(
        _proj_kernel,
        grid=(n, tps),
        in_specs=[
            pl.BlockSpec((tm, D_MODEL), row),
            _resident_spec((D_MODEL, IN_COLS)),
            _const_spec((1, IN_COLS)),
            pl.BlockSpec((3, tm, LANES), lambda b, t: (0, t, 0)),
        ] + w_specs + [_const_spec((CONV_WIDTH, CONV_CH))],
        out_specs=out_specs + w_specs + [pl.BlockSpec((tm, CONV_CH), row)],
        out_shape=(out_shape + [jax.ShapeDtypeStruct(w.shape, BF16) for w in flat_w]
                   + [jax.ShapeDtypeStruct((m, CONV_CH), F32)]),
        scratch_shapes=[pltpu.VMEM((CONV_HALO + tm + CONV_SLACK, CONV_CH), F32),
                        pltpu.VMEM((2, tm + CONV_SLACK, CONV_CH), F32)],
        compiler_params=_params(("parallel", "arbitrary"), STREAM_VMEM_LIMIT_BYTES),
        name="in_proj",
    )(x2d, w_bf, b2d, tables, *flat_w, conv_w)
    n_proj = len(out_shape)
    moe_bf = [wb.reshape(w.shape) for wb, w in zip(res[n_proj:n_proj + len(moe_w)], moe_w)]
    return res[:n_proj], moe_bf, res[-1]


def _rope_angles(pos):
    inv_freq = ROPE_THETA ** (-jnp.arange(ROT_HALF, dtype=F32) * 2.0 / ROT_DIM)
    return pos.astype(F32)[:, None] * inv_freq[None, :]


def _rope_tables(pos):
    t = pos.shape[0]
    ang = _rope_angles(pos)
    cos, sin = jnp.cos(ang), jnp.sin(ang)
    rest = HEAD_DIM - ROT_DIM
    c64 = jnp.concatenate([cos, cos, jnp.ones((t, rest), F32)], axis=1)
    lo64 = jnp.concatenate([-sin, jnp.zeros((t, HEAD_DIM - ROT_HALF), F32)], axis=1)
    hi64 = jnp.concatenate([jnp.zeros((t, ROT_HALF), F32), sin, jnp.zeros((t, rest), F32)], axis=1)
    rep = LANES // HEAD_DIM
    return jnp.stack([jnp.tile(c64, (1, rep)), jnp.tile(lo64, (1, rep)), jnp.tile(hi64, (1, rep))])


def _attn_kernel(q_ref, kv_ref, bias_ref, o_ref, lse_ref, s_ref, p_ref, *, rate, nblk):
    nk = s_ref.shape[3]
    lane = lax.broadcasted_iota(jnp.int32, (ATTN_BLOCK, LANES), 1)
    low_half = lane < HEAD_DIM
    keep_lo = low_half.astype(BF16)
    keep_hi = 1 - keep_lo
    ones_rhs = jnp.ones((nk, LANES), BF16)

    def strided(start, size):
        return pl.ds(start, size, stride=rate) if rate > 1 else pl.ds(start, size)

    def unit_rows(u):
        res, b = (u // nblk, u % nblk) if nblk > 1 else (u, 0)
        rows_q = strided(b * (ATTN_BLOCK * rate) + res, ATTN_BLOCK)
        rows_k = strided(jnp.maximum(b - 1, 0) * (ATTN_BLOCK * rate) + res, nk)
        return rows_q, rows_k, bias_ref[jnp.minimum(b, 1)]

    def scores(slot, rows_q, rows_k, bias):
        for hp in range(HEAD_PAIRS):
            qp = q_ref[hp, rows_q, :].astype(BF16)
            kp = kv_ref[hp, rows_k, :].astype(BF16)
            for half, keep in enumerate((keep_lo, keep_hi)):
                s = lax.dot_general(qp * keep, kp, NT_DIMS, preferred_element_type=F32)
                s_ref[slot, 2 * hp + half] = s + bias

    def values(slot, rows_q, rows_k, m):
        for hp in range(HEAD_PAIRS):
            vp = kv_ref[HEAD_PAIRS + hp, rows_k, :].astype(BF16)
            rhs = jnp.concatenate([vp, ones_rhs], axis=1)
            ol_lo = jnp.dot(p_ref[slot, 2 * hp], rhs, preferred_element_type=F32)
            ol_hi = jnp.dot(p_ref[slot, 2 * hp + 1], rhs, preferred_element_type=F32)
            l_lo, l_hi = ol_lo[:, LANES:], ol_hi[:, LANES:]
            o_pair = jnp.where(low_half, ol_lo[:, :LANES] * (1.0 / l_lo), ol_hi[:, :LANES] * (1.0 / l_hi))
            lse_pair = jnp.where(low_half, m[2 * hp] + jnp.log(l_lo), m[2 * hp + 1] + jnp.log(l_hi))
            o_ref[hp, rows_q, :] = o_pair
            lse_ref[hp, rows_q, :] = lse_pair

    nslot = s_ref.shape[0]

    def unit_group(i, carry):
        units = [unit_rows(nslot * i + j) for j in range(nslot)]
        for j, u in enumerate(units):
            scores(j, *u)
        sc = s_ref[...]
        m = jnp.max(sc, axis=-1, keepdims=True)
        p_ref[...] = jnp.exp(sc - m).astype(BF16)
        for j, u in enumerate(units):
            values(j, u[0], u[1], m[j])
        return carry

    lax.fori_loop(0, rate * nblk // nslot, unit_group, 0)


def _band_bias(nblk):
    qi = jnp.arange(ATTN_BLOCK)[:, None]
    ci = jnp.arange(ATTN_BLOCK)[None, :]
    causal = jnp.where(ci <= qi, 0.0, -jnp.inf).astype(F32)
    if nblk == 1:
        return jnp.stack([causal, causal])
    band = jnp.where(ci >= qi, 0.0, -jnp.inf).astype(F32)
    closed = jnp.full((ATTN_BLOCK, ATTN_BLOCK), -jnp.inf, F32)
    return jnp.stack([jnp.concatenate([causal, closed], axis=1), jnp.concatenate([band, causal], axis=1)])


def _prompt_attention(q, kv, g, n, s):
    rate = DIL_RATES[g]
    nblk = s // rate // ATTN_BLOCK
    nk = 2 * ATTN_BLOCK if nblk > 1 else ATTN_BLOCK
    out_sds = jax.ShapeDtypeStruct((HEAD_PAIRS, n * s, LANES), F32)
    seq = lambda b: (0, b, 0)
    o_spec = pl.BlockSpec((HEAD_PAIRS, s, LANES), seq)
    return pl.pallas_call(
        functools.partial(_attn_kernel, rate=rate, nblk=nblk),
        grid=(n,),
        in_specs=[
            o_spec,
            pl.BlockSpec((2 * HEAD_PAIRS, s, LANES), seq),
            _const_spec((2, ATTN_BLOCK, nk)),
        ],
        out_specs=[o_spec, o_spec],
        out_shape=[out_sds, out_sds],
        scratch_shapes=[pltpu.VMEM((ATTN_UNITS_PER_STAGE, HEADS, ATTN_BLOCK, nk), F32),
                        pltpu.VMEM((ATTN_UNITS_PER_STAGE, HEADS, ATTN_BLOCK, nk), BF16)],
        compiler_params=_params(("parallel",)),
        name=f"band_attn_g{g}",
    )(q, kv, _band_bias(nblk))


def _group_weights(lses):
    lmax = jnp.maximum(jnp.maximum(lses[0], lses[1]), lses[2])
    es = [jnp.exp(l - lmax) for l in lses]
    inv = 1.0 / (es[0] + es[1] + es[2])
    return [e * inv for e in es]


def _conv_tail(y, cb, clg, clb):
    z = _layer_norm(y + cb, clg, clb)
    return z * jax.nn.sigmoid(z)


def _out_proj_ln(x, attn, conv, wo_ref, lng, lnb):
    y = (jnp.dot(attn.astype(BF16), wo_ref[0:ATTN_WIDTH, :], preferred_element_type=F32)
         + jnp.dot(conv.astype(BF16), wo_ref[ATTN_WIDTH:ATTN_WIDTH + CONV_CH, :],
                   preferred_element_type=F32))
    return _layer_norm(DN_ALPHA * x + y, lng, lnb)


def _mix_kernel(x_ref, o0_ref, o1_ref, o2_ref, l0_ref, l1_ref, l2_ref, cv_ref,
                cb_ref, clg_ref, clb_ref, wo_ref, lng_ref, lnb_ref, h_ref):
    conv = _conv_tail(cv_ref[...], cb_ref[...], clg_ref[...], clb_ref[...])
    pieces = []
    for hp in range(HEAD_PAIRS):
        w = _group_weights((l0_ref[hp], l1_ref[hp], l2_ref[hp]))
        pieces.append(w[0] * o0_ref[hp] + w[1] * o1_ref[hp] + w[2] * o2_ref[hp])
    attn = jnp.concatenate(pieces, axis=1)
    h_ref[...] = _out_proj_ln(x_ref[...], attn, conv, wo_ref, lng_ref[...], lnb_ref[...])


def _prompt_mix(x2d, os_, lses, conv_sum, conv_b, clg, clb, wo_bf, lng, lnb, n, s, tm):
    tps = s // tm
    row = lambda b, t: (b * tps + t, 0)
    aw = pl.BlockSpec((HEAD_PAIRS, tm, LANES), lambda b, t: (0, b * tps + t, 0))
    return pl.pallas_call(
        _mix_kernel,
        grid=(n, tps),
        in_specs=[
            pl.BlockSpec((tm, D_MODEL), row),
            aw, aw, aw, aw, aw, aw,
            pl.BlockSpec((tm, CONV_CH), row),
            _const_spec((1, CONV_CH)), _const_spec((1, CONV_CH)), _const_spec((1, CONV_CH)),
            _const_spec((ATTN_WIDTH + CONV_CH, D_MODEL)),
            _const_spec((1, D_MODEL)), _const_spec((1, D_MODEL)),
        ],
        out_specs=pl.BlockSpec((tm, D_MODEL), row),
        out_shape=jax.ShapeDtypeStruct((n * s, D_MODEL), F32),
        compiler_params=_params(("parallel", "parallel")),
        name="mix_out",
    )(x2d, *os_, *lses, conv_sum, conv_b, clg, clb, wo_bf, lng, lnb)


def _routing_rows(logit_rows, bias_ref):
    m = logit_rows[0]
    for r in logit_rows[1:]:
        m = jnp.maximum(m, r)
    ex = [jnp.exp(r - m) for r in logit_rows]
    tot = ex[0]
    for e in ex[1:]:
        tot = tot + e
    scores = [e / tot for e in ex]
    sel = [scores[e] + bias_ref[e:e + 1, :] for e in range(N_EXPERTS)]
    grp = []
    for g in range(N_EXPERT_GROUPS):
        v = sel[g * EXPERTS_PER_GROUP:(g + 1) * EXPERTS_PER_GROUP]
        best = v[0] + v[1]
        for i in range(EXPERTS_PER_GROUP):
            for j in range(i + 1, EXPERTS_PER_GROUP):
                if (i, j) != (0, 1):
                    best = jnp.maximum(best, v[i] + v[j])
        grp.append(best)
    gmax = grp[0]
    for v in grp[1:]:
        gmax = jnp.maximum(gmax, v)
    taken = None
    in_group = []
    for g in range(N_EXPERT_GROUPS):
        hit = grp[g] == gmax
        if taken is None:
            in_group.append(hit)
            taken = hit
        else:
            in_group.append(jnp.logical_and(hit, jnp.logical_not(taken)))
            taken = jnp.logical_or(taken, hit)
    gates = []
    for e in range(N_EXPERTS):
        g = e // EXPERTS_PER_GROUP
        rank = jnp.zeros_like(sel[e])
        for o in range(g * EXPERTS_PER_GROUP, (g + 1) * EXPERTS_PER_GROUP):
            if o == e:
                continue
            ahead = sel[o] > sel[e]
            if o < e:
                ahead = jnp.logical_or(ahead, sel[o] == sel[e])
            rank = rank + ahead.astype(F32)
        chosen = jnp.logical_and(in_group[g], rank < float(2))
        gates.append(jnp.where(chosen, scores[e], 0.0))
    den = gates[0]
    for v in gates[1:]:
        den = den + v
    return [v / den for v in gates]


def _pool_project(parts, pw_ref, pb_ref, ps_ref):
    cols = []
    for gi in range(len(POOL_WINDOWS)):
        lo = gi * POOL_CH
        y = jnp.dot(parts[gi].astype(BF16), pw_ref[gi], preferred_element_type=F32)
        cols.append((y + pb_ref[gi:gi + 1, :]) * ps_ref[:, lo:lo + POOL_CH])
    return jnp.concatenate(cols, axis=1)


def _pool_kernel(x_ref, xp_ref, pw_ref, pb_ref, ps_ref, lng_ref, lnb_ref, out_ref, ext_ref, lv_ref):
    tm = x_ref.shape[0]
    t = pl.program_id(1)
    ext_ref[0:POOL_HALO, :] = jnp.where(t == 0, 0.0, xp_ref[...])
    ext_ref[POOL_HALO:POOL_HALO + tm, :] = x_ref[...]
    end = POOL_HALO + tm
    top = POOL_LEVELS - 1
    src = ext_ref
    for i, w in enumerate(POOL_WINDOWS[:POOL_LEVELS]):
        r0, c0, shift = SUBLANES * (i + 1), i * POOL_CH, w // 2
        lv_ref[i, r0:end, c0:] = src[r0:end, c0:] + src[r0 - shift:end - shift, c0:]
        src = lv_ref.at[i]
    pos = t * tm + lax.broadcasted_iota(jnp.int32, (tm, 1), 0)
    parts = []
    for gi, w in enumerate(POOL_WINDOWS):
        lo = gi * POOL_CH
        cur = x_ref[:, lo:lo + POOL_CH]
        if gi <= top:
            tot = lv_ref[gi, POOL_HALO:end, lo:lo + POOL_CH]
        else:
            tot = (lv_ref[top, POOL_HALO:end, lo:lo + POOL_CH]
                   + lv_ref[top, POOL_HALO - w // 2:end - w // 2, lo:lo + POOL_CH])
        cnt = jnp.minimum(w, pos + 1).astype(F32)
        parts.append(tot / cnt - cur)
    y = _pool_project(parts, pw_ref, pb_ref, ps_ref)
    out_ref[...] = _layer_norm(DN_ALPHA * x_ref[...] + y, lng_ref[...], lnb_ref[...])


def _prompt_pool(h2d, pw_bf, pb, ps, lng, lnb, n, s, tm):
    tps = s // tm
    hb = tm // POOL_HALO
    row = lambda b, t: (b * tps + t, 0)
    ng = len(POOL_WINDOWS)
    return pl.pallas_call(
        _pool_kernel,
        grid=(n, tps),
        in_specs=[
            pl.BlockSpec((tm, D_MODEL), row),
            pl.BlockSpec((POOL_HALO, D_MODEL),
                         lambda b, t: (jnp.maximum((b * tps + t) * hb - 1, 0), 0)),
            _const_spec((ng, POOL_CH, POOL_CH)),
            _const_spec((ng, POOL_CH)),
            _const_spec((1, D_MODEL)), _const_spec((1, D_MODEL)), _const_spec((1, D_MODEL)),
        ],
        out_specs=pl.BlockSpec((tm, D_MODEL), row),
        out_shape=jax.ShapeDtypeStruct((n * s, D_MODEL), F32),
        scratch_shapes=[pltpu.VMEM((POOL_HALO + tm, D_MODEL), F32),
                        pltpu.VMEM((POOL_LEVELS, POOL_HALO + tm, D_MODEL), F32)],
        compiler_params=_params(("parallel", "parallel")),
        name="pool_mix",
    )(h2d, h2d, pw_bf, pb, ps, lng, lnb)


def _dec_proj_kernel(x_ref, w_ref, b_ref, cs_ref, qkvt_ref, glu_ref):
    xb = x_ref[...].astype(BF16)
    cos, sin = cs_ref[0], cs_ref[1]
    chunk = ATTN_WIDTH

    def proj(c0, width):
        return (jnp.dot(xb, w_ref[:, c0:c0 + width], preferred_element_type=F32)
                + b_ref[:, c0:c0 + width])

    for ci in range(QKV_COLS // chunk):
        pt = proj(ci * chunk, chunk).T
        if ci < 2 * N_DIL:
            pieces = []
            for hh in range(HEADS):
                base = hh * HEAD_DIM
                x1 = pt[base:base + ROT_HALF, :]
                x2 = pt[base + ROT_HALF:base + ROT_DIM, :]
                pieces += [x1 * cos - x2 * sin, x2 * cos + x1 * sin, pt[base + ROT_DIM:base + HEAD_DIM, :]]
            pt = jnp.concatenate(pieces, axis=0)
            if ci < N_DIL:
                pt = pt * ATTN_SCALE
        qkvt_ref[ci * chunk:(ci + 1) * chunk, :] = pt
    ga = proj(QKV_COLS, CONV_CH)
    gb = proj(QKV_COLS + CONV_CH, CONV_CH)
    glu_ref[...] = ga * jax.nn.sigmoid(gb)


def _decode_project(x2d, w_bf, b2d, cs):
    nseq = x2d.shape[0]
    args = (x2d, w_bf, b2d, cs)
    return pl.pallas_call(
        _dec_proj_kernel,
        grid=(1,),
        in_specs=[_const_spec(a.shape) for a in args],
        out_specs=[_const_spec((QKV_COLS, nseq)), _const_spec((nseq, CONV_CH))],
        out_shape=[jax.ShapeDtypeStruct((QKV_COLS, nseq), F32),
                   jax.ShapeDtypeStruct((nseq, CONV_CH), F32)],
        compiler_params=_params(("arbitrary",)),
        name="decode_proj",
    )(*args)


def _cache_block(c_ref, nc_ref, ot_ref, lt_ref, qkvt_ref, g, seq, head0):
    hb, w = c_ref.shape[2], c_ref.shape[4]
    nseq = qkvt_ref.shape[1]
    rate = DIL_RATES[g]
    rows = hb * HEAD_DIM
    r0 = pl.multiple_of(head0 * HEAD_DIM, rows)
    mine = lax.broadcasted_iota(jnp.int32, (rows, nseq), 1) == seq

    def column(base):
        x = qkvt_ref[pl.ds(base + r0, rows), :]
        return jnp.sum(jnp.where(mine, x, 0.0), axis=1, keepdims=True).reshape(hb, HEAD_DIM, 1)

    qc = column(g * ATTN_WIDTH)
    kc = column((N_DIL + g) * ATTN_WIDTH)
    vc = column((2 * N_DIL + g) * ATTN_WIDTH)
    kt = c_ref[0, 0]
    vt = c_ref[0, 1]
    tok = lax.broadcasted_iota(jnp.int32, (1, 1, w), 2)
    in_window = (tok & (rate - 1)) == 0
    newest = tok == w - 1

    sc = jnp.where(in_window, jnp.sum(kt * qc, axis=1, keepdims=True), -jnp.inf)
    sn = jnp.sum(qc * kc, axis=1, keepdims=True)
    m = jnp.maximum(jnp.max(sc, axis=2, keepdims=True), sn)
    p = jnp.exp(sc - m)
    pn = jnp.exp(sn - m)
    l = jnp.sum(p, axis=2, keepdims=True) + pn
    inv = 1.0 / l
    o = jnp.sum(vt * (p * inv), axis=2, keepdims=True) + vc * (pn * inv)
    lse = jnp.broadcast_to(m + jnp.log(l), (hb, HEAD_DIM, 1))
    acc_rows = pl.ds(r0, rows)
    ot_ref[acc_rows, :] = jnp.where(mine, o.reshape(rows, 1), ot_ref[acc_rows, :])
    lt_ref[acc_rows, :] = jnp.where(mine, lse.reshape(rows, 1), lt_ref[acc_rows, :])

    def shifted(old, new_col):
        rolled = pltpu.roll(old.reshape(rows, w), w - 1, 1).reshape(hb, HEAD_DIM, w)
        return jnp.where(newest, new_col, rolled)

    nc_ref[0, 0] = shifted(kt, kc)
    nc_ref[0, 1] = shifted(vt, vc)


def _route(hb, rw_ref, rb_ref, ct_ref):
    logits = jnp.dot(hb, rw_ref[...], preferred_element_type=F32)
    lt = logits.T
    comb_rows = _routing_rows([lt[e:e + 1, :] for e in range(N_EXPERTS)], rb_ref)
    ct_ref[...] = jnp.zeros_like(ct_ref)
    for e in range(N_EXPERTS):
        ct_ref[e:e + 1, :] = comb_rows[e]
    return ct_ref[...].T


def _experts_step(hb, comb, k, w1_ref, w3_ref, w2_ref, resident):
    lane = lax.broadcasted_iota(jnp.int32, comb.shape, 1)
    ffn = None
    for ee in range(EXPERTS_PER_SUBSTEP):
        e = k * EXPERTS_PER_SUBSTEP + ee
        we = e if resident else ee
        a = jnp.dot(hb, w1_ref[we], preferred_element_type=F32)
        b = jnp.dot(hb, w3_ref[we], preferred_element_type=F32)
        gate = jnp.sum(jnp.where(lane == e, comb, 0.0), axis=1, keepdims=True)
        gated = (a * jax.nn.sigmoid(a)) * b * gate
        part = jnp.dot(gated.astype(BF16), w2_ref[ee], preferred_element_type=F32)
        ffn = part if ffn is None else ffn + part
    return ffn


def _moe_stream_kernel(*refs, seq_base):
    ng = N_DIL
    (h_ref, rw_ref, rb_ref, w1_ref, w3_ref, w2_ref, lng_ref, lnb_ref, qkvt_ref) = refs[:MOE_FIXED_INPUTS]
    c_refs = refs[MOE_FIXED_INPUTS:MOE_FIXED_INPUTS + ng]
    n_in = MOE_FIXED_INPUTS + ng + (ng if seq_base else 0)
    out_ref = refs[n_in]
    stream_out = refs[n_in + 1:n_in + 1 + 3 * ng]
    comb_ref, ct_ref = refs[n_in + 1 + 3 * ng:]
    i, k = pl.program_id(0), pl.program_id(1)
    flat = i * MOE_SUBSTEPS + k
    hb = h_ref[...].astype(BF16)

    @pl.when(k == 0)
    def _():
        comb_ref[...] = _route(hb, rw_ref, rb_ref, ct_ref)
        out_ref[...] = jnp.zeros_like(out_ref)

    @pl.when(flat == 0)
    def _():
        for s in range(ng):
            for t in range(2):
                acc = stream_out[3 * s + 1 + t]
                acc[...] = jnp.zeros_like(acc)

    out_ref[...] += _experts_step(hb, comb_ref[...], k, w1_ref, w3_ref, w2_ref, resident=True)

    blocks_per_seq = HEADS // CACHE_HEAD_BLOCK
    for g in range(ng):
        _cache_block(c_refs[g], stream_out[3 * g], stream_out[3 * g + 1], stream_out[3 * g + 2], qkvt_ref,
                     g, seq_base + flat // blocks_per_seq, (flat % blocks_per_seq) * CACHE_HEAD_BLOCK)

    @pl.when(k == MOE_SUBSTEPS - 1)
    def _():
        out_ref[...] = _layer_norm(DN_ALPHA * h_ref[...] + out_ref[...], lng_ref[...], lnb_ref[...])


def _moe_stream(h2d, rw_pad, rb, w1, w3, w2, layer, lng, lnb, qkvt, caches_t, tm, seq_base, prev):
    m = h2d.shape[0]
    nseq = qkvt.shape[1]
    blocks_per_seq = HEADS // CACHE_HEAD_BLOCK
    first_block = seq_base * blocks_per_seq
    assert len(caches_t) == N_DIL and ((m // tm) * MOE_SUBSTEPS) % blocks_per_seq == 0
    assert first_block + (m // tm) * MOE_SUBSTEPS <= nseq * blocks_per_seq
    row = lambda i, k: (i, 0)
    w13_spec = pl.BlockSpec((None, N_EXPERTS, D_MODEL, D_EXPERT), lambda i, k: (layer, 0, 0, 0),
                            pipeline_mode=pl.Buffered(1))
    acc_spec = _const_spec((ATTN_WIDTH, nseq))
    acc_sds = jax.ShapeDtypeStruct((ATTN_WIDTH, nseq), F32)

    def block_index(i, k):
        b = first_block + i * MOE_SUBSTEPS + k
        return (b // blocks_per_seq, 0, b % blocks_per_seq, 0, 0)

    cache_specs, stream_specs, stream_shapes = [], [], []
    for c in caches_t:
        spec = pl.BlockSpec((1, 2, CACHE_HEAD_BLOCK, HEAD_DIM, c.shape[4]), block_index)
        cache_specs.append(spec)
        stream_specs += [spec, acc_spec, acc_spec]
        stream_shapes += [jax.ShapeDtypeStruct(c.shape, F32), acc_sds, acc_sds]
    extra_in, extra_specs, aliases = [], [], {}
    if seq_base:
        n_fixed = MOE_FIXED_INPUTS + len(caches_t)
        for s, nc in enumerate(prev):
            aliases[n_fixed + s] = 1 + 3 * s
            extra_in.append(nc)
            extra_specs.append(pl.BlockSpec(memory_space=pl.ANY))
    return pl.pallas_call(
        functools.partial(_moe_stream_kernel, seq_base=seq_base),
        grid=(m // tm, MOE_SUBSTEPS),
        in_specs=[
            pl.BlockSpec((tm, D_MODEL), row),
            _const_spec((D_MODEL, LANES)),
            _const_spec((N_EXPERTS, 1)),
            w13_spec,
            w13_spec,
            pl.BlockSpec((None, EXPERTS_PER_SUBSTEP, D_EXPERT, D_MODEL), lambda i, k: (layer, k, 0, 0)),
            _const_spec((1, D_MODEL)), _const_spec((1, D_MODEL)),
            _resident_spec(qkvt.shape),
        ] + cache_specs + extra_specs,
        out_specs=[pl.BlockSpec((tm, D_MODEL), row)] + stream_specs,
        out_shape=[jax.ShapeDtypeStruct((m, D_MODEL), F32)] + stream_shapes,
        input_output_aliases=aliases,
        scratch_shapes=[pltpu.VMEM((tm, LANES), F32), pltpu.VMEM((LANES, tm), F32)],
        compiler_params=pltpu.CompilerParams(dimension_semantics=("arbitrary", "arbitrary"),
                                             vmem_limit_bytes=STREAM_VMEM_LIMIT_BYTES),
        name=f"moe_stream_from{seq_base}",
    )(h2d, rw_pad, rb, w1, w3, w2, lng, lnb, qkvt, *caches_t, *extra_in)


def _sample_tail_kernel(x_ref, *refs, split):
    acc_refs = refs[:4 * N_DIL]
    (cst_ref, glu_ref, cw_ref, cb_ref, clg_ref, clb_ref, wo_ref, lmg_ref, lmb_ref,
     pst_ref, pw_ref, pb_ref, ps_ref, rw_ref, rb_ref, w1_ref, w3_ref, w2_ref, lfg_ref, lfb_ref,
     y_ref, ncst_ref, npst_ref, h_ref, ffn_ref, comb_ref, ct_ref) = refs[4 * N_DIL:]
    s = pl.program_id(0)
    layer, k = s // MOE_SUBSTEPS, s % MOE_SUBSTEPS

    def start_layer(h):
        h_ref[...] = h
        ffn_ref[...] = jnp.zeros_like(ffn_ref)
        comb_ref[...] = _route(h.astype(BF16), rw_ref, rb_ref, ct_ref)

    @pl.when(s == 0)
    def _():
        npre = CONV_WIDTH - 1
        glu = glu_ref[...]
        acc = cw_ref[npre:npre + 1, :] * glu
        for j in range(npre):
            acc = acc + cw_ref[j:j + 1, :] * cst_ref[j]
        conv = _conv_tail(acc, cb_ref[...], clg_ref[...], clb_ref[...])
        first = lax.broadcasted_iota(jnp.int32, (ATTN_WIDTH, x_ref.shape[0]), 1) < split

        def both(idx):
            return jnp.where(first, acc_refs[idx][...], acc_refs[2 * N_DIL + idx][...])

        os_ = [both(g) for g in range(N_DIL)]
        w = _group_weights([both(N_DIL + g) for g in range(N_DIL)])
        attn = (w[0] * os_[0] + w[1] * os_[1] + w[2] * os_[2]).T
        start_layer(_out_proj_ln(x_ref[...], attn, conv, wo_ref, lmg_ref[0:1, :], lmb_ref[0:1, :]))
        ncst_ref[0:npre - 1] = cst_ref[1:npre]
        ncst_ref[npre - 1] = glu

    @pl.when(s == MOE_SUBSTEPS)
    def _():
        x = h_ref[...]
        parts = []
        for gi, w in enumerate(POOL_WINDOWS):
            lo = gi * POOL_CH
            cur = x[:, lo:lo + POOL_CH]
            tot = cur
            for j in range(1, w):
                tot = tot + pst_ref[POOL_PREFIX - j, :, lo:lo + POOL_CH]
            parts.append(tot / float(w) - cur)
        y = _pool_project(parts, pw_ref, pb_ref, ps_ref)
        start_layer(_layer_norm(DN_ALPHA * x + y, lmg_ref[1:2, :], lmb_ref[1:2, :]))
        npst_ref[0:POOL_PREFIX - 1] = pst_ref[1:POOL_PREFIX]
        npst_ref[POOL_PREFIX - 1] = x

    ffn_ref[...] += _experts_step(h_ref[...].astype(BF16), comb_ref[...], k, w1_ref, w3_ref, w2_ref,
                                  resident=False)

    @pl.when(k == MOE_SUBSTEPS - 1)
    def _():
        out = _layer_norm(DN_ALPHA * h_ref[...] + ffn_ref[...],
                          lfg_ref[pl.ds(layer, 1), :], lfb_ref[pl.ds(layer, 1), :])
        h_ref[...] = out
        y_ref[...] = out


def _sample_tail(x2d, accs, split, conv_t, glu, conv_w, conv_b, clg, clb, wo_bf, ln_mix_g, ln_mix_b,
                 pool_t, pw_bf, pb, ps, rw_pad, rb, w1, w3, w2, ln_ffn_g, ln_ffn_b):
    assert DEPTH == 2
    nseq = x2d.shape[0]
    consts = (x2d, *accs, conv_t, glu, conv_w, conv_b, clg, clb, wo_bf, ln_mix_g, ln_mix_b,
              pool_t, pw_bf, pb, ps, rw_pad, rb)
    step_block = lambda s: (s // MOE_SUBSTEPS, s % MOE_SUBSTEPS, 0, 0)
    w13_spec = pl.BlockSpec((None, EXPERTS_PER_SUBSTEP, D_MODEL, D_EXPERT), step_block)
    w2_spec = pl.BlockSpec((None, EXPERTS_PER_SUBSTEP, D_EXPERT, D_MODEL), step_block)
    return pl.pallas_call(
        functools.partial(_sample_tail_kernel, split=split),
        grid=(DEPTH * MOE_SUBSTEPS,),
        in_specs=([_const_spec(a.shape) for a in consts] + [w13_spec, w13_spec, w2_spec]
                  + [_const_spec(ln_ffn_g.shape), _const_spec(ln_ffn_b.shape)]),
        out_specs=[_const_spec((nseq, D_MODEL)), _const_spec(conv_t.shape), _const_spec(pool_t.shape)],
        out_shape=[jax.ShapeDtypeStruct((nseq, D_MODEL), F32),
                   jax.ShapeDtypeStruct(conv_t.shape, F32),
                   jax.ShapeDtypeStruct(pool_t.shape, F32)],
        scratch_shapes=[pltpu.VMEM((nseq, D_MODEL), F32), pltpu.VMEM((nseq, D_MODEL), F32),
                        pltpu.VMEM((nseq, LANES), F32), pltpu.VMEM((LANES, nseq), F32)],
        compiler_params=_params(("arbitrary",)),
        name="sample_tail",
    )(*consts, w1, w3, w2, ln_ffn_g, ln_ffn_b)


def _token_minor(a):
    return jnp.transpose(a, (0, 2, 3, 4, 1))


def _token_major(a):
    return jnp.transpose(a, (0, 4, 1, 2, 3))


def kernel(x_prompt, x_sample, cache_attn_w128, cache_attn_w512, cache_attn_w2048, state_conv, state_pool,
           w_in, b_in, conv_w, conv_b, conv_ln_g, conv_ln_b, w_out, pool_w, pool_b, pool_scale,
           ln_mix_g, ln_mix_b, ln_ffn_g, ln_ffn_b, router_w, router_bias, moe_w1, moe_w3, moe_w2):
    n, s, d = x_prompt.shape
    nseq = x_sample.shape[0]
    past = cache_attn_w2048.shape[2]
    assert d == D_MODEL and x_sample.shape[1] == 1 and s % ROW_TILE == 0 and s == DIL_WINDOWS[2]
    assert cache_attn_w128.shape[0] == 1 and past == DIL_WINDOWS[2]
    caches = (cache_attn_w128[0], cache_attn_w512[0], cache_attn_w2048[0])

    w_in_bf = w_in[0].astype(BF16)
    w_out_bf = w_out[0].astype(BF16)
    pool_w_bf = pool_w[0].astype(BF16)
    rw_pad = jnp.pad(router_w, ((0, 0), (0, LANES - N_EXPERTS))).astype(BF16)
    rb = router_bias.astype(F32).reshape(N_EXPERTS, 1)
    r2 = lambda v: v.reshape(1, -1)
    b_in2 = r2(b_in[0])
    cb, clg, clb = r2(conv_b[0]), r2(conv_ln_g[0]), r2(conv_ln_b[0])
    ps = r2(pool_scale[0])

    xs = x_sample.reshape(nseq, d)
    ang = _rope_angles(past + jnp.arange(1, dtype=jnp.int32))
    cs = jnp.stack([jnp.broadcast_to(jnp.cos(ang).T, (ROT_HALF, nseq)),
                    jnp.broadcast_to(jnp.sin(ang).T, (ROT_HALF, nseq))])
    qkvt, glus = _decode_project(xs, w_in_bf, b_in2, cs)
    caches_t = [_token_minor(c) for c in caches]

    seqs_per_layer = (n * s // ROW_TILE) * MOE_SUBSTEPS * CACHE_HEAD_BLOCK // HEADS
    assert DEPTH * seqs_per_layer == nseq

    def moe_stream(h2d, layer, prev):
        res = _moe_stream(h2d, rw_pad, rb, *moe_bf, layer, r2(ln_ffn_g[layer]), r2(ln_ffn_b[layer]),
                          qkvt, caches_t, ROW_TILE, layer * seqs_per_layer, prev)
        return res[0], [tuple(res[1 + 3 * g:4 + 3 * g]) for g in range(N_DIL)]

    xp = x_prompt.reshape(n * s, d)
    tabs_p = _rope_tables(jnp.arange(s, dtype=jnp.int32))
    (q0, q1, q2, kv0, kv1, kv2, glu, kt0, kt1, kt2), moe_bf, conv_sum = _project(
        xp, w_in_bf, b_in2, tabs_p, (moe_w1, moe_w3, moe_w2), conv_w[0], n, s, ROW_TILE)
    os_, lses = [], []
    for g, (q, kv) in enumerate(((q0, kv0), (q1, kv1), (q2, kv2))):
        o, l = _prompt_attention(q, kv, g, n, s)
        os_.append(o)
        lses.append(l)
    h = _prompt_mix(xp, os_, lses, conv_sum, cb, clg, clb, w_out_bf,
                    r2(ln_mix_g[0]), r2(ln_mix_b[0]), n, s, ROW_TILE)
    h, first_pass = moe_stream(h, 0, None)
    p_pool = h.reshape(n, s, d)[:, s - POOL_PREFIX:][None]
    h = _prompt_pool(h, pool_w_bf, pool_b[0], ps, r2(ln_mix_g[1]), r2(ln_mix_b[1]), n, s, ROW_TILE)
    h, second_pass = moe_stream(h, 1, [st[0] for st in first_pass])
    y_prompt = h.reshape(n, s, d)
    p_attn = [_token_major(kt.reshape(n, 2, HEADS, HEAD_DIM, kt.shape[2]))[None] for kt in (kt0, kt1, kt2)]
    p_conv = glu.reshape(n, s, CONV_CH)[:, s - (CONV_WIDTH - 1):][None]

    accs = []
    for streamed in (first_pass, second_pass):
        accs += [st[1] for st in streamed] + [st[2] for st in streamed]
    s_attn = [_token_major(st[0])[None] for st in second_pass]
    conv_t = jnp.transpose(state_conv[0], (1, 0, 2))
    pool_t = jnp.transpose(state_pool[0], (1, 0, 2))
    ys, new_conv, new_pool = _sample_tail(
        xs, accs, seqs_per_layer, conv_t, glus, conv_w[0], cb, clg, clb, w_out_bf, ln_mix_g, ln_mix_b,
        pool_t, pool_w_bf, pool_b[0], ps, rw_pad, rb, *moe_bf, ln_ffn_g, ln_ffn_b)
    y_sample = ys.reshape(nseq, 1, d)
    s_conv = jnp.transpose(new_conv, (1, 0, 2))[None]
    s_pool = jnp.transpose(new_pool, (1, 0, 2))[None]

    return (y_prompt, y_sample, p_attn[0], p_attn[1], p_attn[2], p_conv, p_pool,
            s_attn[0], s_attn[1], s_attn[2], s_conv, s_pool)
```
